```python
import jax, jax.numpy as jnp
from jax import lax
import numpy as np

D_MODEL = 1024
BATCH = 8
SEQ = 4096
DEPTH = 2

PLE_DIM = 256
HEAD_DIM = 64
RWKV_WIDTH = D_MODEL // 4
RWKV_HEADS = RWKV_WIDTH // HEAD_DIM
RWKV_W_RANK = 64
RWKV_A_RANK = 64
RWKV_V_RANK = 32
RWKV_G_RANK = 128
RWKV_GN_EPS = 64e-5
MOBA_WIDTH = D_MODEL // 2
MOBA_HEADS = MOBA_WIDTH // HEAD_DIM
MOBA_BLOCK = 256
MOBA_TOPK = 3
MOBA_QBLOCK = 64
MLSTM_WIDTH = D_MODEL // 4
MLSTM_HEADS = MLSTM_WIDTH // HEAD_DIM
MLSTM_CHUNK = 64
MLSTM_CONV = 4
D_FF = ((8 * D_MODEL // 3 + 127) // 128) * 128
FFN_CONV = 3
NORM_EPS = 1e-6
MASK_VALUE = -1e30
RWKV_SLAB = 3 * RWKV_WIDTH + RWKV_W_RANK + RWKV_A_RANK + RWKV_G_RANK
MOBA_SLAB = 3 * MOBA_WIDTH
MLSTM_SLAB = 4 * MLSTM_WIDTH + 2 * MLSTM_HEADS
GATE_SLAB = 3 * D_MODEL
IN_WIDTH = RWKV_SLAB + MOBA_SLAB + MLSTM_SLAB + GATE_SLAB

kernel_name = 'hybrid_rwkv7_moba_mlstm_convffn_block'

F32 = jnp.float32


def rms_norm(x, g):
    xf = x.astype(F32)
    y = xf * lax.rsqrt(jnp.mean(xf * xf, axis=-1, keepdims=True) + NORM_EPS)
    return (y * g).astype(x.dtype)


def token_shift(x):
    return jnp.pad(x, ((0, 0), (1, 0), (0, 0)))[:, :-1]


def causal_dwconv(x, w, b):
    k_w, s = w.shape[0], x.shape[1]
    xp = jnp.pad(x, ((0, 0), (k_w - 1, 0), (0, 0)))
    out = b + w[k_w - 1] * x
    for j in range(k_w - 1):
        out = out + w[j] * xp[:, j:j + s]
    return out


def split_cols(t, sizes):
    return jnp.split(t, np.cumsum(sizes)[:-1].tolist(), axis=-1)


def rwkv7_time_mix(slab, mu, w0, w2, a0, a2, g2, k_k, k_a, r_k, gn_g, gn_b,
                   v_first=None, v0=None, v1=None, v2=None):
    B, S, _ = slab.shape
    H, N = RWKV_HEADS, HEAD_DIM
    dt = slab.dtype
    slab = slab + mu * (token_shift(slab) - slab)
    r, k, v, xw, xa, xg = split_cols(
        slab, [RWKV_WIDTH] * 3 + [RWKV_W_RANK, RWKV_A_RANK, RWKV_G_RANK])
    w = -jax.nn.softplus(-(w0 + jnp.tanh(xw) @ w2)) - 0.5
    a = jax.nn.sigmoid(a0 + xa @ a2)
    g = jax.nn.sigmoid(xg) @ g2
    if v_first is not None:
        v = v + (v_first - v) * jax.nn.sigmoid(v0 + (v @ v1) @ v2)

    def heads(t):
        return t.astype(F32).reshape(B, S, H, N)

    kk = heads(k * k_k)
    kk = kk / jnp.maximum(jnp.sqrt(jnp.sum(kk * kk, axis=-1, keepdims=True)), 1e-12)
    k = k * (1.0 + (a - 1.0) * k_a)
    rh, kh, vh, ah = heads(r), heads(k), heads(v), heads(a)
    decay = jnp.exp(-jnp.exp(heads(w)))

    def step(state, inp):
        r_t, w_t, k_t, v_t, kk_t, a_t = inp
        sa = jnp.einsum('bhvk,bhk->bhv', state, -kk_t)
        state = (state * w_t[:, :, None, :] + sa[..., None] * (kk_t * a_t)[:, :, None, :]
                 + v_t[..., None] * k_t[:, :, None, :])
        return state, jnp.einsum('bhvk,bhk->bhv', state, r_t)

    def tm(t):
        return jnp.moveaxis(t, 1, 0)

    _, y = lax.scan(step, jnp.zeros((B, H, N, N), F32),
                    (tm(rh), tm(decay), tm(kh), tm(vh), tm(kk), tm(ah)))
    y = jnp.moveaxis(y, 0, 1)
    m = jnp.mean(y, axis=-1, keepdims=True)
    var = jnp.mean(jnp.square(y - m), axis=-1, keepdims=True)
    y = ((y - m) * lax.rsqrt(var + RWKV_GN_EPS)).reshape(B, S, RWKV_WIDTH) * gn_g + gn_b
    bonus = jnp.sum(rh * kh * r_k.astype(F32).reshape(H, N), axis=-1, keepdims=True) * vh
    y = (y + bonus.reshape(B, S, RWKV_WIDTH)) * g
    return y.astype(dt), v


def moba_attention(mq, mk, mv):
    B, S, _ = mq.shape
    H, dh, BL, QB = MOBA_HEADS, HEAD_DIM, MOBA_BLOCK, MOBA_QBLOCK
    dt = mq.dtype
    scale = HEAD_DIM ** -0.5

    def heads(t):
        return t.reshape(B, S, H, dh).transpose(0, 2, 1, 3)

    q, k, v = heads(mq), heads(mk), heads(mv)
    nb = -(-S // BL)
    pad = ((0, 0), (0, 0), (0, nb * BL - S), (0, 0))
    kb = jnp.pad(k, pad).reshape(B, H, nb, BL, dh)
    vb = jnp.pad(v, pad).reshape(B, H, nb, BL, dh)
    n_sel = min(MOBA_TOPK, nb - 1)
    nq = S // QB

    def to_qb(t):
        return jnp.moveaxis(t.reshape(B, H, nq, QB, *t.shape[3:]), 2, 0)

    xs = (to_qb(q), jnp.arange(nq, dtype=jnp.int32) * QB)
    if n_sel > 0:
        kmean = jnp.mean(kb.astype(F32), axis=3)
        bscore = jnp.einsum('bhsd,bhnd->bhsn', q.astype(F32), kmean)
        past = jnp.arange(nb)[None, :] < (jnp.arange(S) // BL)[:, None]
        bscore = jnp.where(past, bscore, MASK_VALUE)
        _, sel = lax.top_k(bscore, n_sel)
        xs = xs + (to_qb(sel),)

    gather = jax.vmap(jax.vmap(lambda blocks, idx: blocks[idx]))

    def attend(args):
        qc, start = args[0], args[1]
        ob = start // BL
        k_own = lax.dynamic_index_in_dim(kb, ob, axis=2, keepdims=False)
        v_own = lax.dynamic_index_in_dim(vb, ob, axis=2, keepdims=False)
        qpos = start + jnp.arange(QB)
        kpos = ob * BL + jnp.arange(BL)
        s_own = jnp.einsum('bhqd,bhkd->bhqk', qc, k_own).astype(F32) * scale
        s_own = jnp.where(kpos[None, :] <= qpos[:, None], s_own, MASK_VALUE)
        if n_sel > 0:
            ic = args[2]
            k_sel = gather(kb, ic)
            v_sel = gather(vb, ic)
            s_sel = jnp.einsum('bhqd,bhqjkd->bhqjk', qc, k_sel).astype(F32) * scale
            s_sel = jnp.where((ic < ob)[..., None], s_sel, MASK_VALUE)
            probs = jax.nn.softmax(
                jnp.concatenate([s_sel.reshape(B, H, QB, n_sel * BL), s_own], axis=-1), axis=-1)
            p_sel, p_own = jnp.split(probs, [n_sel * BL], axis=-1)
            out = (jnp.einsum('bhqjk,bhqjkd->bhqd',
                              p_sel.reshape(B, H, QB, n_sel, BL).astype(dt), v_sel)
                   + jnp.einsum('bhqk,bhkd->bhqd', p_own.astype(dt), v_own))
        else:
            probs = jax.nn.softmax(s_own, axis=-1)
            out = jnp.einsum('bhqk,bhkd->bhqd', probs.astype(dt), v_own)
        return out

    o = lax.map(attend, xs)
    o = jnp.moveaxis(o, 0, 2).reshape(B, H, S, dh)
    return o.transpose(0, 2, 1, 3).reshape(B, S, MOBA_WIDTH)


def mlstm_chunkwise(lq, lk, lv, lo, li, lf, conv_w, conv_b, i_b, f_b, hn_g):
    B, S, _ = lq.shape
    H, dh, L = MLSTM_HEADS, HEAD_DIM, MLSTM_CHUNK
    nc = S // L
    dt = lq.dtype
    qk = jax.nn.silu(causal_dwconv(jnp.concatenate([lq, lk], axis=-1), conv_w, conv_b))
    q, k = jnp.split(qk, 2, axis=-1)

    def heads(t):
        return t.astype(F32).reshape(B, S, H, dh).transpose(0, 2, 1, 3)

    q, k, v = heads(q), heads(k) * (dh ** -0.5), heads(lv)
    log_i = (li + i_b).astype(F32).transpose(0, 2, 1)
    log_f = jax.nn.log_sigmoid((lf + f_b).astype(F32)).transpose(0, 2, 1)

    def chunks(t):
        return jnp.moveaxis(t.reshape(B, H, nc, L, *t.shape[3:]), 2, 0)

    causal = jnp.tril(jnp.ones((L, L), dtype=bool))

    def step(carry, inp):
        C, n, m = carry
        qc, kc, vc, ic, fc = inp
        b = jnp.cumsum(fc, axis=-1)
        dmat = jnp.where(causal, b[..., :, None] - b[..., None, :] + ic[..., None, :], -jnp.inf)
        inter = b + m[..., None]
        m_t = jnp.maximum(inter, jnp.max(dmat, axis=-1))
        wts = jnp.exp(dmat - m_t[..., None])
        s_inter = jnp.exp(inter - m_t)
        qk_w = jnp.einsum('bhld,bhsd->bhls', qc, kc) * wts
        num = (s_inter[..., None] * jnp.einsum('bhvk,bhlk->bhlv', C, qc)
               + jnp.einsum('bhls,bhsv->bhlv', qk_w, vc))
        den = s_inter * jnp.einsum('bhk,bhlk->bhl', n, qc) + jnp.sum(qk_w, axis=-1)
        h = num / jnp.maximum(jnp.abs(den), jnp.exp(-m_t))[..., None]
        b_end = b[..., -1]
        g_s = b_end[..., None] - b + ic
        m_new = jnp.maximum(b_end + m, jnp.max(g_s, axis=-1))
        w_s = jnp.exp(g_s - m_new[..., None])
        carry_scale = jnp.exp(b_end + m - m_new)
        C = carry_scale[..., None, None] * C + jnp.einsum('bhs,bhsv,bhsk->bhvk', w_s, vc, kc)
        n = carry_scale[..., None] * n + jnp.einsum('bhs,bhsk->bhk', w_s, kc)
        return (C, n, m_new), h

    init = (jnp.zeros((B, H, dh, dh), F32), jnp.zeros((B, H, dh), F32), jnp.zeros((B, H), F32))
    _, h = lax.scan(step, init, (chunks(q), chunks(k), chunks(v), chunks(log_i), chunks(log_f)))
    h = jnp.moveaxis(h, 0, 2).reshape(B, H, S, dh)
    mu = jnp.mean(h, axis=-1, keepdims=True)
    var = jnp.mean(jnp.square(h - mu), axis=-1, keepdims=True)
    h = ((h - mu) * lax.rsqrt(var + NORM_EPS)).transpose(0, 2, 1, 3).reshape(B, S, MLSTM_WIDTH)
    return (h * hn_g * jax.nn.sigmoid(lo.astype(F32))).astype(dt)


def setup_inputs(seed: int = 0) -> dict:
    key = jax.random.key(seed)
    ks = iter(jax.random.split(key, 48))
    L, D = DEPTH, D_MODEL
    RW, MW, LW, LH = RWKV_WIDTH, MOBA_WIDTH, MLSTM_WIDTH, MLSTM_HEADS

    def nrm(shape, scale):
        return scale * jax.random.normal(next(ks), shape, F32)

    def gain(shape):
        return 1.0 + nrm(shape, 0.05)

    def unif(shape, lo, hi):
        return jax.random.uniform(next(ks), shape, F32, lo, hi)

    return {
        'x': nrm((BATCH, SEQ, D), 1.0),
        'p': nrm((L, BATCH, SEQ, PLE_DIM), 1.0),
        'ln_mix_pre': gain((L, D)),
        'ln_mix_post': gain((L, D)),
        'ln_ffn_pre': gain((L, D)),
        'ln_ffn_post': gain((L, D)),
        'ln_ple': gain((L, D)),
        'w_in': nrm((L, D, IN_WIDTH), D ** -0.5),
        'rwkv_mu': unif((L, RWKV_SLAB), 0.0, 1.0),
        'rwkv_w0': unif((L, RW), -6.0, -1.0),
        'rwkv_w2': nrm((L, RWKV_W_RANK, RW), 0.1 * RWKV_W_RANK ** -0.5),
        'rwkv_a0': nrm((L, RW), 0.1),
        'rwkv_a2': nrm((L, RWKV_A_RANK, RW), 0.1 * RWKV_A_RANK ** -0.5),
        'rwkv_g2': nrm((L, RWKV_G_RANK, RW), RWKV_G_RANK ** -0.5),
        'rwkv_k_k': 0.85 + nrm((L, RW), 0.05),
        'rwkv_k_a': gain((L, RW)),
        'rwkv_r_k': nrm((L, RW), 0.1),
        'rwkv_gn_g': gain((L, RW)),
        'rwkv_gn_b': nrm((L, RW), 0.01),
        'rwkv_v0': nrm((L - 1, RW), 0.1),
        'rwkv_v1': nrm((L - 1, RW, RWKV_V_RANK), RW ** -0.5),
        'rwkv_v2': nrm((L - 1, RWKV_V_RANK, RW), 0.1 * RWKV_V_RANK ** -0.5),
        'mlstm_conv_w': nrm((L, MLSTM_CONV, 2 * LW), 0.5),
        'mlstm_conv_b': nrm((L, 2 * LW), 0.02),
        'mlstm_i_b': nrm((L, LH), 0.1),
        'mlstm_f_b': jnp.linspace(3.0, 6.0, LH, dtype=F32)[None, :] + nrm((L, LH), 0.1),
        'mlstm_hn_g': gain((L, LW)),
        'w_br_rwkv': nrm((L, RW, D), RW ** -0.5),
        'w_br_moba': nrm((L, MW, D), MW ** -0.5),
        'w_br_mlstm': nrm((L, LW, D), LW ** -0.5),
        'w_out': nrm((L, D, D), D ** -0.5),
        'ffn_up': nrm((L, D, 2 * D_FF), D ** -0.5),
        'ffn_conv_w': nrm((L, FFN_CONV, 2 * D_FF), FFN_CONV ** -0.5),
        'ffn_conv_b': nrm((L, 2 * D_FF), 0.02),
        'ffn_down': nrm((L, D_FF, D), D_FF ** -0.5),
        'ple_proj': nrm((L, PLE_DIM, D), PLE_DIM ** -0.5),
        'ple_gate': nrm((L, D, D), D ** -0.5),
    }


def reference(x, p, ln_mix_pre, ln_mix_post, ln_ffn_pre, ln_ffn_post, ln_ple, w_in,
              rwkv_mu, rwkv_w0, rwkv_w2, rwkv_a0, rwkv_a2, rwkv_g2, rwkv_k_k, rwkv_k_a,
              rwkv_r_k, rwkv_gn_g, rwkv_gn_b, rwkv_v0, rwkv_v1, rwkv_v2,
              mlstm_conv_w, mlstm_conv_b, mlstm_i_b, mlstm_f_b, mlstm_hn_g,
              w_br_rwkv, w_br_moba, w_br_mlstm, w_out,
              ffn_up, ffn_conv_w, ffn_conv_b, ffn_down, ple_proj, ple_gate):
    v_first = None
    for i in range(DEPTH):
        h = rms_norm(x, ln_mix_pre[i])
        proj = h @ w_in[i]
        s_rwkv, s_moba, s_mlstm, s_gate = split_cols(
            proj, [RWKV_SLAB, MOBA_SLAB, MLSTM_SLAB, GATE_SLAB])
        if i == 0:
            y_a, v_first = rwkv7_time_mix(
                s_rwkv, rwkv_mu[i], rwkv_w0[i], rwkv_w2[i], rwkv_a0[i], rwkv_a2[i], rwkv_g2[i],
                rwkv_k_k[i], rwkv_k_a[i], rwkv_r_k[i], rwkv_gn_g[i], rwkv_gn_b[i])
        else:
            y_a, _ = rwkv7_time_mix(
                s_rwkv, rwkv_mu[i], rwkv_w0[i], rwkv_w2[i], rwkv_a0[i], rwkv_a2[i], rwkv_g2[i],
                rwkv_k_k[i], rwkv_k_a[i], rwkv_r_k[i], rwkv_gn_g[i], rwkv_gn_b[i],
                v_first, rwkv_v0[i - 1], rwkv_v1[i - 1], rwkv_v2[i - 1])
        mq, mk, mv = split_cols(s_moba, [MOBA_WIDTH] * 3)
        y_b = moba_attention(mq, mk, mv)
        lq, lk, lv, lo, li, lf = split_cols(
            s_mlstm, [MLSTM_WIDTH] * 4 + [MLSTM_HEADS] * 2)
        y_c = mlstm_chunkwise(lq, lk, lv, lo, li, lf, mlstm_conv_w[i], mlstm_conv_b[i],
                              mlstm_i_b[i], mlstm_f_b[i], mlstm_hn_g[i])
        g_a, g_b, g_c = jnp.split(jax.nn.sigmoid(s_gate), 3, axis=-1)
        merged = (g_a * (y_a @ w_br_rwkv[i]) + g_b * (y_b @ w_br_moba[i])
                  + g_c * (y_c @ w_br_mlstm[i]))
        x = x + rms_norm(merged @ w_out[i], ln_mix_post[i])
        h = rms_norm(x, ln_ffn_pre[i])
        u = causal_dwconv(h @ ffn_up[i], ffn_conv_w[i], ffn_conv_b[i])
        u_gate, u_val = jnp.split(u, 2, axis=-1)
        f = (jax.nn.gelu(u_gate, approximate=True) * u_val) @ ffn_down[i]
        x = x + rms_norm(f, ln_ffn_post[i])
        gate = jax.nn.sigmoid(rms_norm(x, ln_ple[i]) @ ple_gate[i])
        x = x + gate * (p[i] @ ple_proj[i])
    return x
```

```python
import functools

import jax
import jax.numpy as jnp
from jax import lax
from jax.experimental import pallas as pl
from jax.experimental.pallas import tpu as pltpu

F32 = jnp.float32
BF16 = jnp.bfloat16

HEAD_DIM = 64
N_HEADS = 4
RW = N_HEADS * HEAD_DIM
CHUNK = 64
MOBA_BLOCK = 256
MOBA_TOPK = 3
NORM_EPS = 1e-6
RWKV_GN_EPS = 64e-5
MASK_VALUE = -1e30
LANE = 128
SUBLANE = 8
VMEM_LIMIT = 56 * 1024 * 1024

COL_GATE = 0
COL_RWKV = 3072
COL_MOBA = 4096
COL_MLSTM_QK = 5632
COL_MLSTM_V = 6144
COL_MLSTM_O = 6400
COL_MLSTM_G = 6656
PACKED_WIDTH = 7168

_DIMS = {
    "nn": (((1,), (0,)), ((), ())),
    "nt": (((1,), (1,)), ((), ())),
    "tn": (((0,), (0,)), ((), ())),
}


def _dot(a, b, dims="nn"):
    return lax.dot_general(a, b, _DIMS[dims], preferred_element_type=F32)


def _split(a):
    hi = a.astype(BF16)
    lo = (a - hi.astype(F32)).astype(BF16)
    return hi, lo


def _mm1(a, b, dims="nn"):
    return _dot(a.astype(BF16), b.astype(BF16), dims)


def _mm3(a, b, dims="nn"):
    ah, al = _split(a)
    bh, bl = _split(b)
    return _dot(ah, bh, dims) + (_dot(ah, bl, dims) + _dot(al, bh, dims))


def _mm2r(a, e, dims="nn"):
    ah, al = _split(a)
    eb = e.astype(BF16)
    return _dot(ah, eb, dims) + _dot(al, eb, dims)


def _mm2l(e, a, dims="nn"):
    ah, al = _split(a)
    eb = e.astype(BF16)
    return _dot(eb, ah, dims) + _dot(eb, al, dims)


def _softplus(x):
    return jnp.maximum(x, 0.0) + jnp.log(1.0 + jnp.exp(-jnp.abs(x)))


def _rms(x, g):
    ms = jnp.mean(x * x, axis=-1, keepdims=True)
    return x * lax.rsqrt(ms + NORM_EPS) * g


def _lagged(x, prev8, lag):
    full = jnp.concatenate([prev8, x], axis=0)
    return pltpu.roll(full, lag, 0)[SUBLANE:, :]


def _tile_heads(x):
    return jnp.concatenate([x] * N_HEADS, axis=0)


def _head_masks(n):
    ri = lax.broadcasted_iota(jnp.int32, (n, n), 0)
    ci = lax.broadcasted_iota(jnp.int32, (n, n), 1)
    same = (ri >> 6) == (ci >> 6)
    rp = ri & 63
    cp = ci & 63
    return same, rp, cp, ri == ci


def _cparams(sem):
    return pltpu.CompilerParams(dimension_semantics=sem, vmem_limit_bytes=VMEM_LIMIT)


def _proj_body(x_ref, g_ref, w_ref, o_ref, h_ref):
    @pl.when(pl.program_id(1) == 0)
    def _():
        h_ref[...] = _rms(x_ref[...], g_ref[...]).astype(BF16)

    o_ref[...] = _dot(h_ref[...], w_ref[...])


def _norm_proj(x2d, g, w, tm, tn):
    t, d = x2d.shape
    n = w.shape[1]
    return pl.pallas_call(
        _proj_body,
        grid=(t // tm, n // tn),
        in_specs=[
            pl.BlockSpec((tm, d), lambda i, j: (i, 0)),
            pl.BlockSpec((1, d), lambda i, j: (0, 0)),
            pl.BlockSpec((d, tn), lambda i, j: (0, j)),
        ],
        out_specs=pl.BlockSpec((tm, tn), lambda i, j: (i, j)),
        out_shape=jax.ShapeDtypeStruct((t, n), F32),
        scratch_shapes=[pltpu.VMEM((tm, d), BF16)],
        compiler_params=_cparams(("parallel", "arbitrary")),
        name="norm_proj",
    )(x2d, g.reshape(1, d), w)


def _rwkv_body(has_vres, *refs):
    if has_vres:
        (slab_ref, vf_ref, mu_ref, w0_ref, w2_ref, a0_ref, a2_ref, g2_ref, kk_ref, ka_ref,
         rk_ref, gg_ref, gb_ref, v0_ref, v1_ref, v2_ref, y_ref, vo_ref, st_ref, prev_ref) = refs
    else:
        (slab_ref, mu_ref, w0_ref, w2_ref, a0_ref, a2_ref, g2_ref, kk_ref, ka_ref,
         rk_ref, gg_ref, gb_ref, y_ref, vo_ref, st_ref, prev_ref) = refs
    c = CHUNK

    @pl.when(pl.program_id(1) == 0)
    def _():
        st_ref[...] = jnp.zeros_like(st_ref)
        prev_ref[...] = jnp.zeros_like(prev_ref)

    slab = slab_ref[...]
    shifted = _lagged(slab, prev_ref[...], 1)
    prev_ref[...] = slab[c - SUBLANE:, :]
    xs = slab + mu_ref[...] * (shifted - slab)
    r = xs[:, 0:RW]
    k = xs[:, RW:2 * RW]
    v = xs[:, 2 * RW:3 * RW]
    xw = xs[:, 768:832]
    xa = xs[:, 832:896]
    xg = xs[:, 896:1024]
    wlog = -_softplus(-(w0_ref[...] + _mm3(jnp.tanh(xw), w2_ref[...]))) - 0.5
    alr = jax.nn.sigmoid(a0_ref[...] + _mm3(xa, a2_ref[...]))
    g = _mm3(jax.nn.sigmoid(xg), g2_ref[...])
    if has_vres:
        mix = jax.nn.sigmoid(v0_ref[...] + _mm3(_mm3(v, v1_ref[...]), v2_ref[...]))
        v = v + (vf_ref[...] - v) * mix
    vo_ref[...] = v

    n = N_HEADS * c
    same, rp, cp, eye = _head_masks(n)
    hm = same.astype(F32)
    strict = same & (rp > cp)
    incl = same & (rp >= cp)

    kk = k * kk_ref[...]
    kk = kk / jnp.maximum(jnp.sqrt(_mm2r(kk * kk, hm)), 1e-12)
    k2 = k * (1.0 + (alr - 1.0) * ka_ref[...])

    lw = -jnp.exp(wlog)
    ti = lax.broadcasted_iota(jnp.int32, (c, c), 0)
    tj = lax.broadcasted_iota(jnp.int32, (c, c), 1)
    cs = _mm2l((ti >= tj).astype(F32), lw)
    cs_end = cs[c - 1:c, :]
    p_in = jnp.exp(cs)
    p_ex = jnp.exp(cs - lw)
    p_inv = jnp.exp(-cs)
    p_tail = jnp.exp(cs_end - cs)

    a_mt = _tile_heads(-kk * p_ex) * hm
    r_mt = _tile_heads(r * p_in) * hm
    b_t = _tile_heads(kk * alr * p_inv)
    k_t = _tile_heads(k2 * p_inv)
    v_mt = _tile_heads(v) * hm
    b_tail = _tile_heads(kk * alr * p_tail) * hm
    k_tail = _tile_heads(k2 * p_tail) * hm

    l_ab = jnp.where(strict, _mm3(a_mt, b_t, "nt"), 0.0)
    l_ak = jnp.where(strict, _mm3(a_mt, k_t, "nt"), 0.0)
    m_rb = jnp.where(incl, _mm3(r_mt, b_t, "nt"), 0.0)
    m_rk = jnp.where(incl, _mm3(r_mt, k_t, "nt"), 0.0)

    tinv = eye.astype(F32) + l_ab
    npow = l_ab
    for _ in range(5):
        npow = _mm3(npow, npow)
        tinv = tinv + _mm3(tinv, npow)

    st = st_ref[...]
    u = _mm3(tinv, _mm3(a_mt, st, "nt") + _mm3(l_ak, v_mt))
    yy = _mm3(r_mt, st, "nt") + _mm3(m_rb, u) + _mm3(m_rk, v_mt)
    y = yy[0:c] + yy[c:2 * c] + yy[2 * c:3 * c] + yy[3 * c:4 * c]
    st_ref[...] = st * p_in[c - 1:c, :] + _mm3(u, b_tail, "tn") + _mm3(v_mt, k_tail, "tn")

    mean = _mm2r(y, hm) * (1.0 / HEAD_DIM)
    d = y - mean
    var = _mm2r(d * d, hm) * (1.0 / HEAD_DIM)
    yn = d * lax.rsqrt(var + RWKV_GN_EPS) * gg_ref[...] + gb_ref[...]
    bonus = _mm2r(r * k2 * rk_ref[...], hm) * v
    y_ref[...] = (yn + bonus) * g


def _rwkv(proj, b, s, params, v_first):
    t = proj.shape[0]
    nc = s // CHUNK
    has_vres = v_first is not None
    row = lambda bi, ci: (bi * nc + ci, 0)
    const = lambda bi, ci: (0, 0)
    in_specs = [pl.BlockSpec((CHUNK, 1024), lambda bi, ci: (bi * nc + ci, COL_RWKV // 1024))]
    args = [proj]
    if has_vres:
        in_specs.append(pl.BlockSpec((CHUNK, RW), row))
        args.append(v_first)
    for prm in params:
        in_specs.append(pl.BlockSpec(prm.shape, const))
        args.append(prm)
    n = N_HEADS * CHUNK
    return pl.pallas_call(
        functools.partial(_rwkv_body, has_vres),
        grid=(b, nc),
        in_specs=in_specs,
        out_specs=[pl.BlockSpec((CHUNK, RW), row), pl.BlockSpec((CHUNK, RW), row)],
        out_shape=[jax.ShapeDtypeStruct((t, RW), F32), jax.ShapeDtypeStruct((t, RW), F32)],
        scratch_shapes=[pltpu.VMEM((n, n), F32), pltpu.VMEM((SUBLANE, 1024), F32)],
        compiler_params=_cparams(("parallel", "arbitrary")),
        name="rwkv7_chunk",
    )(*args)


def _moba_body(nb, n_sel, q_ref, k_ref, v_ref, o_ref, kmean_ref, sel_ref):
    i = pl.program_id(2)
    bl = MOBA_BLOCK
    nbp = kmean_ref.shape[1]
    scale = HEAD_DIM ** -0.5

    @pl.when(i == 0)
    def _():
        kmean_ref[...] = jnp.zeros_like(kmean_ref)
        for h in range(2):
            kh = k_ref[:, h * HEAD_DIM:(h + 1) * HEAD_DIM]
            kmean_ref[h, 0:nb, :] = jnp.mean(kh.reshape(nb, bl, HEAD_DIM), axis=1)

    blk = lax.broadcasted_iota(jnp.int32, (nbp, bl), 0)
    kpos = lax.broadcasted_iota(jnp.int32, (bl, bl), 0)
    qpos = lax.broadcasted_iota(jnp.int32, (bl, bl), 1)
    e0 = lax.broadcasted_iota(jnp.int32, (HEAD_DIM, HEAD_DIM), 0)
    e1 = lax.broadcasted_iota(jnp.int32, (HEAD_DIM, HEAD_DIM), 1)
    eye = (e0 == e1).astype(F32)
    past = blk < i

    for h in range(2):
        lanes = slice(h * HEAD_DIM, (h + 1) * HEAD_DIM)
        qh = q_ref[:, lanes]
        bs = jnp.where(past, _mm3(kmean_ref[h], qh, "nt"), MASK_VALUE)
        rank = jnp.zeros((nbp, bl), jnp.int32)
        for jp in range(nb):
            row = bs[jp:jp + 1, :]
            beats = (row > bs) | ((row == bs) & (jp < blk))
            rank = rank + beats.astype(jnp.int32)
        sel_ref[...] = ((rank < n_sel) & past).astype(F32)

        start = pl.multiple_of(i * bl, bl)
        s = _mm1(k_ref[pl.ds(start, bl), lanes], qh, "nt") * scale
        s = jnp.where(kpos <= qpos, s, MASK_VALUE)
        m = jnp.max(s, axis=0, keepdims=True)
        p = jnp.exp(s - m)
        l = jnp.sum(p, axis=0, keepdims=True)
        acc = _mm1(v_ref[pl.ds(start, bl), lanes], p, "tn")

        def body(j, carry):
            m, l, acc = carry
            off = pl.multiple_of(j * bl, bl)
            s = _mm1(k_ref[pl.ds(off, bl), lanes], qh, "nt") * scale
            s = jnp.where(sel_ref[pl.ds(j, 1), :] > 0.0, s, MASK_VALUE)
            m_new = jnp.maximum(m, jnp.max(s, axis=0, keepdims=True))
            alpha = jnp.exp(m - m_new)
            p = jnp.exp(s - m_new)
            l = alpha * l + jnp.sum(p, axis=0, keepdims=True)
            acc = alpha * acc + _mm1(v_ref[pl.ds(off, bl), lanes], p, "tn")
            return m_new, l, acc

        m, l, acc = lax.fori_loop(0, i, body, (m, l, acc))
        o_ref[:, lanes] = _mm2r(acc / l, eye, "tn")


def _moba(proj, b, s):
    t = proj.shape[0]
    bl = MOBA_BLOCK
    nb = s // bl
    n_sel = min(MOBA_TOPK, nb - 1)
    nbp = -(-nb // SUBLANE) * SUBLANE
    nhp = 512 // LANE
    qc, kc, vc = COL_MOBA // LANE, (COL_MOBA + 512) // LANE, (COL_MOBA + 1024) // LANE
    return pl.pallas_call(
        functools.partial(_moba_body, nb, n_sel),
        grid=(b, nhp, nb),
        in_specs=[
            pl.BlockSpec((bl, LANE), lambda bi, hp, i: (bi * nb + i, qc + hp)),
            pl.BlockSpec((s, LANE), lambda bi, hp, i: (bi, kc + hp)),
            pl.BlockSpec((s, LANE), lambda bi, hp, i: (bi, vc + hp)),
        ],
        out_specs=pl.BlockSpec((bl, LANE), lambda bi, hp, i: (bi * nb + i, hp)),
        out_shape=jax.ShapeDtypeStruct((t, 512), F32),
        scratch_shapes=[pltpu.VMEM((2, nbp, HEAD_DIM), F32), pltpu.VMEM((nbp, bl), F32)],
        compiler_params=_cparams(("parallel", "parallel", "arbitrary")),
        name="moba_attn",
    )(proj, proj, proj)


def _mlstm_body(qk_ref, v_ref, o_ref, gcol_ref, grow_ref, cw_ref, cb_ref, bcol_ref, brow_ref,
                hng_ref, y_ref, cst_ref, n_ref, m_ref, prev_ref):
    c = CHUNK
    n = N_HEADS * c

    @pl.when(pl.program_id(1) == 0)
    def _():
        cst_ref[...] = jnp.zeros_like(cst_ref)
        n_ref[...] = jnp.zeros_like(n_ref)
        m_ref[...] = jnp.zeros_like(m_ref)
        prev_ref[...] = jnp.zeros_like(prev_ref)

    x = qk_ref[...]
    prev8 = prev_ref[...]
    cw = cw_ref[...]
    conv = (cb_ref[...] + cw[3:4] * x + cw[2:3] * _lagged(x, prev8, 1)
            + cw[1:2] * _lagged(x, prev8, 2) + cw[0:1] * _lagged(x, prev8, 3))
    prev_ref[...] = x[c - SUBLANE:, :]
    qk = conv * jax.nn.sigmoid(conv)
    q = qk[:, 0:RW]
    k = qk[:, RW:2 * RW] * (HEAD_DIM ** -0.5)
    v = v_ref[...]

    same, rp, cp, _ = _head_masks(n)
    hm = same.astype(F32)
    incl = same & (rp >= cp)
    q_mt = _tile_heads(q) * hm
    k_t = _tile_heads(k)
    k_mt = k_t * hm
    v_mt = _tile_heads(v) * hm

    gcol = gcol_ref[0] + bcol_ref[...]
    grow = grow_ref[0] + brow_ref[...]
    li_col = gcol[:, 0:1]
    lf_col = -_softplus(-gcol[:, 1:2])
    li_row = grow[0:1, :]
    lf_row = -_softplus(-grow[1:2, :])
    lf_cb = jnp.broadcast_to(lf_col, (n, LANE))
    lf_rb = jnp.broadcast_to(lf_row, (SUBLANE, n))
    b_col = _mm2l(incl.astype(F32), lf_cb)[:, 0:1]
    bend_col = _mm2l(hm, lf_cb)[:, 0:1]
    b_row = _mm2r(lf_rb, (same & (rp <= cp)).astype(F32))[0:1, :]
    bend_row = _mm2r(lf_rb, hm)[0:1, :]

    m_col = m_ref[...]
    dmat = jnp.where(incl, b_col - b_row + li_row, -jnp.inf)
    inter = b_col + m_col
    m_t = jnp.maximum(inter, jnp.max(dmat, axis=1, keepdims=True))
    wts = jnp.exp(dmat - m_t)
    s_inter = jnp.exp(inter - m_t)
    cst = cst_ref[...]
    n_row = n_ref[...]
    qk_w = _mm3(q_mt, k_t, "nt") * wts
    num = s_inter * _mm3(q_mt, cst, "nt") + _mm3(qk_w, v_mt)
    den = (s_inter * jnp.sum(q_mt * n_row, axis=1, keepdims=True)
           + jnp.sum(qk_w, axis=1, keepdims=True))
    hh = num / jnp.maximum(jnp.abs(den), jnp.exp(-m_t))
    y = hh[0:c] + hh[c:2 * c] + hh[2 * c:3 * c] + hh[3 * c:4 * c]

    g_col = bend_col - b_col + li_col
    g_row = bend_row - b_row + li_row
    g_max = jnp.max(jnp.where(same, g_row, -jnp.inf), axis=1, keepdims=True)
    m_new = jnp.maximum(bend_col + m_col, g_max)
    w_col = jnp.exp(g_col - m_new)
    scale_col = jnp.exp(bend_col + m_col - m_new)
    scale_row = jnp.max(jnp.where(same, scale_col, 0.0), axis=0, keepdims=True)
    cst_ref[...] = scale_col * cst + _mm3(v_mt * w_col, k_mt, "tn")
    n_ref[...] = scale_row * n_row + jnp.sum(k_mt * w_col, axis=0, keepdims=True)
    m_ref[...] = m_new

    mean = _mm2r(y, hm) * (1.0 / HEAD_DIM)
    d = y - mean
    var = _mm2r(d * d, hm) * (1.0 / HEAD_DIM)
    y_ref[...] = d * lax.rsqrt(var + NORM_EPS) * hng_ref[...] * jax.nn.sigmoid(o_ref[...])


def _mlstm(proj, b, s, conv_w, conv_b, i_b, f_b, hn_g):
    t = proj.shape[0]
    c = CHUNK
    nc = s // c
    n = N_HEADS * c
    gates = proj[:, COL_MLSTM_G:COL_MLSTM_G + 2 * N_HEADS].reshape(b * nc, c, 2, N_HEADS)
    gcol = gates.transpose(0, 3, 1, 2).reshape(b * nc, n, 2)
    grow = gates.transpose(0, 2, 3, 1).reshape(b * nc, 2, n)
    bias = jnp.stack([jnp.repeat(i_b, c), jnp.repeat(f_b, c)], axis=0)
    row = lambda bi, ci: (bi * nc + ci, 0)
    const = lambda bi, ci: (0, 0)
    return pl.pallas_call(
        _mlstm_body,
        grid=(b, nc),
        in_specs=[
            pl.BlockSpec((c, 2 * RW), lambda bi, ci: (bi * nc + ci, COL_MLSTM_QK // (2 * RW))),
            pl.BlockSpec((c, RW), lambda bi, ci: (bi * nc + ci, COL_MLSTM_V // RW)),
            pl.BlockSpec((c, RW), lambda bi, ci: (bi * nc + ci, COL_MLSTM_O // RW)),
            pl.BlockSpec((1, n, 2), lambda bi, ci: (bi * nc + ci, 0, 0)),
            pl.BlockSpec((1, 2, n), lambda bi, ci: (bi * nc + ci, 0, 0)),
            pl.BlockSpec(conv_w.shape, const),
            pl.BlockSpec((1, 2 * RW), const),
            pl.BlockSpec((n, 2), const),
            pl.BlockSpec((2, n), const),
            pl.BlockSpec((1, RW), const),
        ],
        out_specs=pl.BlockSpec((c, RW), row),
        out_shape=jax.ShapeDtypeStruct((t, RW), F32),
        scratch_shapes=[pltpu.VMEM((n, n), F32), pltpu.VMEM((1, n), F32), pltpu.VMEM((n, 1), F32),
                        pltpu.VMEM((SUBLANE, 2 * RW), F32)],
        compiler_params=_cparams(("parallel", "arbitrary")),
        name="mlstm_chunk",
    )(proj, proj, proj, gcol, grow, conv_w, conv_b.reshape(1, -1), bias.T, bias,
      hn_g.reshape(1, -1))


def _merge_body(x_ref, ya_ref, yb_ref, yc_ref, ga_ref, gb_ref, gc_ref, wa_ref, wb_ref, wc_ref,
                wo_ref, g_ref, o_ref):
    merged = (jax.nn.sigmoid(ga_ref[...]) * _mm1(ya_ref[...], wa_ref[...])
              + jax.nn.sigmoid(gb_ref[...]) * _mm1(yb_ref[...], wb_ref[...])
              + jax.nn.sigmoid(gc_ref[...]) * _mm1(yc_ref[...], wc_ref[...]))
    o_ref[...] = x_ref[...] + _rms(_mm1(merged, wo_ref[...]), g_ref[...])


def _merge(x2d, proj, ya, yb, yc, wa, wb, wc, wo, g, tm):
    t, d = x2d.shape
    row = lambda i: (i, 0)
    const = lambda i: (0, 0)
    return pl.pallas_call(
        _merge_body,
        grid=(t // tm,),
        in_specs=[
            pl.BlockSpec((tm, d), row),
            pl.BlockSpec((tm, ya.shape[1]), row),
            pl.BlockSpec((tm, yb.shape[1]), row),
            pl.BlockSpec((tm, yc.shape[1]), row),
            pl.BlockSpec((tm, d), lambda i: (i, 0)),
            pl.BlockSpec((tm, d), lambda i: (i, 1)),
            pl.BlockSpec((tm, d), lambda i: (i, 2)),
            pl.BlockSpec(wa.shape, const),
            pl.BlockSpec(wb.shape, const),
            pl.BlockSpec(wc.shape, const),
            pl.BlockSpec(wo.shape, const),
            pl.BlockSpec((1, d), const),
        ],
        out_specs=pl.BlockSpec((tm, d), row),
        out_shape=jax.ShapeDtypeStruct((t, d), F32),
        compiler_params=_cparams(("parallel",)),
        name="merge_out",
    )(x2d, ya, yb, yc, proj, proj, proj, wa, wb, wc, wo, g.reshape(1, d))


def _ffn_body(nt_seq, n_ff, x_ref, g1_ref, upg_ref, upv_ref, cwg_ref, cwv_ref, cbg_ref, cbv_ref,
              down_ref, g2_ref, g3_ref, pg_ref, pp_ref, p_ref, o_ref, h_ref, acc_ref, ugp_ref, uvp_ref):
    i = pl.program_id(0)
    j = pl.program_id(1)
    tm = x_ref.shape[0]

    @pl.when(j == 0)
    def _():
        h_ref[...] = _rms(x_ref[...], g1_ref[...]).astype(BF16)
        acc_ref[...] = jnp.zeros_like(acc_ref)

    @pl.when(i % nt_seq == 0)
    def _():
        ugp_ref[j] = jnp.zeros(ugp_ref.shape[1:], F32)
        uvp_ref[j] = jnp.zeros(uvp_ref.shape[1:], F32)

    h = h_ref[...]
    ug = _dot(h, upg_ref[...])
    uv = _dot(h, upv_ref[...])
    pg = ugp_ref[j]
    pv = uvp_ref[j]
    ugp_ref[j] = ug[tm - SUBLANE:, :]
    uvp_ref[j] = uv[tm - SUBLANE:, :]
    cwg = cwg_ref[...]
    cwv = cwv_ref[...]
    cg = cbg_ref[...] + cwg[2:3] * ug + cwg[1:2] * _lagged(ug, pg, 1) + cwg[0:1] * _lagged(ug, pg, 2)
    cv = cbv_ref[...] + cwv[2:3] * uv + cwv[1:2] * _lagged(uv, pv, 1) + cwv[0:1] * _lagged(uv, pv, 2)
    act = jax.nn.gelu(cg, approximate=True) * cv
    acc_ref[...] += _mm1(act, down_ref[...])

    @pl.when(j == n_ff - 1)
    def _():
        x2 = x_ref[...] + _rms(acc_ref[...], g2_ref[...])
        gate = jax.nn.sigmoid(_mm1(_rms(x2, g3_ref[...]), pg_ref[...]))
        o_ref[...] = x2 + gate * _mm1(p_ref[...], pp_ref[...])


def _ffn(x2d, p2d, s, g1, up, cw, cb, down, g2, g3, pgate, pproj, tm, tf):
    t, d = x2d.shape
    dff = down.shape[0]
    n_ff = dff // tf
    nt_seq = s // tm
    ple = p2d.shape[1]
    row = lambda i, j: (i, 0)
    const = lambda i, j: (0, 0)
    cb2 = cb.reshape(1, -1)
    return pl.pallas_call(
        functools.partial(_ffn_body, nt_seq, n_ff),
        grid=(t // tm, n_ff),
        in_specs=[
            pl.BlockSpec((tm, d), row),
            pl.BlockSpec((1, d), const),
            pl.BlockSpec((d, tf), lambda i, j: (0, j)),
            pl.BlockSpec((d, tf), lambda i, j: (0, n_ff + j)),
            pl.BlockSpec((cw.shape[0], tf), lambda i, j: (0, j)),
            pl.BlockSpec((cw.shape[0], tf), lambda i, j: (0, n_ff + j)),
            pl.BlockSpec((1, tf), lambda i, j: (0, j)),
            pl.BlockSpec((1, tf), lambda i, j: (0, n_ff + j)),
            pl.BlockSpec((tf, d), lambda i, j: (j, 0)),
            pl.BlockSpec((1, d), const),
            pl.BlockSpec((1, d), const),
            pl.BlockSpec(pgate.shape, const),
            pl.BlockSpec(pproj.shape, const),
            pl.BlockSpec((tm, ple), row),
        ],
        out_specs=pl.BlockSpec((tm, d), row),
        out_shape=jax.ShapeDtypeStruct((t, d), F32),
        scratch_shapes=[pltpu.VMEM((tm, d), BF16), pltpu.VMEM((tm, d), F32),
                        pltpu.VMEM((n_ff, SUBLANE, tf), F32), pltpu.VMEM((n_ff, SUBLANE, tf), F32)],
        compiler_params=_cparams(("arbitrary", "arbitrary")),
        name="ffn_ple",
    )(x2d, g1.reshape(1, d), up, up, cw, cw, cb2, cb2, down, g2.reshape(1, d), g3.reshape(1, d),
      pgate, pproj, p2d)


def _pack_w_in(w):
    d = w.shape[0]
    rwkv = w[:, 0:1024]
    moba = w[:, 1024:2560]
    ml = w[:, 2560:3592]
    gate = w[:, 3592:6664]
    pad = jnp.zeros((d, PACKED_WIDTH - COL_MLSTM_G - 2 * N_HEADS), w.dtype)
    packed = jnp.concatenate([gate, rwkv, moba, ml, pad], axis=1)
    return packed.astype(BF16)


def _row_tile(t, want):
    return want if t % want == 0 else t


def kernel(x, p, ln_mix_pre, ln_mix_post, ln_ffn_pre, ln_ffn_post, ln_ple, w_in, rwkv_mu, rwkv_w0, rwkv_w2, rwkv_a0, rwkv_a2, rwkv_g2, rwkv_k_k, rwkv_k_a, rwkv_r_k, rwkv_gn_g, rwkv_gn_b, rwkv_v0, rwkv_v1, rwkv_v2, mlstm_conv_w, mlstm_conv_b, mlstm_i_b, mlstm_f_b, mlstm_hn_g, w_br_rwkv, w_br_moba, w_br_mlstm, w_out, ffn_up, ffn_conv_w, ffn_conv_b, ffn_down, ple_proj, ple_gate):
    b, s, d = x.shape
    depth = w_in.shape[0]
    t = b * s
    assert d == 1024 and s % MOBA_BLOCK == 0 and w_in.shape[2] == 6664
    xf = x.reshape(t, d)
    tm_proj = _row_tile(t, 1024)
    tm = min(512, s)
    r2 = lambda a: a.reshape(1, -1)
    v_first = None
    for i in range(depth):
        proj = _norm_proj(xf, ln_mix_pre[i], _pack_w_in(w_in[i]), tm_proj, 1024)
        params = [r2(rwkv_mu[i]), r2(rwkv_w0[i]), rwkv_w2[i], r2(rwkv_a0[i]), rwkv_a2[i], rwkv_g2[i],
                  r2(rwkv_k_k[i]), r2(rwkv_k_a[i]), r2(rwkv_r_k[i]), r2(rwkv_gn_g[i]), r2(rwkv_gn_b[i])]
        if i > 0:
            params += [r2(rwkv_v0[i - 1]), rwkv_v1[i - 1], rwkv_v2[i - 1]]
        y_a, v_cur = _rwkv(proj, b, s, params, v_first if i > 0 else None)
        if i == 0:
            v_first = v_cur
        y_b = _moba(proj, b, s)
        y_c = _mlstm(proj, b, s, mlstm_conv_w[i], mlstm_conv_b[i], mlstm_i_b[i], mlstm_f_b[i],
                     mlstm_hn_g[i])
        xf = _merge(xf, proj, y_a, y_b, y_c, w_br_rwkv[i].astype(BF16), w_br_moba[i].astype(BF16),
                    w_br_mlstm[i].astype(BF16), w_out[i].astype(BF16), ln_mix_post[i], tm)
        xf = _ffn(xf, p[i].reshape(t, -1), s, ln_ffn_pre[i], ffn_up[i].astype(BF16), ffn_conv_w[i],
                  ffn_conv_b[i], ffn_down[i].astype(BF16), ln_ffn_post[i], ln_ple[i],
                  ple_gate[i].astype(BF16), ple_proj[i].astype(BF16), tm, 256)
    return xf.reshape(b, s, d)
```

```python
import functools

import jax
import jax.numpy as jnp
from jax import lax
from jax.experimental import pallas as pl
from jax.experimental.pallas import tpu as pltpu

F32 = jnp.float32
BF16 = jnp.bfloat16

HEAD_DIM = 64
N_HEADS = 4
RW = N_HEADS * HEAD_DIM
CHUNK = 64
MOBA_BLOCK = 256
MOBA_TOPK = 3
NORM_EPS = 1e-6
RWKV_GN_EPS = 64e-5
MASK_VALUE = -1e30
LANE = 128
SUBLANE = 8
VMEM_LIMIT = 56 * 1024 * 1024

COL_GATE = 0
COL_RWKV = 3072
COL_MOBA = 4096
COL_MLSTM_QK = 5632
COL_MLSTM_V = 6144
COL_MLSTM_O = 6400
COL_MLSTM_G = 6656
PACKED_WIDTH = 7168

_DIMS = {
    "nn": (((1,), (0,)), ((), ())),
    "nt": (((1,), (1,)), ((), ())),
    "tn": (((0,), (0,)), ((), ())),
}


def _dot(a, b, dims="nn"):
    return lax.dot_general(a, b, _DIMS[dims], preferred_element_type=F32)


def _split(a):
    hi = a.astype(BF16)
    lo = (a - hi.astype(F32)).astype(BF16)
    return hi, lo


def _mm1(a, b, dims="nn"):
    return _dot(a.astype(BF16), b.astype(BF16), dims)


def _mm3(a, b, dims="nn"):
    ah, al = _split(a)
    bh, bl = _split(b)
    return _dot(ah, bh, dims) + (_dot(ah, bl, dims) + _dot(al, bh, dims))


def _mm2r(a, e, dims="nn"):
    ah, al = _split(a)
    eb = e.astype(BF16)
    return _dot(ah, eb, dims) + _dot(al, eb, dims)


def _mm2l(e, a, dims="nn"):
    ah, al = _split(a)
    eb = e.astype(BF16)
    return _dot(eb, ah, dims) + _dot(eb, al, dims)


def _softplus(x):
    return jnp.maximum(x, 0.0) + jnp.log(1.0 + jnp.exp(-jnp.abs(x)))


def _rms(x, g):
    ms = jnp.mean(x * x, axis=-1, keepdims=True)
    return x * lax.rsqrt(ms + NORM_EPS) * g


def _lagged(x, prev8, lag):
    full = jnp.concatenate([prev8, x], axis=0)
    return pltpu.roll(full, lag, 0)[SUBLANE:, :]


def _tile_heads(x):
    return jnp.concatenate([x] * N_HEADS, axis=0)


def _head_masks(n):
    ri = lax.broadcasted_iota(jnp.int32, (n, n), 0)
    ci = lax.broadcasted_iota(jnp.int32, (n, n), 1)
    same = (ri >> 6) == (ci >> 6)
    rp = ri & 63
    cp = ci & 63
    return same, rp, cp, ri == ci


def _cparams(sem):
    return pltpu.CompilerParams(dimension_semantics=sem, vmem_limit_bytes=VMEM_LIMIT)


def _proj_body(x_ref, g_ref, w_ref, o_ref, h_ref):
    @pl.when(pl.program_id(1) == 0)
    def _():
        h_ref[...] = _rms(x_ref[...], g_ref[...]).astype(BF16)

    o_ref[...] = _dot(h_ref[...], w_ref[...])


def _norm_proj(x2d, g, w, tm, tn):
    t, d = x2d.shape
    n = w.shape[1]
    return pl.pallas_call(
        _proj_body,
        grid=(t // tm, n // tn),
        in_specs=[
            pl.BlockSpec((tm, d), lambda i, j: (i, 0)),
            pl.BlockSpec((1, d), lambda i, j: (0, 0)),
            pl.BlockSpec((d, tn), lambda i, j: (0, j)),
        ],
        out_specs=pl.BlockSpec((tm, tn), lambda i, j: (i, j)),
        out_shape=jax.ShapeDtypeStruct((t, n), F32),
        scratch_shapes=[pltpu.VMEM((tm, d), BF16)],
        compiler_params=_cparams(("parallel", "arbitrary")),
        name="norm_proj",
    )(x2d, g.reshape(1, d), w)


def _rwkv_body(has_vres, *refs):
    if has_vres:
        (slab_ref, vf_ref, mu_ref, w0_ref, w2_ref, a0_ref, a2_ref, g2_ref, kk_ref, ka_ref,
         rk_ref, gg_ref, gb_ref, v0_ref, v1_ref, v2_ref, y_ref, vo_ref, st_ref, prev_ref) = refs
    else:
        (slab_ref, mu_ref, w0_ref, w2_ref, a0_ref, a2_ref, g2_ref, kk_ref, ka_ref,
         rk_ref, gg_ref, gb_ref, y_ref, vo_ref, st_ref, prev_ref) = refs
    c = CHUNK

    @pl.when(pl.program_id(1) == 0)
    def _():
        st_ref[...] = jnp.zeros_like(st_ref)
        prev_ref[...] = jnp.zeros_like(prev_ref)

    slab = slab_ref[...]
    shifted = _lagged(slab, prev_ref[...], 1)
    prev_ref[...] = slab[c - SUBLANE:, :]
    xs = slab + mu_ref[...] * (shifted - slab)
    r = xs[:, 0:RW]
    k = xs[:, RW:2 * RW]
    v = xs[:, 2 * RW:3 * RW]
    xw = xs[:, 768:832]
    xa = xs[:, 832:896]
    xg = xs[:, 896:1024]
    wlog = -_softplus(-(w0_ref[...] + _mm3(jnp.tanh(xw), w2_ref[...]))) - 0.5
    alr = jax.nn.sigmoid(a0_ref[...] + _mm3(xa, a2_ref[...]))
    g = _mm3(jax.nn.sigmoid(xg), g2_ref[...])
    if has_vres:
        mix = jax.nn.sigmoid(v0_ref[...] + _mm3(_mm3(v, v1_ref[...]), v2_ref[...]))
        v = v + (vf_ref[...] - v) * mix
    vo_ref[...] = v

    n = N_HEADS * c
    same, rp, cp, eye = _head_masks(n)
    hm = same.astype(F32)
    strict = same & (rp > cp)
    incl = same & (rp >= cp)

    kk = k * kk_ref[...]
    kk = kk / jnp.maximum(jnp.sqrt(_mm2r(kk * kk, hm)), 1e-12)
    k2 = k * (1.0 + (alr - 1.0) * ka_ref[...])

    lw = -jnp.exp(wlog)
    ti = lax.broadcasted_iota(jnp.int32, (c, c), 0)
    tj = lax.broadcasted_iota(jnp.int32, (c, c), 1)
    cs = _mm2l((ti >= tj).astype(F32), lw)
    cs_end = cs[c - 1:c, :]
    p_in = jnp.exp(cs)
    p_ex = jnp.exp(cs - lw)
    p_inv = jnp.exp(-cs)
    p_tail = jnp.exp(cs_end - cs)

    a_mt = _tile_heads(-kk * p_ex) * hm
    r_mt = _tile_heads(r * p_in) * hm
    b_t = _tile_heads(kk * alr * p_inv)
    k_t = _tile_heads(k2 * p_inv)
    v_mt = _tile_heads(v) * hm
    b_tail = _tile_heads(kk * alr * p_tail) * hm
    k_tail = _tile_heads(k2 * p_tail) * hm

    l_ab = jnp.where(strict, _mm1(a_mt, b_t, "nt"), 0.0)
    l_ak = jnp.where(strict, _mm1(a_mt, k_t, "nt"), 0.0)
    m_rb = jnp.where(incl, _mm1(r_mt, b_t, "nt"), 0.0)
    m_rk = jnp.where(incl, _mm1(r_mt, k_t, "nt"), 0.0)

    tinv = eye.astype(F32) + l_ab
    npow = l_ab
    for _ in range(5):
        npow = _mm1(npow, npow)
        tinv = tinv + _mm1(tinv, npow)

    st = st_ref[...]
    u = _mm1(tinv, _mm1(a_mt, st, "nt") + _mm1(l_ak, v_mt))
    yy = _mm1(r_mt, st, "nt") + _mm1(m_rb, u) + _mm1(m_rk, v_mt)
    y = yy[0:c] + yy[c:2 * c] + yy[2 * c:3 * c] + yy[3 * c:4 * c]
    st_ref[...] = st * p_in[c - 1:c, :] + _mm1(u, b_tail, "tn") + _mm1(v_mt, k_tail, "tn")

    mean = _mm2r(y, hm) * (1.0 / HEAD_DIM)
    d = y - mean
    var = _mm2r(d * d, hm) * (1.0 / HEAD_DIM)
    yn = d * lax.rsqrt(var + RWKV_GN_EPS) * gg_ref[...] + gb_ref[...]
    bonus = _mm2r(r * k2 * rk_ref[...], hm) * v
    y_ref[...] = (yn + bonus) * g


def _rwkv(proj, b, s, params, v_first):
    t = proj.shape[0]
    nc = s // CHUNK
    has_vres = v_first is not None
    row = lambda bi, ci: (bi * nc + ci, 0)
    const = lambda bi, ci: (0, 0)
    in_specs = [pl.BlockSpec((CHUNK, 1024), lambda bi, ci: (bi * nc + ci, COL_RWKV // 1024))]
    args = [proj]
    if has_vres:
        in_specs.append(pl.BlockSpec((CHUNK, RW), row))
        args.append(v_first)
    for prm in params:
        in_specs.append(pl.BlockSpec(prm.shape, const))
        args.append(prm)
    n = N_HEADS * CHUNK
    return pl.pallas_call(
        functools.partial(_rwkv_body, has_vres),
        grid=(b, nc),
        in_specs=in_specs,
        out_specs=[pl.BlockSpec((CHUNK, RW), row), pl.BlockSpec((CHUNK, RW), row)],
        out_shape=[jax.ShapeDtypeStruct((t, RW), F32), jax.ShapeDtypeStruct((t, RW), F32)],
        scratch_shapes=[pltpu.VMEM((n, n), F32), pltpu.VMEM((SUBLANE, 1024), F32)],
        compiler_params=_cparams(("parallel", "arbitrary")),
        name="rwkv7_chunk",
    )(*args)


def _moba_body(nb, n_sel, q_ref, k_ref, v_ref, o_ref, m_ref, acc_ref, qa_ref):
    bl = MOBA_BLOCK
    qt = 2 * bl
    nbp =-(-nb // SUBLANE) * SUBLANE
    scale = HEAD_DIM ** -0.5
    lane = lax.broadcasted_iota(jnp.int32, (bl, LANE), 1)
    head0 = lane < HEAD_DIM
    hmask = (head0, jnp.logical_not(head0))
    spare = (lane - HEAD_DIM, lane)
    blk = lax.broadcasted_iota(jnp.int32, (nbp, bl), 0)
    qpos = lax.broadcasted_iota(jnp.int32, (bl, bl), 0)
    kpos = lax.broadcasted_iota(jnp.int32, (bl, bl), 1)
    causal = kpos <= qpos
    e_row = lax.broadcasted_iota(jnp.int32, (nbp, LANE), 0)
    e_lane = lax.broadcasted_iota(jnp.int32, (nbp, LANE), 1)
    place = ((e_lane == e_row + HEAD_DIM).astype(BF16), (e_lane == e_row).astype(BF16))
    klane = e_lane < HEAD_DIM

    kmean = jnp.mean(k_ref[...].reshape(nb, bl, LANE), axis=1)
    if nbp > nb:
        kmean = jnp.concatenate([kmean, jnp.zeros((nbp - nb, LANE), F32)], axis=0)
    kmean_h = (jnp.where(klane, kmean, 0.0), jnp.where(klane, 0.0, kmean))

    def rows(i):
        return pl.ds(pl.multiple_of(i * bl, bl), bl)

    def kv_tiles(j):
        kb = k_ref[rows(j), :] * scale
        vb = v_ref[rows(j), :]
        kp = [jnp.where(hmask[h], kb, (spare[h] == j).astype(F32)).astype(BF16) for h in range(2)]
        vp = [jnp.where(hmask[h], vb, 1.0).astype(BF16) for h in range(2)]
        return kp, vp

    def diag_body(i, carry):
        qf = q_ref[rows(i), :]
        kp, vp = kv_tiles(i)
        past = blk < i
        for h in range(2):
            bs = jnp.where(past, _mm3(kmean_h[h], qf, "nt"), MASK_VALUE)
            rank = jnp.zeros((nbp, bl), jnp.int32)
            for jp in range(nb):
                row = bs[jp:jp + 1, :]
                beats = (row > bs) | ((row == bs) & (jp < blk))
                rank = rank + beats.astype(jnp.int32)
            sel_t = ((rank < n_sel) & past).astype(BF16)
            picked = _dot(sel_t, place[h], "tn")
            bias = jnp.where((spare[h] >= 0) & (spare[h] < nb) & (picked < 0.5), MASK_VALUE, 0.0)
            qh = jnp.where(hmask[h], qf, 0.0)
            qa_ref[h, rows(i), :] = (qh + bias).astype(BF16)

            s = jnp.where(causal, _dot(qh.astype(BF16), kp[h], "nt"), MASK_VALUE)
            m = jnp.max(s, axis=1, keepdims=True)
            p = jnp.exp(s - m)
            m_ref[h, rows(i), :] = jnp.broadcast_to(m, (bl, LANE))
            acc_ref[h, rows(i), :] = _dot(p.astype(BF16), vp[h])
        return carry

    lax.fori_loop(0, nb, diag_body, 0)

    def key_body(j, carry):
        kp, vp = kv_tiles(j)

        def q_body(t, carry2):
            rs = pl.ds(pl.multiple_of(t * qt, qt), qt)
            s = [_dot(qa_ref[h, rs, :], kp[h], "nt") for h in range(2)]
            for h in range(2):
                m_prev = m_ref[h, rs, :]
                m_new = jnp.maximum(m_prev, jnp.max(s[h], axis=1, keepdims=True))
                p = jnp.exp(s[h] - m_new[:, 0:1])
                m_ref[h, rs, :] = m_new
                acc_ref[h, rs, :] = (acc_ref[h, rs, :] * jnp.exp(m_prev - m_new)
                                     + _dot(p.astype(BF16), vp[h]))
            return carry2

        lax.fori_loop((j + 1) // 2, nb // 2, q_body, 0)
        return carry

    lax.fori_loop(0, nb - 1, key_body, 0)

    def out_body(i, carry):
        a0 = acc_ref[0, rows(i), :]
        a1 = acc_ref[1, rows(i), :]
        o_ref[rows(i), :] = jnp.where(head0, a0 / pltpu.roll(a0, HEAD_DIM, 1), a1 / pltpu.roll(a1, HEAD_DIM, 1))
        return carry

    lax.fori_loop(0, nb, out_body, 0)


def _moba(proj, b, s):
    t = proj.shape[0]
    nb = s // MOBA_BLOCK
    n_sel = min(MOBA_TOPK, nb - 1)
    nhp = 512 // LANE
    qc, kc, vc = COL_MOBA // LANE, (COL_MOBA + 512) // LANE, (COL_MOBA + 1024) // LANE
    return pl.pallas_call(
        functools.partial(_moba_body, nb, n_sel),
        grid=(b, nhp),
        in_specs=[
            pl.BlockSpec((s, LANE), lambda bi, hp: (bi, qc + hp)),
            pl.BlockSpec((s, LANE), lambda bi, hp: (bi, kc + hp)),
            pl.BlockSpec((s, LANE), lambda bi, hp: (bi, vc + hp)),
        ],
        out_specs=pl.BlockSpec((s, LANE), lambda bi, hp: (bi, hp)),
        out_shape=jax.ShapeDtypeStruct((t, 512), F32),
        scratch_shapes=[pltpu.VMEM((2, s, LANE), F32), pltpu.VMEM((2, s, LANE), F32),
                        pltpu.VMEM((2, s, LANE), BF16)],
        compiler_params=_cparams(("parallel", "parallel")),
        name="moba_attn",
    )(proj, proj, proj)


def _mlstm_body(qk_ref, v_ref, o_ref, gcol_ref, grow_ref, cw_ref, cb_ref, bcol_ref, brow_ref,
                hng_ref, y_ref, cst_ref, n_ref, m_ref, prev_ref):
    c = CHUNK
    n = N_HEADS * c

    @pl.when(pl.program_id(1) == 0)
    def _():
        cst_ref[...] = jnp.zeros_like(cst_ref)
        n_ref[...] = jnp.zeros_like(n_ref)
        m_ref[...] = jnp.zeros_like(m_ref)
        prev_ref[...] = jnp.zeros_like(prev_ref)

    x = qk_ref[...]
    prev8 = prev_ref[...]
    cw = cw_ref[...]
    conv = (cb_ref[...] + cw[3:4] * x + cw[2:3] * _lagged(x, prev8, 1)
            + cw[1:2] * _lagged(x, prev8, 2) + cw[0:1] * _lagged(x, prev8, 3))
    prev_ref[...] = x[c - SUBLANE:, :]
    qk = conv * jax.nn.sigmoid(conv)
    q = qk[:, 0:RW]
    k = qk[:, RW:2 * RW] * (HEAD_DIM ** -0.5)
    v = v_ref[...]

    same, rp, cp, _ = _head_masks(n)
    hm = same.astype(F32)
    incl = same & (rp >= cp)
    q_mt = _tile_heads(q) * hm
    k_t = _tile_heads(k)
    k_mt = k_t * hm
    v_mt = _tile_heads(v) * hm

    gcol = gcol_ref[0] + bcol_ref[...]
    grow = grow_ref[0] + brow_ref[...]
    li_col = gcol[:, 0:1]
    lf_col = -_softplus(-gcol[:, 1:2])
    li_row = grow[0:1, :]
    lf_row = -_softplus(-grow[1:2, :])
    lf_cb = jnp.broadcast_to(lf_col, (n, LANE))
    lf_rb = jnp.broadcast_to(lf_row, (SUBLANE, n))
    b_col = _mm2l(incl.astype(F32), lf_cb)[:, 0:1]
    bend_col = _mm2l(hm, lf_cb)[:, 0:1]
    b_row = _mm2r(lf_rb, (same & (rp <= cp)).astype(F32))[0:1, :]
    bend_row = _mm2r(lf_rb, hm)[0:1, :]

    m_col = m_ref[...]
    dmat = jnp.where(incl, b_col - b_row + li_row, -jnp.inf)
    inter = b_col + m_col
    m_t = jnp.maximum(inter, jnp.max(dmat, axis=1, keepdims=True))
    wts = jnp.exp(dmat - m_t)
    s_inter = jnp.exp(inter - m_t)
    cst = cst_ref[...]
    n_row = n_ref[...]
    qk_w = _mm1(q_mt, k_t, "nt") * wts
    num = s_inter * _mm1(q_mt, cst, "nt") + _mm1(qk_w, v_mt)
    den = (s_inter * jnp.sum(q_mt * n_row, axis=1, keepdims=True)
           + jnp.sum(qk_w, axis=1, keepdims=True))
    hh = num / jnp.maximum(jnp.abs(den), jnp.exp(-m_t))
    y = hh[0:c] + hh[c:2 * c] + hh[2 * c:3 * c] + hh[3 * c:4 * c]

    g_col = bend_col - b_col + li_col
    g_row = bend_row - b_row + li_row
    g_max = jnp.max(jnp.where(same, g_row, -jnp.inf), axis=1, keepdims=True)
    m_new = jnp.maximum(bend_col + m_col, g_max)
    w_col = jnp.exp(g_col - m_new)
    scale_col = jnp.exp(bend_col + m_col - m_new)
    scale_row = jnp.max(jnp.where(same, scale_col, 0.0), axis=0, keepdims=True)
    cst_ref[...] = scale_col * cst + _mm1(v_mt * w_col, k_mt, "tn")
    n_ref[...] = scale_row * n_row + jnp.sum(k_mt * w_col, axis=0, keepdims=True)
    m_ref[...] = m_new

    mean = _mm2r(y, hm) * (1.0 / HEAD_DIM)
    d = y - mean
    var = _mm2r(d * d, hm) * (1.0 / HEAD_DIM)
    y_ref[...] = d * lax.rsqrt(var + NORM_EPS) * hng_ref[...] * jax.nn.sigmoid(o_ref[...])


def _mlstm(proj, b, s, conv_w, conv_b, i_b, f_b, hn_g):
    t = proj.shape[0]
    c = CHUNK
    nc = s // c
    n = N_HEADS * c
    gates = proj[:, COL_MLSTM_G:COL_MLSTM_G + 2 * N_HEADS].reshape(b * nc, c, 2, N_HEADS)
    gcol = gates.transpose(0, 3, 1, 2).reshape(b * nc, n, 2)
    grow = gates.transpose(0, 2, 3, 1).reshape(b * nc, 2, n)
    bias = jnp.stack([jnp.repeat(i_b, c), jnp.repeat(f_b, c)], axis=0)
    row = lambda bi, ci: (bi * nc + ci, 0)
    const = lambda bi, ci: (0, 0)
    return pl.pallas_call(
        _mlstm_body,
        grid=(b, nc),
        in_specs=[
            pl.BlockSpec((c, 2 * RW), lambda bi, ci: (bi * nc + ci, COL_MLSTM_QK // (2 * RW))),
            pl.BlockSpec((c, RW), lambda bi, ci: (bi * nc + ci, COL_MLSTM_V // RW)),
            pl.BlockSpec((c, RW), lambda bi, ci: (bi * nc + ci, COL_MLSTM_O // RW)),
            pl.BlockSpec((1, n, 2), lambda bi, ci: (bi * nc + ci, 0, 0)),
            pl.BlockSpec((1, 2, n), lambda bi, ci: (bi * nc + ci, 0, 0)),
            pl.BlockSpec(conv_w.shape, const),
            pl.BlockSpec((1, 2 * RW), const),
            pl.BlockSpec((n, 2), const),
            pl.BlockSpec((2, n), const),
            pl.BlockSpec((1, RW), const),
        ],
        out_specs=pl.BlockSpec((c, RW), row),
        out_shape=jax.ShapeDtypeStruct((t, RW), F32),
        scratch_shapes=[pltpu.VMEM((n, n), F32), pltpu.VMEM((1, n), F32), pltpu.VMEM((n, 1), F32),
                        pltpu.VMEM((SUBLANE, 2 * RW), F32)],
        compiler_params=_cparams(("parallel", "arbitrary")),
        name="mlstm_chunk",
    )(proj, proj, proj, gcol, grow, conv_w, conv_b.reshape(1, -1), bias.T, bias,
      hn_g.reshape(1, -1))


def _merge_body(x_ref, ya_ref, yb_ref, yc_ref, ga_ref, gb_ref, gc_ref, wa_ref, wb_ref, wc_ref,
                wo_ref, g_ref, o_ref):
    merged = (jax.nn.sigmoid(ga_ref[...]) * _mm1(ya_ref[...], wa_ref[...])
              + jax.nn.sigmoid(gb_ref[...]) * _mm1(yb_ref[...], wb_ref[...])
              + jax.nn.sigmoid(gc_ref[...]) * _mm1(yc_ref[...], wc_ref[...]))
    o_ref[...] = x_ref[...] + _rms(_mm1(merged, wo_ref[...]), g_ref[...])


def _merge(x2d, proj, ya, yb, yc, wa, wb, wc, wo, g, tm):
    t, d = x2d.shape
    row = lambda i: (i, 0)
    const = lambda i: (0, 0)
    return pl.pallas_call(
        _merge_body,
        grid=(t // tm,),
        in_specs=[
            pl.BlockSpec((tm, d), row),
            pl.BlockSpec((tm, ya.shape[1]), row),
            pl.BlockSpec((tm, yb.shape[1]), row),
            pl.BlockSpec((tm, yc.shape[1]), row),
            pl.BlockSpec((tm, d), lambda i: (i, 0)),
            pl.BlockSpec((tm, d), lambda i: (i, 1)),
            pl.BlockSpec((tm, d), lambda i: (i, 2)),
            pl.BlockSpec(wa.shape, const),
            pl.BlockSpec(wb.shape, const),
            pl.BlockSpec(wc.shape, const),
            pl.BlockSpec(wo.shape, const),
            pl.BlockSpec((1, d), const),
        ],
        out_specs=pl.BlockSpec((tm, d), row),
        out_shape=jax.ShapeDtypeStruct((t, d), F32),
        compiler_params=_cparams(("parallel",)),
        name="merge_out",
    )(x2d, ya, yb, yc, proj, proj, proj, wa, wb, wc, wo, g.reshape(1, d))


def _ffn_body(nt_seq, n_ff, x_ref, g1_ref, upg_ref, upv_ref, cwg_ref, cwv_ref, cbg_ref, cbv_ref,
              down_ref, g2_ref, g3_ref, pg_ref, pp_ref, p_ref, o_ref, h_ref, acc_ref, ugp_ref, uvp_ref):
    i = pl.program_id(0)
    j = pl.program_id(1)
    tm = x_ref.shape[0]

    @pl.when(j == 0)
    def _():
        h_ref[...] = _rms(x_ref[...], g1_ref[...]).astype(BF16)
        acc_ref[...] = jnp.zeros_like(acc_ref)

    @pl.when(i % nt_seq == 0)
    def _():
        ugp_ref[j] = jnp.zeros(ugp_ref.shape[1:], F32)
        uvp_ref[j] = jnp.zeros(uvp_ref.shape[1:], F32)

    h = h_ref[...]
    ug = _dot(h, upg_ref[...])
    uv = _dot(h, upv_ref[...])
    pg = ugp_ref[j]
    pv = uvp_ref[j]
    ugp_ref[j] = ug[tm - SUBLANE:, :]
    uvp_ref[j] = uv[tm - SUBLANE:, :]
    cwg = cwg_ref[...]
    cwv = cwv_ref[...]
    cg = cbg_ref[...] + cwg[2:3] * ug + cwg[1:2] * _lagged(ug, pg, 1) + cwg[0:1] * _lagged(ug, pg, 2)
    cv = cbv_ref[...] + cwv[2:3] * uv + cwv[1:2] * _lagged(uv, pv, 1) + cwv[0:1] * _lagged(uv, pv, 2)
    act = jax.nn.gelu(cg, approximate=True) * cv
    acc_ref[...] += _mm1(act, down_ref[...])

    @pl.when(j == n_ff - 1)
    def _():
        x2 = x_ref[...] + _rms(acc_ref[...], g2_ref[...])
        gate = jax.nn.sigmoid(_mm1(_rms(x2, g3_ref[...]), pg_ref[...]))
        o_ref[...] = x2 + gate * _mm1(p_ref[...], pp_ref[...])


def _ffn(x2d, p2d, s, g1, up, cw, cb, down, g2, g3, pgate, pproj, tm, tf):
    t, d = x2d.shape
    dff = down.shape[0]
    n_ff = dff // tf
    nt_seq = s // tm
    ple = p2d.shape[1]
    row = lambda i, j: (i, 0)
    const = lambda i, j: (0, 0)
    cb2 = cb.reshape(1, -1)
    return pl.pallas_call(
        functools.partial(_ffn_body, nt_seq, n_ff),
        grid=(t // tm, n_ff),
        in_specs=[
            pl.BlockSpec((tm, d), row),
            pl.BlockSpec((1, d), const),
            pl.BlockSpec((d, tf), lambda i, j: (0, j)),
            pl.BlockSpec((d, tf), lambda i, j: (0, n_ff + j)),
            pl.BlockSpec((cw.shape[0], tf), lambda i, j: (0, j)),
            pl.BlockSpec((cw.shape[0], tf), lambda i, j: (0, n_ff + j)),
            pl.BlockSpec((1, tf), lambda i, j: (0, j)),
            pl.BlockSpec((1, tf), lambda i, j: (0, n_ff + j)),
            pl.BlockSpec((tf, d), lambda i, j: (j, 0)),
            pl.BlockSpec((1, d), const),
            pl.BlockSpec((1, d), const),
            pl.BlockSpec(pgate.shape, const),
            pl.BlockSpec(pproj.shape, const),
            pl.BlockSpec((tm, ple), row),
        ],
        out_specs=pl.BlockSpec((tm, d), row),
        out_shape=jax.ShapeDtypeStruct((t, d), F32),
        scratch_shapes=[pltpu.VMEM((tm, d), BF16), pltpu.VMEM((tm, d), F32),
                        pltpu.VMEM((n_ff, SUBLANE, tf), F32), pltpu.VMEM((n_ff, SUBLANE, tf), F32)],
        compiler_params=_cparams(("arbitrary", "arbitrary")),
        name="ffn_ple",
    )(x2d, g1.reshape(1, d), up, up, cw, cw, cb2, cb2, down, g2.reshape(1, d), g3.reshape(1, d),
      pgate, pproj, p2d)


def _pack_w_in(w):
    d = w.shape[0]
    rwkv = w[:, 0:1024]
    moba = w[:, 1024:2560]
    ml = w[:, 2560:3592]
    gate = w[:, 3592:6664]
    pad = jnp.zeros((d, PACKED_WIDTH - COL_MLSTM_G - 2 * N_HEADS), w.dtype)
    packed = jnp.concatenate([gate, rwkv, moba, ml, pad], axis=1)
    return packed.astype(BF16)


def _row_tile(t, want):
    return want if t % want == 0 else t


def kernel(x, p, ln_mix_pre, ln_mix_post, ln_ffn_pre, ln_ffn_post, ln_ple, w_in, rwkv_mu, rwkv_w0, rwkv_w2, rwkv_a0, rwkv_a2, rwkv_g2, rwkv_k_k, rwkv_k_a, rwkv_r_k, rwkv_gn_g, rwkv_gn_b, rwkv_v0, rwkv_v1, rwkv_v2, mlstm_conv_w, mlstm_conv_b, mlstm_i_b, mlstm_f_b, mlstm_hn_g, w_br_rwkv, w_br_moba, w_br_mlstm, w_out, ffn_up, ffn_conv_w, ffn_conv_b, ffn_down, ple_proj, ple_gate):
    b, s, d = x.shape
    depth = w_in.shape[0]
    t = b * s
    assert d == 1024 and s % MOBA_BLOCK == 0 and w_in.shape[2] == 6664
    xf = x.reshape(t, d)
    tm_proj = _row_tile(t, 1024)
    tm = min(512, s)
    r2 = lambda a: a.reshape(1, -1)
    v_first = None
    for i in range(depth):
        proj = _norm_proj(xf, ln_mix_pre[i], _pack_w_in(w_in[i]), tm_proj, 1024)
        params = [r2(rwkv_mu[i]), r2(rwkv_w0[i]), rwkv_w2[i], r2(rwkv_a0[i]), rwkv_a2[i], rwkv_g2[i],
                  r2(rwkv_k_k[i]), r2(rwkv_k_a[i]), r2(rwkv_r_k[i]), r2(rwkv_gn_g[i]), r2(rwkv_gn_b[i])]
        if i > 0:
            params += [r2(rwkv_v0[i - 1]), rwkv_v1[i - 1], rwkv_v2[i - 1]]
        y_a, v_cur = _rwkv(proj, b, s, params, v_first if i > 0 else None)
        if i == 0:
            v_first = v_cur
        y_b = _moba(proj, b, s)
        y_c = _mlstm(proj, b, s, mlstm_conv_w[i], mlstm_conv_b[i], mlstm_i_b[i], mlstm_f_b[i],
                     mlstm_hn_g[i])
        xf = _merge(xf, proj, y_a, y_b, y_c, w_br_rwkv[i].astype(BF16), w_br_moba[i].astype(BF16),
                    w_br_mlstm[i].astype(BF16), w_out[i].astype(BF16), ln_mix_post[i], tm)
        xf = _ffn(xf, p[i].reshape(t, -1), s, ln_ffn_pre[i], ffn_up[i].astype(BF16), ffn_conv_w[i],
                  ffn_conv_b[i], ffn_down[i].astype(BF16), ln_ffn_post[i], ln_ple[i],
                  ple_gate[i].astype(BF16), ple_proj[i].astype(BF16), tm, 256)
    return xf.reshape(b, s, d)
```

```python
import functools

import jax
import jax.numpy as jnp
from jax import lax
from jax.experimental import pallas as pl
from jax.experimental.pallas import tpu as pltpu

F32 = jnp.float32
BF16 = jnp.bfloat16

HEAD_DIM = 64
N_HEADS = 4
RW = N_HEADS * HEAD_DIM
BATCH_TILE = 4
CHUNK = 64
MOBA_BLOCK = 256
MOBA_TOPK = 3
NORM_EPS = 1e-6
RWKV_GN_EPS = 64e-5
MASK_VALUE = -1e30
LANE = 128
SUBLANE = 8
VMEM_LIMIT = 56 * 1024 * 1024

COL_GATE = 0
COL_RWKV = 3072
COL_MOBA = 4096
COL_MLSTM_QK = 5632
COL_MLSTM_V = 6144
COL_MLSTM_O = 6400
COL_MLSTM_G = 6656
PACKED_WIDTH = 7168

_DIMS = {
    "nn": (((1,), (0,)), ((), ())),
    "nt": (((1,), (1,)), ((), ())),
    "tn": (((0,), (0,)), ((), ())),
}


def _dot(a, b, dims="nn"):
    return lax.dot_general(a, b, _DIMS[dims], preferred_element_type=F32)


def _split(a):
    hi = a.astype(BF16)
    lo = (a - hi.astype(F32)).astype(BF16)
    return hi, lo


def _mm1(a, b, dims="nn"):
    return _dot(a.astype(BF16), b.astype(BF16), dims)


def _mm3(a, b, dims="nn"):
    ah, al = _split(a)
    bh, bl = _split(b)
    return _dot(ah, bh, dims) + (_dot(ah, bl, dims) + _dot(al, bh, dims))


def _mm2r(a, e, dims="nn"):
    ah, al = _split(a)
    eb = e.astype(BF16)
    return _dot(ah, eb, dims) + _dot(al, eb, dims)


def _mm2l(e, a, dims="nn"):
    ah, al = _split(a)
    eb = e.astype(BF16)
    return _dot(eb, ah, dims) + _dot(eb, al, dims)


def _softplus(x):
    return jnp.maximum(x, 0.0) + jnp.log(1.0 + jnp.exp(-jnp.abs(x)))


def _rms(x, g):
    ms = jnp.mean(x * x, axis=-1, keepdims=True)
    return x * lax.rsqrt(ms + NORM_EPS) * g


def _lagged(x, prev8, lag):
    full = jnp.concatenate([prev8, x], axis=0)
    return pltpu.roll(full, lag, 0)[SUBLANE:, :]


def _tile_heads(x):
    return jnp.concatenate([x] * N_HEADS, axis=-2)


def _per(fn, *arrs):
    return jnp.stack([fn(*(a[i] for a in arrs)) for i in range(arrs[0].shape[0])])


def _rows(fn, x, w):
    lead = x.shape[:-1]
    return fn(x.reshape(-1, x.shape[-1]), w).reshape(*lead, -1)


def _bmm1(a, b, dims="nn"):
    return _per(lambda x, y: _mm1(x, y, dims), a, b)


def _batch_tile(b):
    return BATCH_TILE if b % BATCH_TILE == 0 else 1


def _head_masks(n):
    ri = lax.broadcasted_iota(jnp.int32, (n, n), 0)
    ci = lax.broadcasted_iota(jnp.int32, (n, n), 1)
    same = (ri >> 6) == (ci >> 6)
    rp = ri & 63
    cp = ci & 63
    return same, rp, cp, ri == ci


def _cparams(sem):
    return pltpu.CompilerParams(dimension_semantics=sem, vmem_limit_bytes=VMEM_LIMIT)


def _proj_body(x_ref, g_ref, w_ref, o_ref, h_ref):
    @pl.when(pl.program_id(1) == 0)
    def _():
        h_ref[...] = _rms(x_ref[...], g_ref[...]).astype(BF16)

    o_ref[...] = _dot(h_ref[...], w_ref[...])


def _norm_proj(x2d, g, w, tm, tn):
    t, d = x2d.shape
    n = w.shape[1]
    return pl.pallas_call(
        _proj_body,
        grid=(t // tm, n // tn),
        in_specs=[
            pl.BlockSpec((tm, d), lambda i, j: (i, 0)),
            pl.BlockSpec((1, d), lambda i, j: (0, 0)),
            pl.BlockSpec((d, tn), lambda i, j: (0, j)),
        ],
        out_specs=pl.BlockSpec((tm, tn), lambda i, j: (i, j)),
        out_shape=jax.ShapeDtypeStruct((t, n), F32),
        scratch_shapes=[pltpu.VMEM((tm, d), BF16)],
        compiler_params=_cparams(("parallel", "arbitrary")),
        name="norm_proj",
    )(x2d, g.reshape(1, d), w)


def _rwkv_body(has_vres, *refs):
    if has_vres:
        (slab_ref, vf_ref, mu_ref, w0_ref, w2_ref, a0_ref, a2_ref, g2_ref, kk_ref, ka_ref,
         rk_ref, gg_ref, gb_ref, v0_ref, v1_ref, v2_ref, y_ref, vo_ref, st_ref, prev_ref) = refs
    else:
        (slab_ref, mu_ref, w0_ref, w2_ref, a0_ref, a2_ref, g2_ref, kk_ref, ka_ref,
         rk_ref, gg_ref, gb_ref, y_ref, vo_ref, st_ref, prev_ref) = refs
    c = CHUNK

    @pl.when(pl.program_id(1) == 0)
    def _():
        st_ref[...] = jnp.zeros_like(st_ref)
        prev_ref[...] = jnp.zeros_like(prev_ref)

    slab = slab_ref[...]
    shifted = _per(lambda x, p8: _lagged(x, p8, 1), slab, prev_ref[...])
    prev_ref[...] = slab[:, c - SUBLANE:, :]
    xs = slab + mu_ref[...] * (shifted - slab)
    r = xs[:, :, 0:RW]
    k = xs[:, :, RW:2 * RW]
    v = xs[:, :, 2 * RW:3 * RW]
    xw = xs[:, :, 768:832]
    xa = xs[:, :, 832:896]
    xg = xs[:, :, 896:1024]
    wlog = -_softplus(-(w0_ref[...] + _rows(_mm3, jnp.tanh(xw), w2_ref[...]))) - 0.5
    alr = jax.nn.sigmoid(a0_ref[...] + _rows(_mm3, xa, a2_ref[...]))
    g = _rows(_mm3, jax.nn.sigmoid(xg), g2_ref[...])
    if has_vres:
        mix = jax.nn.sigmoid(v0_ref[...] + _rows(_mm3, _rows(_mm3, v, v1_ref[...]), v2_ref[...]))
        v = v + (vf_ref[...] - v) * mix
    vo_ref[...] = v

    n = N_HEADS * c
    same, rp, cp, eye = _head_masks(n)
    hm = same.astype(F32)
    strict = same & (rp > cp)
    incl = same & (rp >= cp)

    kk = k * kk_ref[...]
    kk = kk / jnp.maximum(jnp.sqrt(_rows(_mm2r, kk * kk, hm)), 1e-12)
    k2 = k * (1.0 + (alr - 1.0) * ka_ref[...])

    lw = -jnp.exp(wlog)
    ti = lax.broadcasted_iota(jnp.int32, (c, c), 0)
    tj = lax.broadcasted_iota(jnp.int32, (c, c), 1)
    tril = (ti >= tj).astype(F32)
    cs = _per(lambda x: _mm2l(tril, x), lw)
    cs_end = cs[:, c - 1:c, :]
    p_in = jnp.exp(cs)
    p_ex = jnp.exp(cs - lw)
    p_inv = jnp.exp(-cs)
    p_tail = jnp.exp(cs_end - cs)

    a_mt = _tile_heads(-kk * p_ex) * hm
    r_mt = _tile_heads(r * p_in) * hm
    b_t = _tile_heads(kk * alr * p_inv)
    k_t = _tile_heads(k2 * p_inv)
    v_mt = _tile_heads(v) * hm
    b_tail = _tile_heads(kk * alr * p_tail) * hm
    k_tail = _tile_heads(k2 * p_tail) * hm

    l_ab = jnp.where(strict, _bmm1(a_mt, b_t, "nt"), 0.0)
    l_ak = jnp.where(strict, _bmm1(a_mt, k_t, "nt"), 0.0)
    m_rb = jnp.where(incl, _bmm1(r_mt, b_t, "nt"), 0.0)
    m_rk = jnp.where(incl, _bmm1(r_mt, k_t, "nt"), 0.0)

    tinv = eye.astype(F32) + l_ab
    npow = l_ab
    for _ in range(5):
        npow = _bmm1(npow, npow)
        tinv = tinv + _bmm1(tinv, npow)

    st = st_ref[...]
    u = _bmm1(tinv, _bmm1(a_mt, st, "nt") + _bmm1(l_ak, v_mt))
    yy = _bmm1(r_mt, st, "nt") + _bmm1(m_rb, u) + _bmm1(m_rk, v_mt)
    y = yy[:, 0:c] + yy[:, c:2 * c] + yy[:, 2 * c:3 * c] + yy[:, 3 * c:4 * c]
    st_ref[...] = st * p_in[:, c - 1:c, :] + _bmm1(u, b_tail, "tn") + _bmm1(v_mt, k_tail, "tn")

    mean = _rows(_mm2r, y, hm) * (1.0 / HEAD_DIM)
    d = y - mean
    var = _rows(_mm2r, d * d, hm) * (1.0 / HEAD_DIM)
    yn = d * lax.rsqrt(var + RWKV_GN_EPS) * gg_ref[...] + gb_ref[...]
    bonus = _rows(_mm2r, r * k2 * rk_ref[...], hm) * v
    y_ref[...] = (yn + bonus) * g


def _rwkv(proj3, params, v_first3):
    b, s, _ = proj3.shape
    nc = s // CHUNK
    bt = _batch_tile(b)
    has_vres = v_first3 is not None
    row = lambda bi, ci: (bi, ci, 0)
    const = lambda bi, ci: (0, 0)
    in_specs = [pl.BlockSpec((bt, CHUNK, 1024), lambda bi, ci: (bi, ci, COL_RWKV // 1024))]
    args = [proj3]
    if has_vres:
        in_specs.append(pl.BlockSpec((bt, CHUNK, RW), row))
        args.append(v_first3)
    for prm in params:
        in_specs.append(pl.BlockSpec(prm.shape, const))
        args.append(prm)
    n = N_HEADS * CHUNK
    return pl.pallas_call(
        functools.partial(_rwkv_body, has_vres),
        grid=(b // bt, nc),
        in_specs=in_specs,
        out_specs=[pl.BlockSpec((bt, CHUNK, RW), row), pl.BlockSpec((bt, CHUNK, RW), row)],
        out_shape=[jax.ShapeDtypeStruct((b, s, RW), F32), jax.ShapeDtypeStruct((b, s, RW), F32)],
        scratch_shapes=[pltpu.VMEM((bt, n, n), F32), pltpu.VMEM((bt, SUBLANE, 1024), F32)],
        compiler_params=_cparams(("parallel", "arbitrary")),
        name="rwkv7_chunk",
    )(*args)


def _moba_body(nb, n_sel, q_ref, k_ref, v_ref, o_ref, m_ref, acc_ref, qa_ref):
    bl = MOBA_BLOCK
    qt = 2 * bl
    nbp =-(-nb // SUBLANE) * SUBLANE
    scale = HEAD_DIM ** -0.5
    lane = lax.broadcasted_iota(jnp.int32, (bl, LANE), 1)
    head0 = lane < HEAD_DIM
    hmask = (head0, jnp.logical_not(head0))
    spare = (lane - HEAD_DIM, lane)
    blk = lax.broadcasted_iota(jnp.int32, (nbp, bl), 0)
    qpos = lax.broadcasted_iota(jnp.int32, (bl, bl), 0)
    kpos = lax.broadcasted_iota(jnp.int32, (bl, bl), 1)
    causal = kpos <= qpos
    e_row = lax.broadcasted_iota(jnp.int32, (nbp, LANE), 0)
    e_lane = lax.broadcasted_iota(jnp.int32, (nbp, LANE), 1)
    place = ((e_lane == e_row + HEAD_DIM).astype(BF16), (e_lane == e_row).astype(BF16))
    klane = e_lane < HEAD_DIM

    kmean = jnp.mean(k_ref[...].reshape(nb, bl, LANE), axis=1)
    if nbp > nb:
        kmean = jnp.concatenate([kmean, jnp.zeros((nbp - nb, LANE), F32)], axis=0)
    kmean_h = (jnp.where(klane, kmean, 0.0), jnp.where(klane, 0.0, kmean))

    def rows(i):
        return pl.ds(pl.multiple_of(i * bl, bl), bl)

    def kv_tiles(j):
        kb = k_ref[rows(j), :] * scale
        vb = v_ref[rows(j), :]
        kp = [jnp.where(hmask[h], kb, (spare[h] == j).astype(F32)).astype(BF16) for h in range(2)]
        vp = [jnp.where(hmask[h], vb, 1.0).astype(BF16) for h in range(2)]
        return kp, vp

    def diag_body(i, carry):
        qf = q_ref[rows(i), :]
        kp, vp = kv_tiles(i)
        past = blk < i
        qh = [jnp.where(hmask[h], qf, 0.0) for h in range(2)]
        s = [_dot(qh[h].astype(BF16), kp[h], "nt") for h in range(2)]
        bs = [jnp.where(past, _mm3(kmean_h[h], qf, "nt"), MASK_VALUE) for h in range(2)]
        sel_t = []
        for h in range(2):
            rank = jnp.zeros((nbp, bl), jnp.int32)
            for jp in range(nb):
                row = bs[h][jp:jp + 1, :]
                beats = (row > bs[h]) | ((row == bs[h]) & (jp < blk))
                rank = rank + beats.astype(jnp.int32)
            sel_t.append(((rank < n_sel) & past).astype(BF16))
        picked = [_dot(sel_t[h], place[h], "tn") for h in range(2)]
        for h in range(2):
            sm = jnp.where(causal, s[h], MASK_VALUE)
            m = jnp.max(sm, axis=1, keepdims=True)
            p = jnp.exp(sm - m)
            m_ref[h, rows(i), :] = jnp.broadcast_to(m, (bl, LANE))
            acc_ref[h, rows(i), :] = _dot(p.astype(BF16), vp[h])
            bias = jnp.where((spare[h] >= 0) & (spare[h] < nb) & (picked[h] < 0.5), MASK_VALUE, 0.0)
            qa_ref[h, rows(i), :] = (qh[h] + bias).astype(BF16)
        return carry

    lax.fori_loop(0, nb, diag_body, 0)

    def key_body(j, carry):
        kp, vp = kv_tiles(j)

        def q_body(t, carry2):
            rs = pl.ds(pl.multiple_of(t * qt, qt), qt)
            s = [_dot(qa_ref[h, rs, :], kp[h], "nt") for h in range(2)]
            for h in range(2):
                m_prev = m_ref[h, rs, :]
                m_new = jnp.maximum(m_prev, jnp.max(s[h], axis=1, keepdims=True))
                p = jnp.exp(s[h] - jnp.concatenate([m_new, m_new], axis=1))
                m_ref[h, rs, :] = m_new
                acc_ref[h, rs, :] = (acc_ref[h, rs, :] * jnp.exp(m_prev - m_new)
                                     + _dot(p.astype(BF16), vp[h]))
            return carry2

        lax.fori_loop((j + 1) // 2, nb // 2, q_body, 0)
        return carry

    lax.fori_loop(0, nb - 1, key_body, 0)

    def out_body(i, carry):
        a0 = acc_ref[0, rows(i), :]
        a1 = acc_ref[1, rows(i), :]
        o_ref[rows(i), :] = jnp.where(head0, a0 / pltpu.roll(a0, HEAD_DIM, 1), a1 / pltpu.roll(a1, HEAD_DIM, 1))
        return carry

    lax.fori_loop(0, nb, out_body, 0)


def _moba(proj, b, s):
    t = proj.shape[0]
    nb = s // MOBA_BLOCK
    n_sel = min(MOBA_TOPK, nb - 1)
    nhp = 512 // LANE
    qc, kc, vc = COL_MOBA // LANE, (COL_MOBA + 512) // LANE, (COL_MOBA + 1024) // LANE
    return pl.pallas_call(
        functools.partial(_moba_body, nb, n_sel),
        grid=(b, nhp),
        in_specs=[
            pl.BlockSpec((s, LANE), lambda bi, hp: (bi, qc + hp)),
            pl.BlockSpec((s, LANE), lambda bi, hp: (bi, kc + hp)),
            pl.BlockSpec((s, LANE), lambda bi, hp: (bi, vc + hp)),
        ],
        out_specs=pl.BlockSpec((s, LANE), lambda bi, hp: (bi, hp)),
        out_shape=jax.ShapeDtypeStruct((t, 512), F32),
        scratch_shapes=[pltpu.VMEM((2, s, LANE), F32), pltpu.VMEM((2, s, LANE), F32),
                        pltpu.VMEM((2, s, LANE), BF16)],
        compiler_params=_cparams(("parallel", "parallel")),
        name="moba_attn",
    )(proj, proj, proj)


def _mlstm_body(qk_ref, v_ref, o_ref, gcol_ref, grow_ref, cw_ref, cb_ref, bcol_ref, brow_ref,
                hng_ref, y_ref, cst_ref, n_ref, m_ref, prev_ref):
    c = CHUNK
    n = N_HEADS * c
    bt = qk_ref.shape[0]

    @pl.when(pl.program_id(1) == 0)
    def _():
        cst_ref[...] = jnp.zeros_like(cst_ref)
        n_ref[...] = jnp.zeros_like(n_ref)
        m_ref[...] = jnp.zeros_like(m_ref)
        prev_ref[...] = jnp.zeros_like(prev_ref)

    x = qk_ref[...]
    prev8 = prev_ref[...]
    cw = cw_ref[...]
    lag = lambda j: _per(lambda xx, p8: _lagged(xx, p8, j), x, prev8)
    conv = cb_ref[...] + cw[3:4] * x + cw[2:3] * lag(1) + cw[1:2] * lag(2) + cw[0:1] * lag(3)
    prev_ref[...] = x[:, c - SUBLANE:, :]
    qk = conv * jax.nn.sigmoid(conv)
    q = qk[:, :, 0:RW]
    k = qk[:, :, RW:2 * RW] * (HEAD_DIM ** -0.5)
    v = v_ref[...]

    same, rp, cp, _ = _head_masks(n)
    hm = same.astype(F32)
    incl = same & (rp >= cp)
    q_mt = _tile_heads(q) * hm
    k_t = _tile_heads(k)
    k_mt = k_t * hm
    v_mt = _tile_heads(v) * hm

    gcol = gcol_ref[:, 0] + bcol_ref[...]
    grow = grow_ref[:, 0] + brow_ref[...]
    li_col = gcol[:, :, 0:1]
    lf_col = -_softplus(-gcol[:, :, 1:2])
    li_row = grow[:, 0:1, :]
    lf_row = -_softplus(-grow[:, 1:2, :])
    lf_cb = jnp.broadcast_to(lf_col, (bt, n, LANE))
    lf_rb = jnp.broadcast_to(lf_row, (bt, SUBLANE, n))
    inclf = incl.astype(F32)
    b_col = _per(lambda z: _mm2l(inclf, z), lf_cb)[:, :, 0:1]
    bend_col = _per(lambda z: _mm2l(hm, z), lf_cb)[:, :, 0:1]
    b_row = _rows(_mm2r, lf_rb, (same & (rp <= cp)).astype(F32))[:, 0:1, :]
    bend_row = _rows(_mm2r, lf_rb, hm)[:, 0:1, :]

    m_col = m_ref[...]
    dmat = jnp.where(incl, b_col - b_row + li_row, -jnp.inf)
    inter = b_col + m_col
    m_t = jnp.maximum(inter, jnp.max(dmat, axis=2, keepdims=True))
    wts = jnp.exp(dmat - m_t)
    s_inter = jnp.exp(inter - m_t)
    cst = cst_ref[...]
    n_row = n_ref[...]
    qk_w = _bmm1(q_mt, k_t, "nt") * wts
    num = s_inter * _bmm1(q_mt, cst, "nt") + _bmm1(qk_w, v_mt)
    den = (s_inter * jnp.sum(q_mt * n_row, axis=2, keepdims=True)
           + jnp.sum(qk_w, axis=2, keepdims=True))
    hh = num / jnp.maximum(jnp.abs(den), jnp.exp(-m_t))
    y = hh[:, 0:c] + hh[:, c:2 * c] + hh[:, 2 * c:3 * c] + hh[:, 3 * c:4 * c]

    g_col = bend_col - b_col + li_col
    g_row = bend_row - b_row + li_row
    g_max = jnp.max(jnp.where(same, g_row, -jnp.inf), axis=2, keepdims=True)
    m_new = jnp.maximum(bend_col + m_col, g_max)
    w_col = jnp.exp(g_col - m_new)
    scale_col = jnp.exp(bend_col + m_col - m_new)
    scale_row = jnp.max(jnp.where(same, scale_col, 0.0), axis=1, keepdims=True)
    cst_ref[...] = scale_col * cst + _bmm1(v_mt * w_col, k_mt, "tn")
    n_ref[...] = scale_row * n_row + jnp.sum(k_mt * w_col, axis=1, keepdims=True)
    m_ref[...] = m_new

    mean = _rows(_mm2r, y, hm) * (1.0 / HEAD_DIM)
    d = y - mean
    var = _rows(_mm2r, d * d, hm) * (1.0 / HEAD_DIM)
    y_ref[...] = d * lax.rsqrt(var + NORM_EPS) * hng_ref[...] * jax.nn.sigmoid(o_ref[...])


def _mlstm(proj3, conv_w, conv_b, i_b, f_b, hn_g):
    b, s, _ = proj3.shape
    c = CHUNK
    nc = s // c
    n = N_HEADS * c
    bt = _batch_tile(b)
    gates = proj3[:, :, COL_MLSTM_G:COL_MLSTM_G + 2 * N_HEADS].reshape(b, nc, c, 2, N_HEADS)
    gcol = gates.transpose(0, 1, 4, 2, 3).reshape(b, nc, n, 2)
    grow = gates.transpose(0, 1, 3, 4, 2).reshape(b, nc, 2, n)
    bias = jnp.stack([jnp.repeat(i_b, c), jnp.repeat(f_b, c)], axis=0)
    row = lambda bi, ci: (bi, ci, 0)
    const = lambda bi, ci: (0, 0)
    return pl.pallas_call(
        _mlstm_body,
        grid=(b // bt, nc),
        in_specs=[
            pl.BlockSpec((bt, c, 2 * RW), lambda bi, ci: (bi, ci, COL_MLSTM_QK // (2 * RW))),
            pl.BlockSpec((bt, c, RW), lambda bi, ci: (bi, ci, COL_MLSTM_V // RW)),
            pl.BlockSpec((bt, c, RW), lambda bi, ci: (bi, ci, COL_MLSTM_O // RW)),
            pl.BlockSpec((bt, 1, n, 2), lambda bi, ci: (bi, ci, 0, 0)),
            pl.BlockSpec((bt, 1, 2, n), lambda bi, ci: (bi, ci, 0, 0)),
            pl.BlockSpec(conv_w.shape, const),
            pl.BlockSpec((1, 2 * RW), const),
            pl.BlockSpec((n, 2), const),
            pl.BlockSpec((2, n), const),
            pl.BlockSpec((1, RW), const),
        ],
        out_specs=pl.BlockSpec((bt, c, RW), row),
        out_shape=jax.ShapeDtypeStruct((b, s, RW), F32),
        scratch_shapes=[pltpu.VMEM((bt, n, n), F32), pltpu.VMEM((bt, 1, n), F32),
                        pltpu.VMEM((bt, n, 1), F32), pltpu.VMEM((bt, SUBLANE, 2 * RW), F32)],
        compiler_params=_cparams(("parallel", "arbitrary")),
        name="mlstm_chunk",
    )(proj3, proj3, proj3, gcol, grow, conv_w, conv_b.reshape(1, -1), bias.T, bias,
      hn_g.reshape(1, -1))


def _merge_body(x_ref, ya_ref, yb_ref, yc_ref, ga_ref, gb_ref, gc_ref, wa_ref, wb_ref, wc_ref,
                wo_ref, g_ref, o_ref):
    merged = (jax.nn.sigmoid(ga_ref[...]) * _mm1(ya_ref[...], wa_ref[...])
              + jax.nn.sigmoid(gb_ref[...]) * _mm1(yb_ref[...], wb_ref[...])
              + jax.nn.sigmoid(gc_ref[...]) * _mm1(yc_ref[...], wc_ref[...]))
    o_ref[...] = x_ref[...] + _rms(_mm1(merged, wo_ref[...]), g_ref[...])


def _merge(x2d, proj, ya, yb, yc, wa, wb, wc, wo, g, tm):
    t, d = x2d.shape
    row = lambda i: (i, 0)
    const = lambda i: (0, 0)
    return pl.pallas_call(
        _merge_body,
        grid=(t // tm,),
        in_specs=[
            pl.BlockSpec((tm, d), row),
            pl.BlockSpec((tm, ya.shape[1]), row),
            pl.BlockSpec((tm, yb.shape[1]), row),
            pl.BlockSpec((tm, yc.shape[1]), row),
            pl.BlockSpec((tm, d), lambda i: (i, 0)),
            pl.BlockSpec((tm, d), lambda i: (i, 1)),
            pl.BlockSpec((tm, d), lambda i: (i, 2)),
            pl.BlockSpec(wa.shape, const),
            pl.BlockSpec(wb.shape, const),
            pl.BlockSpec(wc.shape, const),
            pl.BlockSpec(wo.shape, const),
            pl.BlockSpec((1, d), const),
        ],
        out_specs=pl.BlockSpec((tm, d), row),
        out_shape=jax.ShapeDtypeStruct((t, d), F32),
        compiler_params=_cparams(("parallel",)),
        name="merge_out",
    )(x2d, ya, yb, yc, proj, proj, proj, wa, wb, wc, wo, g.reshape(1, d))


def _ffn_body(nt_seq, n_ff, x_ref, g1_ref, upg_ref, upv_ref, cwg_ref, cwv_ref, cbg_ref, cbv_ref,
              down_ref, g2_ref, g3_ref, pg_ref, pp_ref, p_ref, o_ref, h_ref, acc_ref, ugp_ref, uvp_ref):
    i = pl.program_id(0)
    j = pl.program_id(1)
    tm = x_ref.shape[0]

    @pl.when(j == 0)
    def _():
        h_ref[...] = _rms(x_ref[...], g1_ref[...]).astype(BF16)
        acc_ref[...] = jnp.zeros_like(acc_ref)

    @pl.when(i % nt_seq == 0)
    def _():
        ugp_ref[j] = jnp.zeros(ugp_ref.shape[1:], F32)
        uvp_ref[j] = jnp.zeros(uvp_ref.shape[1:], F32)

    h = h_ref[...]
    ug = _dot(h, upg_ref[...])
    uv = _dot(h, upv_ref[...])
    pg = ugp_ref[j]
    pv = uvp_ref[j]
    ugp_ref[j] = ug[tm - SUBLANE:, :]
    uvp_ref[j] = uv[tm - SUBLANE:, :]
    cwg = cwg_ref[...]
    cwv = cwv_ref[...]
    cg = cbg_ref[...] + cwg[2:3] * ug + cwg[1:2] * _lagged(ug, pg, 1) + cwg[0:1] * _lagged(ug, pg, 2)
    cv = cbv_ref[...] + cwv[2:3] * uv + cwv[1:2] * _lagged(uv, pv, 1) + cwv[0:1] * _lagged(uv, pv, 2)
    act = jax.nn.gelu(cg, approximate=True) * cv
    acc_ref[...] += _mm1(act, down_ref[...])

    @pl.when(j == n_ff - 1)
    def _():
        x2 = x_ref[...] + _rms(acc_ref[...], g2_ref[...])
        gate = jax.nn.sigmoid(_mm1(_rms(x2, g3_ref[...]), pg_ref[...]))
        o_ref[...] = x2 + gate * _mm1(p_ref[...], pp_ref[...])


def _ffn(x2d, p2d, s, g1, up, cw, cb, down, g2, g3, pgate, pproj, tm, tf):
    t, d = x2d.shape
    dff = down.shape[0]
    n_ff = dff // tf
    nt_seq = s // tm
    ple = p2d.shape[1]
    row = lambda i, j: (i, 0)
    const = lambda i, j: (0, 0)
    cb2 = cb.reshape(1, -1)
    return pl.pallas_call(
        functools.partial(_ffn_body, nt_seq, n_ff),
        grid=(t // tm, n_ff),
        in_specs=[
            pl.BlockSpec((tm, d), row),
            pl.BlockSpec((1, d), const),
            pl.BlockSpec((d, tf), lambda i, j: (0, j)),
            pl.BlockSpec((d, tf), lambda i, j: (0, n_ff + j)),
            pl.BlockSpec((cw.shape[0], tf), lambda i, j: (0, j)),
            pl.BlockSpec((cw.shape[0], tf), lambda i, j: (0, n_ff + j)),
            pl.BlockSpec((1, tf), lambda i, j: (0, j)),
            pl.BlockSpec((1, tf), lambda i, j: (0, n_ff + j)),
            pl.BlockSpec((tf, d), lambda i, j: (j, 0)),
            pl.BlockSpec((1, d), const),
            pl.BlockSpec((1, d), const),
            pl.BlockSpec(pgate.shape, const),
            pl.BlockSpec(pproj.shape, const),
            pl.BlockSpec((tm, ple), row),
        ],
        out_specs=pl.BlockSpec((tm, d), row),
        out_shape=jax.ShapeDtypeStruct((t, d), F32),
        scratch_shapes=[pltpu.VMEM((tm, d), BF16), pltpu.VMEM((tm, d), F32),
                        pltpu.VMEM((n_ff, SUBLANE, tf), F32), pltpu.VMEM((n_ff, SUBLANE, tf), F32)],
        compiler_params=_cparams(("arbitrary", "arbitrary")),
        name="ffn_ple",
    )(x2d, g1.reshape(1, d), up, up, cw, cw, cb2, cb2, down, g2.reshape(1, d), g3.reshape(1, d),
      pgate, pproj, p2d)


def _pack_w_in(w):
    d = w.shape[0]
    rwkv = w[:, 0:1024]
    moba = w[:, 1024:2560]
    ml = w[:, 2560:3592]
    gate = w[:, 3592:6664]
    pad = jnp.zeros((d, PACKED_WIDTH - COL_MLSTM_G - 2 * N_HEADS), w.dtype)
    packed = jnp.concatenate([gate, rwkv, moba, ml, pad], axis=1)
    return packed.astype(BF16)


def _row_tile(t, want):
    return want if t % want == 0 else t


def kernel(x, p, ln_mix_pre, ln_mix_post, ln_ffn_pre, ln_ffn_post, ln_ple, w_in, rwkv_mu, rwkv_w0, rwkv_w2, rwkv_a0, rwkv_a2, rwkv_g2, rwkv_k_k, rwkv_k_a, rwkv_r_k, rwkv_gn_g, rwkv_gn_b, rwkv_v0, rwkv_v1, rwkv_v2, mlstm_conv_w, mlstm_conv_b, mlstm_i_b, mlstm_f_b, mlstm_hn_g, w_br_rwkv, w_br_moba, w_br_mlstm, w_out, ffn_up, ffn_conv_w, ffn_conv_b, ffn_down, ple_proj, ple_gate):
    b, s, d = x.shape
    depth = w_in.shape[0]
    t = b * s
    assert d == 1024 and s % MOBA_BLOCK == 0 and w_in.shape[2] == 6664
    xf = x.reshape(t, d)
    tm_proj = _row_tile(t, 1024)
    tm = min(512, s)
    r2 = lambda a: a.reshape(1, -1)
    v_first = None
    for i in range(depth):
        proj = _norm_proj(xf, ln_mix_pre[i], _pack_w_in(w_in[i]), tm_proj, 1024)
        params = [r2(rwkv_mu[i]), r2(rwkv_w0[i]), rwkv_w2[i], r2(rwkv_a0[i]), rwkv_a2[i], rwkv_g2[i],
                  r2(rwkv_k_k[i]), r2(rwkv_k_a[i]), r2(rwkv_r_k[i]), r2(rwkv_gn_g[i]), r2(rwkv_gn_b[i])]
        if i > 0:
            params += [r2(rwkv_v0[i - 1]), rwkv_v1[i - 1], rwkv_v2[i - 1]]
        proj3 = proj.reshape(b, s, -1)
        y_a, v_cur = _rwkv(proj3, params, v_first if i > 0 else None)
        if i == 0:
            v_first = v_cur
        y_b = _moba(proj, b, s)
        y_c = _mlstm(proj3, mlstm_conv_w[i], mlstm_conv_b[i], mlstm_i_b[i], mlstm_f_b[i], mlstm_hn_g[i])
        xf = _merge(xf, proj, y_a.reshape(t, -1), y_b, y_c.reshape(t, -1), w_br_rwkv[i].astype(BF16), w_br_moba[i].astype(BF16),
                    w_br_mlstm[i].astype(BF16), w_out[i].astype(BF16), ln_mix_post[i], tm)
        xf = _ffn(xf, p[i].reshape(t, -1), s, ln_ffn_pre[i], ffn_up[i].astype(BF16), ffn_conv_w[i],
                  ffn_conv_b[i], ffn_down[i].astype(BF16), ln_ffn_post[i], ln_ple[i],
                  ple_gate[i].astype(BF16), ple_proj[i].astype(BF16), tm, 256)
    return xf.reshape(b, s, d)
```

```python
import functools

import jax
import jax.numpy as jnp
from jax import lax
from jax.experimental import pallas as pl
from jax.experimental.pallas import tpu as pltpu

F32 = jnp.float32
BF16 = jnp.bfloat16

HEAD_DIM = 64
N_HEADS = 4
RW = N_HEADS * HEAD_DIM
BATCH_TILE = 4
CHUNK = 64
MOBA_BLOCK = 256
MOBA_TOPK = 3
MOBA_QTILE_BLOCKS = 4
NORM_EPS = 1e-6
RWKV_GN_EPS = 64e-5
MASK_VALUE = -1e30
LOG2E = 1.4426950408889634
LANE = 128
SUBLANE = 8
VMEM_LIMIT = 56 * 1024 * 1024

COL_GATE = 0
COL_RWKV = 3072
COL_MOBA = 4096
COL_MLSTM_QK = 5632
COL_MLSTM_V = 6144
COL_MLSTM_O = 6400
COL_MLSTM_G = 6656
PACKED_WIDTH = 7168

_DIMS = {
    "nn": (((1,), (0,)), ((), ())),
    "nt": (((1,), (1,)), ((), ())),
    "tn": (((0,), (0,)), ((), ())),
}


def _dot(a, b, dims="nn"):
    return lax.dot_general(a, b, _DIMS[dims], preferred_element_type=F32)


def _split(a):
    hi = a.astype(BF16)
    lo = (a - hi.astype(F32)).astype(BF16)
    return hi, lo


def _mm1(a, b, dims="nn"):
    return _dot(a.astype(BF16), b.astype(BF16), dims)


def _mm3(a, b, dims="nn"):
    ah, al = _split(a)
    bh, bl = _split(b)
    return _dot(ah, bh, dims) + (_dot(ah, bl, dims) + _dot(al, bh, dims))


def _mm2r(a, e, dims="nn"):
    ah, al = _split(a)
    eb = e.astype(BF16)
    return _dot(ah, eb, dims) + _dot(al, eb, dims)


def _mm2l(e, a, dims="nn"):
    ah, al = _split(a)
    eb = e.astype(BF16)
    return _dot(eb, ah, dims) + _dot(eb, al, dims)


def _softplus(x):
    return jnp.maximum(x, 0.0) + jnp.log(1.0 + jnp.exp(-jnp.abs(x)))


def _rms(x, g):
    ms = jnp.mean(x * x, axis=-1, keepdims=True)
    return x * lax.rsqrt(ms + NORM_EPS) * g


def _lagged(x, prev8, lag):
    full = jnp.concatenate([prev8, x], axis=0)
    return pltpu.roll(full, lag, 0)[SUBLANE:, :]


def _tile_heads(x):
    return jnp.concatenate([x] * N_HEADS, axis=-2)


def _per(fn, *arrs):
    return jnp.stack([fn(*(a[i] for a in arrs)) for i in range(arrs[0].shape[0])])


def _rows(fn, x, w):
    lead = x.shape[:-1]
    return fn(x.reshape(-1, x.shape[-1]), w).reshape(*lead, -1)


def _bmm1(a, b, dims="nn"):
    return _per(lambda x, y: _mm1(x, y, dims), a, b)


def _batch_tile(b):
    return BATCH_TILE if b % BATCH_TILE == 0 else 1


def _head_masks(n):
    ri = lax.broadcasted_iota(jnp.int32, (n, n), 0)
    ci = lax.broadcasted_iota(jnp.int32, (n, n), 1)
    same = (ri >> 6) == (ci >> 6)
    rp = ri & 63
    cp = ci & 63
    return same, rp, cp, ri == ci


def _cparams(sem):
    return pltpu.CompilerParams(dimension_semantics=sem, vmem_limit_bytes=VMEM_LIMIT)


def _proj_body(x_ref, g_ref, w_ref, o_ref, h_ref):
    @pl.when(pl.program_id(1) == 0)
    def _():
        h_ref[...] = _rms(x_ref[...], g_ref[...]).astype(BF16)

    o_ref[...] = _dot(h_ref[...], w_ref[...]).astype(o_ref.dtype)


def _norm_proj(x2d, g, w, tm, tn):
    t, d = x2d.shape
    n = w.shape[1]
    return pl.pallas_call(
        _proj_body,
        grid=(t // tm, n // tn),
        in_specs=[
            pl.BlockSpec((tm, d), lambda i, j: (i, 0)),
            pl.BlockSpec((1, d), lambda i, j: (0, 0)),
            pl.BlockSpec((d, tn), lambda i, j: (0, j)),
        ],
        out_specs=pl.BlockSpec((tm, tn), lambda i, j: (i, j)),
        out_shape=jax.ShapeDtypeStruct((t, n), BF16),
        scratch_shapes=[pltpu.VMEM((tm, d), BF16)],
        compiler_params=_cparams(("parallel", "arbitrary")),
        name="norm_proj",
    )(x2d, g.reshape(1, d), w)


def _rwkv_body(has_vres, *refs):
    if has_vres:
        (slab_ref, vf_ref, mu_ref, w0_ref, w2_ref, a0_ref, a2_ref, g2_ref, kk_ref, ka_ref,
         rk_ref, gg_ref, gb_ref, v0_ref, v1_ref, v2_ref, y_ref, vo_ref, st_ref, prev_ref) = refs
    else:
        (slab_ref, mu_ref, w0_ref, w2_ref, a0_ref, a2_ref, g2_ref, kk_ref, ka_ref,
         rk_ref, gg_ref, gb_ref, y_ref, vo_ref, st_ref, prev_ref) = refs
    c = CHUNK

    @pl.when(pl.program_id(1) == 0)
    def _():
        st_ref[...] = jnp.zeros_like(st_ref)
        prev_ref[...] = jnp.zeros_like(prev_ref)

    slab = slab_ref[...].astype(F32)
    shifted = _per(lambda x, p8: _lagged(x, p8, 1), slab, prev_ref[...])
    prev_ref[...] = slab[:, c - SUBLANE:, :]
    xs = slab + mu_ref[...] * (shifted - slab)
    r = xs[:, :, 0:RW]
    k = xs[:, :, RW:2 * RW]
    v = xs[:, :, 2 * RW:3 * RW]
    xw = xs[:, :, 768:832]
    xa = xs[:, :, 832:896]
    xg = xs[:, :, 896:1024]
    wlog = -_softplus(-(w0_ref[...] + _rows(_mm3, jnp.tanh(xw), w2_ref[...]))) - 0.5
    alr = jax.nn.sigmoid(a0_ref[...] + _rows(_mm3, xa, a2_ref[...]))
    g = _rows(_mm3, jax.nn.sigmoid(xg), g2_ref[...])
    if has_vres:
        mix = jax.nn.sigmoid(v0_ref[...] + _rows(_mm3, _rows(_mm3, v, v1_ref[...]), v2_ref[...]))
        v = v + (vf_ref[...] - v) * mix
    vo_ref[...] = v

    n = N_HEADS * c
    same, rp, cp, eye = _head_masks(n)
    hm = same.astype(F32)
    strict = same & (rp > cp)
    incl = same & (rp >= cp)

    kk = k * kk_ref[...]
    kk = kk / jnp.maximum(jnp.sqrt(_rows(_mm2r, kk * kk, hm)), 1e-12)
    k2 = k * (1.0 + (alr - 1.0) * ka_ref[...])

    lw = -jnp.exp(wlog)
    ti = lax.broadcasted_iota(jnp.int32, (c, c), 0)
    tj = lax.broadcasted_iota(jnp.int32, (c, c), 1)
    tril = (ti >= tj).astype(F32)
    cs = _per(lambda x: _mm2l(tril, x), lw)
    cs_end = cs[:, c - 1:c, :]
    p_in = jnp.exp(cs)
    p_ex = jnp.exp(cs - lw)
    p_inv = jnp.exp(-cs)
    p_tail = jnp.exp(cs_end - cs)

    a_mt = _tile_heads(-kk * p_ex) * hm
    r_mt = _tile_heads(r * p_in) * hm
    b_t = _tile_heads(kk * alr * p_inv)
    k_t = _tile_heads(k2 * p_inv)
    v_mt = _tile_heads(v) * hm
    b_tail = _tile_heads(kk * alr * p_tail) * hm
    k_tail = _tile_heads(k2 * p_tail) * hm

    l_ab = jnp.where(strict, _bmm1(a_mt, b_t, "nt"), 0.0)
    l_ak = jnp.where(strict, _bmm1(a_mt, k_t, "nt"), 0.0)
    m_rb = jnp.where(incl, _bmm1(r_mt, b_t, "nt"), 0.0)
    m_rk = jnp.where(incl, _bmm1(r_mt, k_t, "nt"), 0.0)

    tinv = eye.astype(F32) + l_ab
    npow = l_ab
    for _ in range(5):
        npow = _bmm1(npow, npow)
        tinv = tinv + _bmm1(tinv, npow)

    st = st_ref[...]
    u = _bmm1(tinv, _bmm1(a_mt, st, "nt") + _bmm1(l_ak, v_mt))
    yy = _bmm1(r_mt, st, "nt") + _bmm1(m_rb, u) + _bmm1(m_rk, v_mt)
    y = yy[:, 0:c] + yy[:, c:2 * c] + yy[:, 2 * c:3 * c] + yy[:, 3 * c:4 * c]
    st_ref[...] = st * p_in[:, c - 1:c, :] + _bmm1(u, b_tail, "tn") + _bmm1(v_mt, k_tail, "tn")

    mean = _rows(_mm2r, y, hm) * (1.0 / HEAD_DIM)
    d = y - mean
    var = _rows(_mm2r, d * d, hm) * (1.0 / HEAD_DIM)
    yn = d * lax.rsqrt(var + RWKV_GN_EPS) * gg_ref[...] + gb_ref[...]
    bonus = _rows(_mm2r, r * k2 * rk_ref[...], hm) * v
    y_ref[...] = (yn + bonus) * g


def _rwkv(proj3, params, v_first3):
    b, s, _ = proj3.shape
    nc = s // CHUNK
    bt = _batch_tile(b)
    has_vres = v_first3 is not None
    row = lambda bi, ci: (bi, ci, 0)
    const = lambda bi, ci: (0, 0)
    in_specs = [pl.BlockSpec((bt, CHUNK, 1024), lambda bi, ci: (bi, ci, COL_RWKV // 1024))]
    args = [proj3]
    if has_vres:
        in_specs.append(pl.BlockSpec((bt, CHUNK, RW), row))
        args.append(v_first3)
    for prm in params:
        in_specs.append(pl.BlockSpec(prm.shape, const))
        args.append(prm)
    n = N_HEADS * CHUNK
    return pl.pallas_call(
        functools.partial(_rwkv_body, has_vres),
        grid=(b // bt, nc),
        in_specs=in_specs,
        out_specs=[pl.BlockSpec((bt, CHUNK, RW), row), pl.BlockSpec((bt, CHUNK, RW), row)],
        out_shape=[jax.ShapeDtypeStruct((b, s, RW), F32), jax.ShapeDtypeStruct((b, s, RW), F32)],
        scratch_shapes=[pltpu.VMEM((bt, n, n), F32), pltpu.VMEM((bt, SUBLANE, 1024), F32)],
        compiler_params=_cparams(("parallel", "arbitrary")),
        name="rwkv7_chunk",
    )(*args)


def _moba_body(nb, n_sel, q_ref, k_ref, v_ref, o_ref, m_ref, acc_ref, qa_ref):
    bl = MOBA_BLOCK
    qb = MOBA_QTILE_BLOCKS if nb % MOBA_QTILE_BLOCKS == 0 else 1
    qt = qb * bl
    nt = nb // qb
    nbp =-(-nb // SUBLANE) * SUBLANE
    scale = HEAD_DIM ** -0.5
    lane = lax.broadcasted_iota(jnp.int32, (bl, LANE), 1)
    head0 = lane < HEAD_DIM
    hmask = (head0, jnp.logical_not(head0))
    spare = (lane - HEAD_DIM, lane)
    blk = lax.broadcasted_iota(jnp.int32, (nbp, bl), 0)
    qpos = lax.broadcasted_iota(jnp.int32, (bl, bl), 0)
    kpos = lax.broadcasted_iota(jnp.int32, (bl, bl), 1)
    causal = kpos <= qpos
    e_row = lax.broadcasted_iota(jnp.int32, (nbp, LANE), 0)
    e_lane = lax.broadcasted_iota(jnp.int32, (nbp, LANE), 1)
    place = ((e_lane == e_row + HEAD_DIM).astype(BF16), (e_lane == e_row).astype(BF16))
    klane = e_lane < HEAD_DIM

    kmean = jnp.mean(k_ref[...].astype(F32).reshape(nb, bl, LANE), axis=1)
    if nbp > nb:
        kmean = jnp.concatenate([kmean, jnp.zeros((nbp - nb, LANE), F32)], axis=0)
    kmean_h = (jnp.where(klane, kmean, 0.0), jnp.where(klane, 0.0, kmean))

    def rows(i):
        return pl.ds(pl.multiple_of(i * bl, bl), bl)

    def kv_tiles(j):
        kb = k_ref[rows(j), :].astype(F32) * (scale * LOG2E)
        vb = v_ref[rows(j), :]
        kp = [jnp.where(hmask[h], kb, (spare[h] == j).astype(F32)).astype(BF16) for h in range(2)]
        vp = [jnp.where(hmask[h], vb, 1.0).astype(BF16) for h in range(2)]
        return kp, vp

    def diag_body(i, carry):
        qf = q_ref[rows(i), :].astype(F32)
        kp, vp = kv_tiles(i)
        past = blk < i
        qh = [jnp.where(hmask[h], qf, 0.0) for h in range(2)]
        s = [_dot(qh[h].astype(BF16), kp[h], "nt") for h in range(2)]
        bs = [jnp.where(past, _mm3(kmean_h[h], qf, "nt"), MASK_VALUE) for h in range(2)]
        sel_t = []
        for h in range(2):
            rank = jnp.zeros((nbp, bl), jnp.int32)
            for jp in range(nb):
                row = bs[h][jp:jp + 1, :]
                beats = (row > bs[h]) | ((row == bs[h]) & (jp < blk))
                rank = rank + beats.astype(jnp.int32)
            sel_t.append(((rank < n_sel) & past).astype(BF16))
        picked = [_dot(sel_t[h], place[h], "tn") for h in range(2)]
        for h in range(2):
            sm = jnp.where(causal, s[h], MASK_VALUE)
            m = jnp.max(sm, axis=1, keepdims=True)
            p = jnp.exp2(sm - m)
            m_ref[h, rows(i), :] = jnp.broadcast_to(m, (bl, LANE))
            acc_ref[h, rows(i), :] = _dot(p.astype(BF16), vp[h])
            bias = jnp.where((spare[h] >= 0) & (spare[h] < nb) & (picked[h] < 0.5), MASK_VALUE, 0.0)
            qa_ref[h, rows(i), :] = (qh[h] + bias).astype(BF16)
        return carry

    lax.fori_loop(0, nb, diag_body, 0)

    def key_body(j, carry):
        kp, vp = kv_tiles(j)

        def tile(t):
            return pl.ds(pl.multiple_of(t * qt, qt), qt)

        def q_body(t, carry2):
            rs = tile(t)
            s = [_dot(qa_ref[h, rs, :], kp[h], "nt") for h in range(2)]
            for h in range(2):
                m_prev = m_ref[h, rs, :]
                m_new = jnp.maximum(m_prev, jnp.max(s[h], axis=1, keepdims=True))
                p = jnp.exp2(s[h] - jnp.concatenate([m_new, m_new], axis=1))
                m_ref[h, rs, :] = m_new
                acc_ref[h, rs, :] = (acc_ref[h, rs, :] * jnp.exp2(m_prev - m_new)
                                     + _dot(p.astype(BF16), vp[h]))
            return carry2

        lax.fori_loop((j + 1) // qb, nt, q_body, 0)
        return carry

    lax.fori_loop(0, nb - 1, key_body, 0)

    def out_body(i, carry):
        a0 = acc_ref[0, rows(i), :]
        a1 = acc_ref[1, rows(i), :]
        o_ref[rows(i), :] = jnp.where(head0, a0 / pltpu.roll(a0, HEAD_DIM, 1), a1 / pltpu.roll(a1, HEAD_DIM, 1))
        return carry

    lax.fori_loop(0, nb, out_body, 0)


def _moba(proj, b, s):
    t = proj.shape[0]
    nb = s // MOBA_BLOCK
    n_sel = min(MOBA_TOPK, nb - 1)
    nhp = 512 // LANE
    qc, kc, vc = COL_MOBA // LANE, (COL_MOBA + 512) // LANE, (COL_MOBA + 1024) // LANE
    return pl.pallas_call(
        functools.partial(_moba_body, nb, n_sel),
        grid=(b, nhp),
        in_specs=[
            pl.BlockSpec((s, LANE), lambda bi, hp: (bi, qc + hp)),
            pl.BlockSpec((s, LANE), lambda bi, hp: (bi, kc + hp)),
            pl.BlockSpec((s, LANE), lambda bi, hp: (bi, vc + hp)),
        ],
        out_specs=pl.BlockSpec((s, LANE), lambda bi, hp: (bi, hp)),
        out_shape=jax.ShapeDtypeStruct((t, 512), F32),
        scratch_shapes=[pltpu.VMEM((2, s, LANE), F32), pltpu.VMEM((2, s, LANE), F32),
                        pltpu.VMEM((2, s, LANE), BF16)],
        compiler_params=_cparams(("parallel", "parallel")),
        name="moba_attn",
    )(proj, proj, proj)


def _mlstm_body(qk_ref, v_ref, o_ref, gcol_ref, grow_ref, cw_ref, cb_ref, bcol_ref, brow_ref,
                hng_ref, y_ref, cst_ref, n_ref, m_ref, prev_ref):
    c = CHUNK
    n = N_HEADS * c
    bt = qk_ref.shape[0]

    @pl.when(pl.program_id(1) == 0)
    def _():
        cst_ref[...] = jnp.zeros_like(cst_ref)
        n_ref[...] = jnp.zeros_like(n_ref)
        m_ref[...] = jnp.zeros_like(m_ref)
        prev_ref[...] = jnp.zeros_like(prev_ref)

    x = qk_ref[...].astype(F32)
    prev8 = prev_ref[...]
    cw = cw_ref[...]
    lag = lambda j: _per(lambda xx, p8: _lagged(xx, p8, j), x, prev8)
    conv = cb_ref[...] + cw[3:4] * x + cw[2:3] * lag(1) + cw[1:2] * lag(2) + cw[0:1] * lag(3)
    prev_ref[...] = x[:, c - SUBLANE:, :]
    qk = conv * jax.nn.sigmoid(conv)
    q = qk[:, :, 0:RW]
    k = qk[:, :, RW:2 * RW] * (HEAD_DIM ** -0.5)
    v = v_ref[...].astype(F32)

    same, rp, cp, _ = _head_masks(n)
    hm = same.astype(F32)
    incl = same & (rp >= cp)
    q_mt = _tile_heads(q) * hm
    k_t = _tile_heads(k)
    k_mt = k_t * hm
    v_mt = _tile_heads(v) * hm

    gcol = gcol_ref[:, 0] + bcol_ref[...]
    grow = grow_ref[:, 0] + brow_ref[...]
    li_col = gcol[:, :, 0:1]
    lf_col = -_softplus(-gcol[:, :, 1:2])
    li_row = grow[:, 0:1, :]
    lf_row = -_softplus(-grow[:, 1:2, :])
    lf_cb = jnp.broadcast_to(lf_col, (bt, n, LANE))
    lf_rb = jnp.broadcast_to(lf_row, (bt, SUBLANE, n))
    inclf = incl.astype(F32)
    b_col = _per(lambda z: _mm2l(inclf, z), lf_cb)[:, :, 0:1]
    bend_col = _per(lambda z: _mm2l(hm, z), lf_cb)[:, :, 0:1]
    b_row = _rows(_mm2r, lf_rb, (same & (rp <= cp)).astype(F32))[:, 0:1, :]
    bend_row = _rows(_mm2r, lf_rb, hm)[:, 0:1, :]

    m_col = m_ref[...]
    dmat = jnp.where(incl, b_col - b_row + li_row, -jnp.inf)
    inter = b_col + m_col
    m_t = jnp.maximum(inter, jnp.max(dmat, axis=2, keepdims=True))
    wts = jnp.exp(dmat - m_t)
    s_inter = jnp.exp(inter - m_t)
    cst = cst_ref[...]
    n_row = n_ref[...]
    qk_w = _bmm1(q_mt, k_t, "nt") * wts
    num = s_inter * _bmm1(q_mt, cst, "nt") + _bmm1(qk_w, v_mt)
    den = (s_inter * jnp.sum(q_mt * n_row, axis=2, keepdims=True)
           + jnp.sum(qk_w, axis=2, keepdims=True))
    hh = num / jnp.maximum(jnp.abs(den), jnp.exp(-m_t))
    y = hh[:, 0:c] + hh[:, c:2 * c] + hh[:, 2 * c:3 * c] + hh[:, 3 * c:4 * c]

    g_col = bend_col - b_col + li_col
    g_row = bend_row - b_row + li_row
    g_max = jnp.max(jnp.where(same, g_row, -jnp.inf), axis=2, keepdims=True)
    m_new = jnp.maximum(bend_col + m_col, g_max)
    w_col = jnp.exp(g_col - m_new)
    scale_col = jnp.exp(bend_col + m_col - m_new)
    scale_row = jnp.max(jnp.where(same, scale_col, 0.0), axis=1, keepdims=True)
    cst_ref[...] = scale_col * cst + _bmm1(v_mt * w_col, k_mt, "tn")
    n_ref[...] = scale_row * n_row + jnp.sum(k_mt * w_col, axis=1, keepdims=True)
    m_ref[...] = m_new

    mean = _rows(_mm2r, y, hm) * (1.0 / HEAD_DIM)
    d = y - mean
    var = _rows(_mm2r, d * d, hm) * (1.0 / HEAD_DIM)
    y_ref[...] = d * lax.rsqrt(var + NORM_EPS) * hng_ref[...] * jax.nn.sigmoid(o_ref[...].astype(F32))


def _mlstm(proj3, conv_w, conv_b, i_b, f_b, hn_g):
    b, s, _ = proj3.shape
    c = CHUNK
    nc = s // c
    n = N_HEADS * c
    bt = _batch_tile(b)
    gates = proj3[:, :, COL_MLSTM_G:COL_MLSTM_G + 2 * N_HEADS].astype(F32).reshape(b, nc, c, 2, N_HEADS)
    gcol = gates.transpose(0, 1, 4, 2, 3).reshape(b, nc, n, 2)
    grow = gates.transpose(0, 1, 3, 4, 2).reshape(b, nc, 2, n)
    bias = jnp.stack([jnp.repeat(i_b, c), jnp.repeat(f_b, c)], axis=0)
    row = lambda bi, ci: (bi, ci, 0)
    const = lambda bi, ci: (0, 0)
    return pl.pallas_call(
        _mlstm_body,
        grid=(b // bt, nc),
        in_specs=[
            pl.BlockSpec((bt, c, 2 * RW), lambda bi, ci: (bi, ci, COL_MLSTM_QK // (2 * RW))),
            pl.BlockSpec((bt, c, RW), lambda bi, ci: (bi, ci, COL_MLSTM_V // RW)),
            pl.BlockSpec((bt, c, RW), lambda bi, ci: (bi, ci, COL_MLSTM_O // RW)),
            pl.BlockSpec((bt, 1, n, 2), lambda bi, ci: (bi, ci, 0, 0)),
            pl.BlockSpec((bt, 1, 2, n), lambda bi, ci: (bi, ci, 0, 0)),
            pl.BlockSpec(conv_w.shape, const),
            pl.BlockSpec((1, 2 * RW), const),
            pl.BlockSpec((n, 2), const),
            pl.BlockSpec((2, n), const),
            pl.BlockSpec((1, RW), const),
        ],
        out_specs=pl.BlockSpec((bt, c, RW), row),
        out_shape=jax.ShapeDtypeStruct((b, s, RW), F32),
        scratch_shapes=[pltpu.VMEM((bt, n, n), F32), pltpu.VMEM((bt, 1, n), F32),
                        pltpu.VMEM((bt, n, 1), F32), pltpu.VMEM((bt, SUBLANE, 2 * RW), F32)],
        compiler_params=_cparams(("parallel", "arbitrary")),
        name="mlstm_chunk",
    )(proj3, proj3, proj3, gcol, grow, conv_w, conv_b.reshape(1, -1), bias.T, bias,
      hn_g.reshape(1, -1))


def _merge_body(x_ref, ya_ref, yb_ref, yc_ref, ga_ref, gb_ref, gc_ref, wa_ref, wb_ref, wc_ref,
                wo_ref, g_ref, o_ref):
    merged = (jax.nn.sigmoid(ga_ref[...].astype(F32)) * _mm1(ya_ref[...], wa_ref[...])
              + jax.nn.sigmoid(gb_ref[...].astype(F32)) * _mm1(yb_ref[...], wb_ref[...])
              + jax.nn.sigmoid(gc_ref[...].astype(F32)) * _mm1(yc_ref[...], wc_ref[...]))
    o_ref[...] = x_ref[...] + _rms(_mm1(merged, wo_ref[...]), g_ref[...])


def _merge(x2d, proj, ya, yb, yc, wa, wb, wc, wo, g, tm):
    t, d = x2d.shape
    row = lambda i: (i, 0)
    const = lambda i: (0, 0)
    return pl.pallas_call(
        _merge_body,
        grid=(t // tm,),
        in_specs=[
            pl.BlockSpec((tm, d), row),
            pl.BlockSpec((tm, ya.shape[1]), row),
            pl.BlockSpec((tm, yb.shape[1]), row),
            pl.BlockSpec((tm, yc.shape[1]), row),
            pl.BlockSpec((tm, d), lambda i: (i, 0)),
            pl.BlockSpec((tm, d), lambda i: (i, 1)),
            pl.BlockSpec((tm, d), lambda i: (i, 2)),
            pl.BlockSpec(wa.shape, const),
            pl.BlockSpec(wb.shape, const),
            pl.BlockSpec(wc.shape, const),
            pl.BlockSpec(wo.shape, const),
            pl.BlockSpec((1, d), const),
        ],
        out_specs=pl.BlockSpec((tm, d), row),
        out_shape=jax.ShapeDtypeStruct((t, d), F32),
        compiler_params=_cparams(("parallel",)),
        name="merge_out",
    )(x2d, ya, yb, yc, proj, proj, proj, wa, wb, wc, wo, g.reshape(1, d))


FFN_ROW_GROUPS = 4


def _ffn_body(nt_seq, n_ff, x_ref, g1_ref, upg_ref, upv_ref, cwg_ref, cwv_ref, cbg_ref, cbv_ref,
              down_ref, g2_ref, g3_ref, pg_ref, pp_ref, p_ref, o_ref, h_ref, acc_ref, ugp_ref, uvp_ref):
    i = pl.program_id(0)
    j = pl.program_id(1)
    tm = x_ref.shape[0]

    @pl.when(j == 0)
    def _():
        h_ref[...] = _rms(x_ref[...], g1_ref[...]).astype(BF16)
        acc_ref[...] = jnp.zeros_like(acc_ref)

    @pl.when(i % nt_seq == 0)
    def _():
        ugp_ref[j] = jnp.zeros(ugp_ref.shape[1:], F32)
        uvp_ref[j] = jnp.zeros(uvp_ref.shape[1:], F32)

    ngrp = FFN_ROW_GROUPS if tm % (FFN_ROW_GROUPS * SUBLANE) == 0 else 1
    rg = tm // ngrp
    upg = upg_ref[...]
    upv = upv_ref[...]
    hs = [h_ref[r * rg:(r + 1) * rg, :] for r in range(ngrp)]
    ug = [_dot(hr, upg) for hr in hs]
    uv = [_dot(hr, upv) for hr in hs]
    pg = [ugp_ref[j]] + [ug[r][rg - SUBLANE:, :] for r in range(ngrp - 1)]
    pv = [uvp_ref[j]] + [uv[r][rg - SUBLANE:, :] for r in range(ngrp - 1)]
    ugp_ref[j] = ug[-1][rg - SUBLANE:, :]
    uvp_ref[j] = uv[-1][rg - SUBLANE:, :]
    cwg = cwg_ref[...]
    cwv = cwv_ref[...]
    down = down_ref[...]
    for r in range(ngrp):
        cg = (cbg_ref[...] + cwg[2:3] * ug[r] + cwg[1:2] * _lagged(ug[r], pg[r], 1)
              + cwg[0:1] * _lagged(ug[r], pg[r], 2))
        cv = (cbv_ref[...] + cwv[2:3] * uv[r] + cwv[1:2] * _lagged(uv[r], pv[r], 1)
              + cwv[0:1] * _lagged(uv[r], pv[r], 2))
        act = jax.nn.gelu(cg, approximate=True) * cv
        acc_ref[r * rg:(r + 1) * rg, :] += _mm1(act, down)

    @pl.when(j == n_ff - 1)
    def _():
        x2 = x_ref[...] + _rms(acc_ref[...], g2_ref[...])
        gate = jax.nn.sigmoid(_mm1(_rms(x2, g3_ref[...]), pg_ref[...]))
        o_ref[...] = x2 + gate * _mm1(p_ref[...], pp_ref[...])


def _ffn(x2d, p2d, s, g1, up, cw, cb, down, g2, g3, pgate, pproj, tm, tf):
    t, d = x2d.shape
    dff = down.shape[0]
    n_ff = dff // tf
    nt_seq = s // tm
    ple = p2d.shape[1]
    row = lambda i, j: (i, 0)
    const = lambda i, j: (0, 0)
    cb2 = cb.reshape(1, -1)
    return pl.pallas_call(
        functools.partial(_ffn_body, nt_seq, n_ff),
        grid=(t // tm, n_ff),
        in_specs=[
            pl.BlockSpec((tm, d), row),
            pl.BlockSpec((1, d), const),
            pl.BlockSpec((d, tf), lambda i, j: (0, j)),
            pl.BlockSpec((d, tf), lambda i, j: (0, n_ff + j)),
            pl.BlockSpec((cw.shape[0], tf), lambda i, j: (0, j)),
            pl.BlockSpec((cw.shape[0], tf), lambda i, j: (0, n_ff + j)),
            pl.BlockSpec((1, tf), lambda i, j: (0, j)),
            pl.BlockSpec((1, tf), lambda i, j: (0, n_ff + j)),
            pl.BlockSpec((tf, d), lambda i, j: (j, 0)),
            pl.BlockSpec((1, d), const),
            pl.BlockSpec((1, d), const),
            pl.BlockSpec(pgate.shape, const),
            pl.BlockSpec(pproj.shape, const),
            pl.BlockSpec((tm, ple), row),
        ],
        out_specs=pl.BlockSpec((tm, d), row),
        out_shape=jax.ShapeDtypeStruct((t, d), F32),
        scratch_shapes=[pltpu.VMEM((tm, d), BF16), pltpu.VMEM((tm, d), F32),
                        pltpu.VMEM((n_ff, SUBLANE, tf), F32), pltpu.VMEM((n_ff, SUBLANE, tf), F32)],
        compiler_params=_cparams(("arbitrary", "arbitrary")),
        name="ffn_ple",
    )(x2d, g1.reshape(1, d), up, up, cw, cw, cb2, cb2, down, g2.reshape(1, d), g3.reshape(1, d),
      pgate, pproj, p2d)


def _pack_w_in(w):
    d = w.shape[0]
    rwkv = w[:, 0:1024]
    moba = w[:, 1024:2560]
    ml = w[:, 2560:3592]
    gate = w[:, 3592:6664]
    pad = jnp.zeros((d, PACKED_WIDTH - COL_MLSTM_G - 2 * N_HEADS), w.dtype)
    packed = jnp.concatenate([gate, rwkv, moba, ml, pad], axis=1)
    return packed.astype(BF16)


def _row_tile(t, want):
    return want if t % want == 0 else t


def kernel(x, p, ln_mix_pre, ln_mix_post, ln_ffn_pre, ln_ffn_post, ln_ple, w_in, rwkv_mu, rwkv_w0, rwkv_w2, rwkv_a0, rwkv_a2, rwkv_g2, rwkv_k_k, rwkv_k_a, rwkv_r_k, rwkv_gn_g, rwkv_gn_b, rwkv_v0, rwkv_v1, rwkv_v2, mlstm_conv_w, mlstm_conv_b, mlstm_i_b, mlstm_f_b, mlstm_hn_g, w_br_rwkv, w_br_moba, w_br_mlstm, w_out, ffn_up, ffn_conv_w, ffn_conv_b, ffn_down, ple_proj, ple_gate):
    b, s, d = x.shape
    depth = w_in.shape[0]
    t = b * s
    assert d == 1024 and s % MOBA_BLOCK == 0 and w_in.shape[2] == 6664
    xf = x.reshape(t, d)
    tm_proj = _row_tile(t, 1024)
    tm = min(512, s)
    tm_ffn = 1024 if s % 1024 == 0 else tm
    r2 = lambda a: a.reshape(1, -1)
    v_first = None
    for i in range(depth):
        proj = _norm_proj(xf, ln_mix_pre[i], _pack_w_in(w_in[i]), tm_proj, 1024)
        params = [r2(rwkv_mu[i]), r2(rwkv_w0[i]), rwkv_w2[i], r2(rwkv_a0[i]), rwkv_a2[i], rwkv_g2[i],
                  r2(rwkv_k_k[i]), r2(rwkv_k_a[i]), r2(rwkv_r_k[i]), r2(rwkv_gn_g[i]), r2(rwkv_gn_b[i])]
        if i > 0:
            params += [r2(rwkv_v0[i - 1]), rwkv_v1[i - 1], rwkv_v2[i - 1]]
        proj3 = proj.reshape(b, s, -1)
        y_a, v_cur = _rwkv(proj3, params, v_first if i > 0 else None)
        if i == 0:
            v_first = v_cur
        y_b = _moba(proj, b, s)
        y_c = _mlstm(proj3, mlstm_conv_w[i], mlstm_conv_b[i], mlstm_i_b[i], mlstm_f_b[i], mlstm_hn_g[i])
        xf = _merge(xf, proj, y_a.reshape(t, -1), y_b, y_c.reshape(t, -1), w_br_rwkv[i].astype(BF16), w_br_moba[i].astype(BF16),
                    w_br_mlstm[i].astype(BF16), w_out[i].astype(BF16), ln_mix_post[i], tm)
        xf = _ffn(xf, p[i].reshape(t, -1), s, ln_ffn_pre[i], ffn_up[i].astype(BF16), ffn_conv_w[i],
                  ffn_conv_b[i], ffn_down[i].astype(BF16), ln_ffn_post[i], ln_ple[i],
                  ple_gate[i].astype(BF16), ple_proj[i].astype(BF16), tm_ffn, 256)
    return xf.reshape(b, s, d)
```

```python
import functools

import jax
import jax.numpy as jnp
from jax import lax
from jax.experimental import pallas as pl
from jax.experimental.pallas import tpu as pltpu

F32 = jnp.float32
BF16 = jnp.bfloat16

HEAD_DIM = 64
N_HEADS = 4
RW = N_HEADS * HEAD_DIM
BATCH_TILE = 4
CHUNK = 64
MOBA_BLOCK = 256
MOBA_TOPK = 3
MOBA_QTILE_BLOCKS = 4
NORM_EPS = 1e-6
RWKV_GN_EPS = 64e-5
MASK_VALUE = -1e30
LOG2E = 1.4426950408889634
LANE = 128
SUBLANE = 8
VMEM_LIMIT = 56 * 1024 * 1024

COL_GATE = 0
COL_RWKV = 3072
COL_MOBA = 4096
COL_MLSTM_QK = 5632
COL_MLSTM_V = 6144
COL_MLSTM_O = 6400
COL_MLSTM_G = 6656
PACKED_WIDTH = 7168

_DIMS = {
    "nn": (((1,), (0,)), ((), ())),
    "nt": (((1,), (1,)), ((), ())),
    "tn": (((0,), (0,)), ((), ())),
}


def _dot(a, b, dims="nn"):
    return lax.dot_general(a, b, _DIMS[dims], preferred_element_type=F32)


def _split(a):
    hi = a.astype(BF16)
    lo = (a - hi.astype(F32)).astype(BF16)
    return hi, lo


def _mm1(a, b, dims="nn"):
    return _dot(a.astype(BF16), b.astype(BF16), dims)


def _mm3(a, b, dims="nn"):
    ah, al = _split(a)
    bh, bl = _split(b)
    return _dot(ah, bh, dims) + (_dot(ah, bl, dims) + _dot(al, bh, dims))


def _mm2r(a, e, dims="nn"):
    ah, al = _split(a)
    eb = e.astype(BF16)
    return _dot(ah, eb, dims) + _dot(al, eb, dims)


def _mm2l(e, a, dims="nn"):
    ah, al = _split(a)
    eb = e.astype(BF16)
    return _dot(eb, ah, dims) + _dot(eb, al, dims)


def _softplus(x):
    return jnp.maximum(x, 0.0) + jnp.log(1.0 + jnp.exp(-jnp.abs(x)))


def _rms(x, g):
    ms = jnp.mean(x * x, axis=-1, keepdims=True)
    return x * lax.rsqrt(ms + NORM_EPS) * g


def _lagged(x, prev8, lag):
    full = jnp.concatenate([prev8, x], axis=0)
    return pltpu.roll(full, lag, 0)[SUBLANE:, :]


def _tile_heads(x):
    return jnp.concatenate([x] * N_HEADS, axis=-2)


def _per(fn, *arrs):
    return jnp.stack([fn(*(a[i] for a in arrs)) for i in range(arrs[0].shape[0])])


def _rows(fn, x, w):
    lead = x.shape[:-1]
    return fn(x.reshape(-1, x.shape[-1]), w).reshape(*lead, -1)


def _bmm1(a, b, dims="nn"):
    return _per(lambda x, y: _mm1(x, y, dims), a, b)


def _batch_tile(b):
    return BATCH_TILE if b % BATCH_TILE == 0 else 1


def _head_masks(n):
    ri = lax.broadcasted_iota(jnp.int32, (n, n), 0)
    ci = lax.broadcasted_iota(jnp.int32, (n, n), 1)
    same = (ri >> 6) == (ci >> 6)
    rp = ri & 63
    cp = ci & 63
    return same, rp, cp, ri == ci


def _cparams(sem):
    return pltpu.CompilerParams(dimension_semantics=sem, vmem_limit_bytes=VMEM_LIMIT)


def _proj_body(x_ref, g_ref, w_ref, o_ref, h_ref):
    @pl.when(pl.program_id(1) == 0)
    def _():
        h_ref[...] = _rms(x_ref[...], g_ref[...]).astype(BF16)

    o_ref[...] = _dot(h_ref[...], w_ref[...]).astype(o_ref.dtype)


def _norm_proj(x2d, g, w, tm, tn):
    t, d = x2d.shape
    n = w.shape[1]
    return pl.pallas_call(
        _proj_body,
        grid=(t // tm, n // tn),
        in_specs=[
            pl.BlockSpec((tm, d), lambda i, j: (i, 0)),
            pl.BlockSpec((1, d), lambda i, j: (0, 0)),
            pl.BlockSpec((d, tn), lambda i, j: (0, j)),
        ],
        out_specs=pl.BlockSpec((tm, tn), lambda i, j: (i, j)),
        out_shape=jax.ShapeDtypeStruct((t, n), BF16),
        scratch_shapes=[pltpu.VMEM((tm, d), BF16)],
        compiler_params=_cparams(("parallel", "arbitrary")),
        name="norm_proj",
    )(x2d, g.reshape(1, d), w)


def _rwkv_body(has_vres, *refs):
    if has_vres:
        (slab_ref, vf_ref, mu_ref, w0_ref, w2_ref, a0_ref, a2_ref, g2_ref, kk_ref, ka_ref,
         rk_ref, gg_ref, gb_ref, v0_ref, v1_ref, v2_ref, y_ref, vo_ref, st_ref, prev_ref) = refs
    else:
        (slab_ref, mu_ref, w0_ref, w2_ref, a0_ref, a2_ref, g2_ref, kk_ref, ka_ref,
         rk_ref, gg_ref, gb_ref, y_ref, vo_ref, st_ref, prev_ref) = refs
    c = CHUNK

    @pl.when(pl.program_id(1) == 0)
    def _():
        st_ref[...] = jnp.zeros_like(st_ref)
        prev_ref[...] = jnp.zeros_like(prev_ref)

    slab = slab_ref[...].astype(F32)
    shifted = _per(lambda x, p8: _lagged(x, p8, 1), slab, prev_ref[...])
    prev_ref[...] = slab[:, c - SUBLANE:, :]
    xs = slab + mu_ref[...] * (shifted - slab)
    r = xs[:, :, 0:RW]
    k = xs[:, :, RW:2 * RW]
    v = xs[:, :, 2 * RW:3 * RW]
    xw = xs[:, :, 768:832]
    xa = xs[:, :, 832:896]
    xg = xs[:, :, 896:1024]
    wlog = -_softplus(-(w0_ref[...] + _rows(_mm3, jnp.tanh(xw), w2_ref[...]))) - 0.5
    alr = jax.nn.sigmoid(a0_ref[...] + _rows(_mm3, xa, a2_ref[...]))
    g = _rows(_mm3, jax.nn.sigmoid(xg), g2_ref[...])
    if has_vres:
        mix = jax.nn.sigmoid(v0_ref[...] + _rows(_mm3, _rows(_mm3, v, v1_ref[...]), v2_ref[...]))
        v = v + (vf_ref[...] - v) * mix
    vo_ref[...] = v

    n = N_HEADS * c
    same, rp, cp, eye = _head_masks(n)
    hm = same.astype(F32)
    strict = same & (rp > cp)
    incl = same & (rp >= cp)

    kk = k * kk_ref[...]
    kk = kk / jnp.maximum(jnp.sqrt(_rows(_mm2r, kk * kk, hm)), 1e-12)
    k2 = k * (1.0 + (alr - 1.0) * ka_ref[...])

    lw = -jnp.exp(wlog)
    ti = lax.broadcasted_iota(jnp.int32, (c, c), 0)
    tj = lax.broadcasted_iota(jnp.int32, (c, c), 1)
    tril = (ti >= tj).astype(F32)
    cs = _per(lambda x: _mm2l(tril, x), lw)
    cs_end = cs[:, c - 1:c, :]
    p_in = jnp.exp(cs)
    p_ex = jnp.exp(cs - lw)
    p_inv = jnp.exp(-cs)
    p_tail = jnp.exp(cs_end - cs)

    a_mt = _tile_heads(-kk * p_ex) * hm
    r_mt = _tile_heads(r * p_in) * hm
    b_t = _tile_heads(kk * alr * p_inv)
    k_t = _tile_heads(k2 * p_inv)
    v_mt = _tile_heads(v) * hm
    b_tail = _tile_heads(kk * alr * p_tail) * hm
    k_tail = _tile_heads(k2 * p_tail) * hm

    l_ab = jnp.where(strict, _bmm1(a_mt, b_t, "nt"), 0.0)
    l_ak = jnp.where(strict, _bmm1(a_mt, k_t, "nt"), 0.0)
    m_rb = jnp.where(incl, _bmm1(r_mt, b_t, "nt"), 0.0)
    m_rk = jnp.where(incl, _bmm1(r_mt, k_t, "nt"), 0.0)

    tinv = eye.astype(F32) + l_ab
    npow = l_ab
    for _ in range(5):
        npow = _bmm1(npow, npow)
        tinv = tinv + _bmm1(tinv, npow)

    st = st_ref[...]
    u = _bmm1(tinv, _bmm1(a_mt, st, "nt") + _bmm1(l_ak, v_mt))
    yy = _bmm1(r_mt, st, "nt") + _bmm1(m_rb, u) + _bmm1(m_rk, v_mt)
    y = yy[:, 0:c] + yy[:, c:2 * c] + yy[:, 2 * c:3 * c] + yy[:, 3 * c:4 * c]
    st_ref[...] = st * p_in[:, c - 1:c, :] + _bmm1(u, b_tail, "tn") + _bmm1(v_mt, k_tail, "tn")

    mean = _rows(_mm2r, y, hm) * (1.0 / HEAD_DIM)
    d = y - mean
    var = _rows(_mm2r, d * d, hm) * (1.0 / HEAD_DIM)
    yn = d * lax.rsqrt(var + RWKV_GN_EPS) * gg_ref[...] + gb_ref[...]
    bonus = _rows(_mm2r, r * k2 * rk_ref[...], hm) * v
    y_ref[...] = (yn + bonus) * g


def _rwkv(proj3, params, v_first3):
    b, s, _ = proj3.shape
    nc = s // CHUNK
    bt = _batch_tile(b)
    has_vres = v_first3 is not None
    row = lambda bi, ci: (bi, ci, 0)
    const = lambda bi, ci: (0, 0)
    in_specs = [pl.BlockSpec((bt, CHUNK, 1024), lambda bi, ci: (bi, ci, COL_RWKV // 1024))]
    args = [proj3]
    if has_vres:
        in_specs.append(pl.BlockSpec((bt, CHUNK, RW), row))
        args.append(v_first3)
    for prm in params:
        in_specs.append(pl.BlockSpec(prm.shape, const))
        args.append(prm)
    n = N_HEADS * CHUNK
    return pl.pallas_call(
        functools.partial(_rwkv_body, has_vres),
        grid=(b // bt, nc),
        in_specs=in_specs,
        out_specs=[pl.BlockSpec((bt, CHUNK, RW), row), pl.BlockSpec((bt, CHUNK, RW), row)],
        out_shape=[jax.ShapeDtypeStruct((b, s, RW), F32), jax.ShapeDtypeStruct((b, s, RW), F32)],
        scratch_shapes=[pltpu.VMEM((bt, n, n), F32), pltpu.VMEM((bt, SUBLANE, 1024), F32)],
        compiler_params=_cparams(("parallel", "arbitrary")),
        name="rwkv7_chunk",
    )(*args)


def _moba_body(nb, n_sel, q_ref, k_ref, v_ref, o_ref, m_ref, acc_ref, qa_ref):
    bl = MOBA_BLOCK
    qb = MOBA_QTILE_BLOCKS if nb % MOBA_QTILE_BLOCKS == 0 else 1
    qt = qb * bl
    nt = nb // qb
    nbp =-(-nb // SUBLANE) * SUBLANE
    scale = HEAD_DIM ** -0.5
    lane = lax.broadcasted_iota(jnp.int32, (bl, LANE), 1)
    head0 = lane < HEAD_DIM
    hmask = (head0, jnp.logical_not(head0))
    spare = (lane - HEAD_DIM, lane)
    blk = lax.broadcasted_iota(jnp.int32, (nbp, bl), 0)
    qpos = lax.broadcasted_iota(jnp.int32, (bl, bl), 0)
    kpos = lax.broadcasted_iota(jnp.int32, (bl, bl), 1)
    causal = kpos <= qpos
    e_row = lax.broadcasted_iota(jnp.int32, (nbp, LANE), 0)
    e_lane = lax.broadcasted_iota(jnp.int32, (nbp, LANE), 1)
    place = ((e_lane == e_row + HEAD_DIM).astype(BF16), (e_lane == e_row).astype(BF16))
    klane = e_lane < HEAD_DIM

    kmean = jnp.mean(k_ref[...].astype(F32).reshape(nb, bl, LANE), axis=1)
    if nbp > nb:
        kmean = jnp.concatenate([kmean, jnp.zeros((nbp - nb, LANE), F32)], axis=0)
    kmean_h = (jnp.where(klane, kmean, 0.0), jnp.where(klane, 0.0, kmean))

    def rows(i):
        if isinstance(i, int):
            return pl.ds(i * bl, bl)
        return pl.ds(pl.multiple_of(i * bl, bl), bl)

    def kv_tiles(j):
        kb = k_ref[rows(j), :].astype(F32) * (scale * LOG2E)
        vb = v_ref[rows(j), :]
        kp = [jnp.where(hmask[h], kb, (spare[h] == j).astype(F32)).astype(BF16) for h in range(2)]
        vp = [jnp.where(hmask[h], vb, 1.0).astype(BF16) for h in range(2)]
        return kp, vp

    dg = 2 if nb % 2 == 0 else 1

    def diag_body(ii, carry):
        chains = [(ii * dg + g, h) for g in range(dg) for h in range(2)]
        qf = {g: q_ref[rows(ii * dg + g), :].astype(F32) for g in range(dg)}
        kv = {g: kv_tiles(ii * dg + g) for g in range(dg)}
        qh = [jnp.where(hmask[h], qf[g], 0.0) for g in range(dg) for h in range(2)]
        s = [_dot(qh[c].astype(BF16), kv[c // 2][0][h], "nt") for c, (_, h) in enumerate(chains)]
        bs = [jnp.where(blk < i, _mm3(kmean_h[h], qf[c // 2], "nt"), MASK_VALUE)
              for c, (i, h) in enumerate(chains)]
        sel_t = []
        for c, (i, h) in enumerate(chains):
            rank = jnp.zeros((nbp, bl), jnp.int32)
            for jp in range(nb):
                row = bs[c][jp:jp + 1, :]
                beats = (row > bs[c]) | ((row == bs[c]) & (jp < blk))
                rank = rank + beats.astype(jnp.int32)
            sel_t.append(((rank < n_sel) & (blk < i)).astype(BF16))
        picked = [_dot(sel_t[c], place[h], "tn") for c, (_, h) in enumerate(chains)]
        for c, (i, h) in enumerate(chains):
            sm = jnp.where(causal, s[c], MASK_VALUE)
            m = jnp.max(sm, axis=1, keepdims=True)
            p = jnp.exp2(sm - m)
            m_ref[h, rows(i), :] = jnp.broadcast_to(m, (bl, LANE))
            acc_ref[h, rows(i), :] = _dot(p.astype(BF16), kv[c // 2][1][h])
            bias = jnp.where((spare[h] >= 0) & (spare[h] < nb) & (picked[c] < 0.5), MASK_VALUE, 0.0)
            qa_ref[h, rows(i), :] = (qh[c] + bias).astype(BF16)
        return carry

    lax.fori_loop(0, nb // dg, diag_body, 0)

    def q_tile(t, kp, vp):
        rs = pl.ds(t * qt, qt)
        s = [_dot(qa_ref[h, rs, :], kp[h], "nt") for h in range(2)]
        for h in range(2):
            m_prev = m_ref[h, rs, :]
            m_new = jnp.maximum(m_prev, jnp.max(s[h], axis=1, keepdims=True))
            p = jnp.exp2(s[h] - jnp.concatenate([m_new, m_new], axis=1))
            m_ref[h, rs, :] = m_new
            acc_ref[h, rs, :] = (acc_ref[h, rs, :] * jnp.exp2(m_prev - m_new)
                                 + _dot(p.astype(BF16), vp[h]))

    for t0 in range(nt):
        def key_body(j, carry, t0=t0):
            kp, vp = kv_tiles(j)
            for t in range(t0, nt):
                q_tile(t, kp, vp)
            return carry

        lax.fori_loop(max(0, t0 * qb - 1), min(nb - 1, (t0 + 1) * qb - 1), key_body, 0)

    def out_body(i, carry):
        a0 = acc_ref[0, rows(i), :]
        a1 = acc_ref[1, rows(i), :]
        o_ref[rows(i), :] = jnp.where(head0, a0 / pltpu.roll(a0, HEAD_DIM, 1), a1 / pltpu.roll(a1, HEAD_DIM, 1))
        return carry

    lax.fori_loop(0, nb, out_body, 0)


def _moba(proj, b, s):
    t = proj.shape[0]
    nb = s // MOBA_BLOCK
    n_sel = min(MOBA_TOPK, nb - 1)
    nhp = 512 // LANE
    qc, kc, vc = COL_MOBA // LANE, (COL_MOBA + 512) // LANE, (COL_MOBA + 1024) // LANE
    return pl.pallas_call(
        functools.partial(_moba_body, nb, n_sel),
        grid=(b, nhp),
        in_specs=[
            pl.BlockSpec((s, LANE), lambda bi, hp: (bi, qc + hp)),
            pl.BlockSpec((s, LANE), lambda bi, hp: (bi, kc + hp)),
            pl.BlockSpec((s, LANE), lambda bi, hp: (bi, vc + hp)),
        ],
        out_specs=pl.BlockSpec((s, LANE), lambda bi, hp: (bi, hp)),
        out_shape=jax.ShapeDtypeStruct((t, 512), F32),
        scratch_shapes=[pltpu.VMEM((2, s, LANE), F32), pltpu.VMEM((2, s, LANE), F32),
                        pltpu.VMEM((2, s, LANE), BF16)],
        compiler_params=_cparams(("parallel", "parallel")),
        name="moba_attn",
    )(proj, proj, proj)


def _mlstm_body(qk_ref, v_ref, o_ref, gcol_ref, grow_ref, cw_ref, cb_ref, bcol_ref, brow_ref,
                hng_ref, y_ref, cst_ref, n_ref, m_ref, prev_ref):
    c = CHUNK
    n = N_HEADS * c
    bt = qk_ref.shape[0]

    @pl.when(pl.program_id(1) == 0)
    def _():
        cst_ref[...] = jnp.zeros_like(cst_ref)
        n_ref[...] = jnp.zeros_like(n_ref)
        m_ref[...] = jnp.zeros_like(m_ref)
        prev_ref[...] = jnp.zeros_like(prev_ref)

    x = qk_ref[...].astype(F32)
    prev8 = prev_ref[...]
    cw = cw_ref[...]
    lag = lambda j: _per(lambda xx, p8: _lagged(xx, p8, j), x, prev8)
    conv = cb_ref[...] + cw[3:4] * x + cw[2:3] * lag(1) + cw[1:2] * lag(2) + cw[0:1] * lag(3)
    prev_ref[...] = x[:, c - SUBLANE:, :]
    qk = conv * jax.nn.sigmoid(conv)
    q = qk[:, :, 0:RW]
    k = qk[:, :, RW:2 * RW] * (HEAD_DIM ** -0.5)
    v = v_ref[...].astype(F32)

    same, rp, cp, _ = _head_masks(n)
    hm = same.astype(F32)
    incl = same & (rp >= cp)
    q_mt = _tile_heads(q) * hm
    k_t = _tile_heads(k)
    k_mt = k_t * hm
    v_mt = _tile_heads(v) * hm

    li_col = gcol_ref[:, 0] + bcol_ref[...]
    grow = grow_ref[:, 0] + brow_ref[...]
    li_row = grow[:, 0:1, :]
    lf_row = -_softplus(-grow[:, 1:2, :])
    lf_rb = jnp.broadcast_to(lf_row, (bt, SUBLANE, n))
    inclf = incl.astype(F32)
    b_col = _per(lambda z: _mm2l(inclf, z, "nt"), lf_rb)[:, :, 0:1]
    bend_col = _per(lambda z: _mm2l(hm, z, "nt"), lf_rb)[:, :, 0:1]
    b_row = _rows(_mm2r, lf_rb, (same & (rp <= cp)).astype(F32))[:, 0:1, :]
    bend_row = _rows(_mm2r, lf_rb, hm)[:, 0:1, :]

    m_col = m_ref[...]
    dmat = jnp.where(incl, b_col - b_row + li_row, -jnp.inf)
    inter = b_col + m_col
    m_t = jnp.maximum(inter, jnp.max(dmat, axis=2, keepdims=True))
    wts = jnp.exp(dmat - m_t)
    s_inter = jnp.exp(inter - m_t)
    cst = cst_ref[...]
    n_row = n_ref[...]
    qk_w = _bmm1(q_mt, k_t, "nt") * wts
    num = s_inter * _bmm1(q_mt, cst, "nt") + _bmm1(qk_w, v_mt)
    den = (s_inter * jnp.sum(q_mt * n_row, axis=2, keepdims=True)
           + jnp.sum(qk_w, axis=2, keepdims=True))
    hh = num / jnp.maximum(jnp.abs(den), jnp.exp(-m_t))
    y = hh[:, 0:c] + hh[:, c:2 * c] + hh[:, 2 * c:3 * c] + hh[:, 3 * c:4 * c]

    g_col = bend_col - b_col + li_col
    g_row = bend_row - b_row + li_row
    g_max = jnp.max(jnp.where(same, g_row, -jnp.inf), axis=2, keepdims=True)
    m_new = jnp.maximum(bend_col + m_col, g_max)
    w_col = jnp.exp(g_col - m_new)
    scale_col = jnp.exp(bend_col + m_col - m_new)
    scale_row = jnp.max(jnp.where(same, scale_col, 0.0), axis=1, keepdims=True)
    cst_ref[...] = scale_col * cst + _bmm1(v_mt * w_col, k_mt, "tn")
    n_ref[...] = scale_row * n_row + jnp.sum(k_mt * w_col, axis=1, keepdims=True)
    m_ref[...] = m_new

    mean = _rows(_mm2r, y, hm) * (1.0 / HEAD_DIM)
    d = y - mean
    var = _rows(_mm2r, d * d, hm) * (1.0 / HEAD_DIM)
    y_ref[...] = d * lax.rsqrt(var + NORM_EPS) * hng_ref[...] * jax.nn.sigmoid(o_ref[...].astype(F32))


def _mlstm(proj3, conv_w, conv_b, i_b, f_b, hn_g):
    b, s, _ = proj3.shape
    c = CHUNK
    nc = s // c
    n = N_HEADS * c
    bt = _batch_tile(b)
    gates = proj3[:, :, COL_MLSTM_G:COL_MLSTM_G + 2 * N_HEADS].astype(F32).reshape(b, nc, c, 2, N_HEADS)
    gcol = gates[:, :, :, 0, :].transpose(0, 1, 3, 2).reshape(b, nc, n, 1)
    grow = gates.transpose(0, 1, 3, 4, 2).reshape(b, nc, 2, n)
    bias = jnp.stack([jnp.repeat(i_b, c), jnp.repeat(f_b, c)], axis=0)
    row = lambda bi, ci: (bi, ci, 0)
    const = lambda bi, ci: (0, 0)
    return pl.pallas_call(
        _mlstm_body,
        grid=(b // bt, nc),
        in_specs=[
            pl.BlockSpec((bt, c, 2 * RW), lambda bi, ci: (bi, ci, COL_MLSTM_QK // (2 * RW))),
            pl.BlockSpec((bt, c, RW), lambda bi, ci: (bi, ci, COL_MLSTM_V // RW)),
            pl.BlockSpec((bt, c, RW), lambda bi, ci: (bi, ci, COL_MLSTM_O // RW)),
            pl.BlockSpec((bt, 1, n, 1), lambda bi, ci: (bi, ci, 0, 0)),
            pl.BlockSpec((bt, 1, 2, n), lambda bi, ci: (bi, ci, 0, 0)),
            pl.BlockSpec(conv_w.shape, const),
            pl.BlockSpec((1, 2 * RW), const),
            pl.BlockSpec((n, 1), const),
            pl.BlockSpec((2, n), const),
            pl.BlockSpec((1, RW), const),
        ],
        out_specs=pl.BlockSpec((bt, c, RW), row),
        out_shape=jax.ShapeDtypeStruct((b, s, RW), F32),
        scratch_shapes=[pltpu.VMEM((bt, n, n), F32), pltpu.VMEM((bt, 1, n), F32),
                        pltpu.VMEM((bt, n, 1), F32), pltpu.VMEM((bt, SUBLANE, 2 * RW), F32)],
        compiler_params=_cparams(("parallel", "arbitrary")),
        name="mlstm_chunk",
    )(proj3, proj3, proj3, gcol, grow, conv_w, conv_b.reshape(1, -1), bias[0].reshape(n, 1), bias,
      hn_g.reshape(1, -1))


def _merge_body(x_ref, ya_ref, yb_ref, yc_ref, ga_ref, gb_ref, gc_ref, wa_ref, wb_ref, wc_ref,
                wo_ref, g_ref, o_ref):
    merged = (jax.nn.sigmoid(ga_ref[...].astype(F32)) * _mm1(ya_ref[...], wa_ref[...])
              + jax.nn.sigmoid(gb_ref[...].astype(F32)) * _mm1(yb_ref[...], wb_ref[...])
              + jax.nn.sigmoid(gc_ref[...].astype(F32)) * _mm1(yc_ref[...], wc_ref[...]))
    o_ref[...] = x_ref[...] + _rms(_mm1(merged, wo_ref[...]), g_ref[...])


def _merge(x2d, proj, ya, yb, yc, wa, wb, wc, wo, g, tm):
    t, d = x2d.shape
    row = lambda i: (i, 0)
    const = lambda i: (0, 0)
    return pl.pallas_call(
        _merge_body,
        grid=(t // tm,),
        in_specs=[
            pl.BlockSpec((tm, d), row),
            pl.BlockSpec((tm, ya.shape[1]), row),
            pl.BlockSpec((tm, yb.shape[1]), row),
            pl.BlockSpec((tm, yc.shape[1]), row),
            pl.BlockSpec((tm, d), lambda i: (i, 0)),
            pl.BlockSpec((tm, d), lambda i: (i, 1)),
            pl.BlockSpec((tm, d), lambda i: (i, 2)),
            pl.BlockSpec(wa.shape, const),
            pl.BlockSpec(wb.shape, const),
            pl.BlockSpec(wc.shape, const),
            pl.BlockSpec(wo.shape, const),
            pl.BlockSpec((1, d), const),
        ],
        out_specs=pl.BlockSpec((tm, d), row),
        out_shape=jax.ShapeDtypeStruct((t, d), F32),
        compiler_params=_cparams(("parallel",)),
        name="merge_out",
    )(x2d, ya, yb, yc, proj, proj, proj, wa, wb, wc, wo, g.reshape(1, d))


FFN_ROW_GROUPS = 4


def _ffn_body(nt_seq, n_ff, x_ref, g1_ref, upg_ref, upv_ref, cwg_ref, cwv_ref, cbg_ref, cbv_ref,
              down_ref, g2_ref, g3_ref, pg_ref, pp_ref, p_ref, o_ref, h_ref, acc_ref, ugp_ref, uvp_ref):
    i = pl.program_id(0)
    j = pl.program_id(1)
    tm = x_ref.shape[0]

    @pl.when(j == 0)
    def _():
        h_ref[...] = _rms(x_ref[...], g1_ref[...]).astype(BF16)
        acc_ref[...] = jnp.zeros_like(acc_ref)

    @pl.when(i % nt_seq == 0)
    def _():
        ugp_ref[j] = jnp.zeros(ugp_ref.shape[1:], F32)
        uvp_ref[j] = jnp.zeros(uvp_ref.shape[1:], F32)

    ngrp = FFN_ROW_GROUPS if tm % (FFN_ROW_GROUPS * SUBLANE) == 0 else 1
    rg = tm // ngrp
    upg = upg_ref[...]
    upv = upv_ref[...]
    hs = [h_ref[r * rg:(r + 1) * rg, :] for r in range(ngrp)]
    ug = [_dot(hr, upg) for hr in hs]
    uv = [_dot(hr, upv) for hr in hs]
    pg = [ugp_ref[j]] + [ug[r][rg - SUBLANE:, :] for r in range(ngrp - 1)]
    pv = [uvp_ref[j]] + [uv[r][rg - SUBLANE:, :] for r in range(ngrp - 1)]
    ugp_ref[j] = ug[-1][rg - SUBLANE:, :]
    uvp_ref[j] = uv[-1][rg - SUBLANE:, :]
    cwg = cwg_ref[...]
    cwv = cwv_ref[...]
    down = down_ref[...]
    for r in range(ngrp):
        cg = (cbg_ref[...] + cwg[2:3] * ug[r] + cwg[1:2] * _lagged(ug[r], pg[r], 1)
              + cwg[0:1] * _lagged(ug[r], pg[r], 2))
        cv = (cbv_ref[...] + cwv[2:3] * uv[r] + cwv[1:2] * _lagged(uv[r], pv[r], 1)
              + cwv[0:1] * _lagged(uv[r], pv[r], 2))
        act = jax.nn.gelu(cg, approximate=True) * cv
        acc_ref[r * rg:(r + 1) * rg, :] += _mm1(act, down)

    @pl.when(j == n_ff - 1)
    def _():
        x2 = x_ref[...] + _rms(acc_ref[...], g2_ref[...])
        gate = jax.nn.sigmoid(_mm1(_rms(x2, g3_ref[...]), pg_ref[...]))
        o_ref[...] = x2 + gate * _mm1(p_ref[...], pp_ref[...])


def _ffn(x2d, p2d, s, g1, up, cw, cb, down, g2, g3, pgate, pproj, tm, tf):
    t, d = x2d.shape
    dff = down.shape[0]
    n_ff = dff // tf
    nt_seq = s // tm
    ple = p2d.shape[1]
    row = lambda i, j: (i, 0)
    const = lambda i, j: (0, 0)
    cb2 = cb.reshape(1, -1)
    return pl.pallas_call(
        functools.partial(_ffn_body, nt_seq, n_ff),
        grid=(t // tm, n_ff),
        in_specs=[
            pl.BlockSpec((tm, d), row),
            pl.BlockSpec((1, d), const),
            pl.BlockSpec((d, tf), lambda i, j: (0, j)),
            pl.BlockSpec((d, tf), lambda i, j: (0, n_ff + j)),
            pl.BlockSpec((cw.shape[0], tf), lambda i, j: (0, j)),
            pl.BlockSpec((cw.shape[0], tf), lambda i, j: (0, n_ff + j)),
            pl.BlockSpec((1, tf), lambda i, j: (0, j)),
            pl.BlockSpec((1, tf), lambda i, j: (0, n_ff + j)),
            pl.BlockSpec((tf, d), lambda i, j: (j, 0)),
            pl.BlockSpec((1, d), const),
            pl.BlockSpec((1, d), const),
            pl.BlockSpec(pgate.shape, const),
            pl.BlockSpec(pproj.shape, const),
            pl.BlockSpec((tm, ple), row),
        ],
        out_specs=pl.BlockSpec((tm, d), row),
        out_shape=jax.ShapeDtypeStruct((t, d), F32),
        scratch_shapes=[pltpu.VMEM((tm, d), BF16), pltpu.VMEM((tm, d), F32),
                        pltpu.VMEM((n_ff, SUBLANE, tf), F32), pltpu.VMEM((n_ff, SUBLANE, tf), F32)],
        compiler_params=_cparams(("arbitrary", "arbitrary")),
        name="ffn_ple",
    )(x2d, g1.reshape(1, d), up, up, cw, cw, cb2, cb2, down, g2.reshape(1, d), g3.reshape(1, d),
      pgate, pproj, p2d)


def _pack_w_in(w):
    d = w.shape[0]
    rwkv = w[:, 0:1024]
    moba = w[:, 1024:2560]
    ml = w[:, 2560:3592]
    gate = w[:, 3592:6664]
    pad = jnp.zeros((d, PACKED_WIDTH - COL_MLSTM_G - 2 * N_HEADS), w.dtype)
    packed = jnp.concatenate([gate, rwkv, moba, ml, pad], axis=1)
    return packed.astype(BF16)


def _row_tile(t, want):
    return want if t % want == 0 else t


def kernel(x, p, ln_mix_pre, ln_mix_post, ln_ffn_pre, ln_ffn_post, ln_ple, w_in, rwkv_mu, rwkv_w0, rwkv_w2, rwkv_a0, rwkv_a2, rwkv_g2, rwkv_k_k, rwkv_k_a, rwkv_r_k, rwkv_gn_g, rwkv_gn_b, rwkv_v0, rwkv_v1, rwkv_v2, mlstm_conv_w, mlstm_conv_b, mlstm_i_b, mlstm_f_b, mlstm_hn_g, w_br_rwkv, w_br_moba, w_br_mlstm, w_out, ffn_up, ffn_conv_w, ffn_conv_b, ffn_down, ple_proj, ple_gate):
    b, s, d = x.shape
    depth = w_in.shape[0]
    t = b * s
    assert d == 1024 and s % MOBA_BLOCK == 0 and w_in.shape[2] == 6664
    xf = x.reshape(t, d)
    tm_proj = _row_tile(t, 1024)
    tm = min(512, s)
    tm_ffn = 1024 if s % 1024 == 0 else tm
    r2 = lambda a: a.reshape(1, -1)
    v_first = None
    for i in range(depth):
        proj = _norm_proj(xf, ln_mix_pre[i], _pack_w_in(w_in[i]), tm_proj, 1024)
        params = [r2(rwkv_mu[i]), r2(rwkv_w0[i]), rwkv_w2[i], r2(rwkv_a0[i]), rwkv_a2[i], rwkv_g2[i],
                  r2(rwkv_k_k[i]), r2(rwkv_k_a[i]), r2(rwkv_r_k[i]), r2(rwkv_gn_g[i]), r2(rwkv_gn_b[i])]
        if i > 0:
            params += [r2(rwkv_v0[i - 1]), rwkv_v1[i - 1], rwkv_v2[i - 1]]
        proj3 = proj.reshape(b, s, -1)
        y_a, v_cur = _rwkv(proj3, params, v_first if i > 0 else None)
        if i == 0:
            v_first = v_cur
        y_b = _moba(proj, b, s)
        y_c = _mlstm(proj3, mlstm_conv_w[i], mlstm_conv_b[i], mlstm_i_b[i], mlstm_f_b[i], mlstm_hn_g[i])
        xf = _merge(xf, proj, y_a.reshape(t, -1), y_b, y_c.reshape(t, -1), w_br_rwkv[i].astype(BF16), w_br_moba[i].astype(BF16),
                    w_br_mlstm[i].astype(BF16), w_out[i].astype(BF16), ln_mix_post[i], tm)
        xf = _ffn(xf, p[i].reshape(t, -1), s, ln_ffn_pre[i], ffn_up[i].astype(BF16), ffn_conv_w[i],
                  ffn_conv_b[i], ffn_down[i].astype(BF16), ln_ffn_post[i], ln_ple[i],
                  ple_gate[i].astype(BF16), ple_proj[i].astype(BF16), tm_ffn, 256)
    return xf.reshape(b, s, d)
```

```python
import functools

import jax
import jax.numpy as jnp
from jax import lax
from jax.experimental import pallas as pl
from jax.experimental.pallas import tpu as pltpu

F32 = jnp.float32
BF16 = jnp.bfloat16

HEAD_DIM = 64
N_HEADS = 4
RW = N_HEADS * HEAD_DIM
BATCH_TILE = 4
CHUNK = 64
MOBA_BLOCK = 256
MOBA_TOPK = 3
MOBA_QTILE_BLOCKS = 4
NORM_EPS = 1e-6
RWKV_GN_EPS = 64e-5
MASK_VALUE = -1e30
LOG2E = 1.4426950408889634
LANE = 128
SUBLANE = 8
VMEM_LIMIT = 56 * 1024 * 1024

COL_GATE = 0
COL_RWKV = 3072
COL_MOBA = 4096
COL_MLSTM_QK = 5632
COL_MLSTM_V = 6144
COL_MLSTM_O = 6400
COL_MLSTM_G = 6656
PACKED_WIDTH = 7168

_DIMS = {
    "nn": (((1,), (0,)), ((), ())),
    "nt": (((1,), (1,)), ((), ())),
    "tn": (((0,), (0,)), ((), ())),
}


def _dot(a, b, dims="nn"):
    return lax.dot_general(a, b, _DIMS[dims], preferred_element_type=F32)


def _split(a):
    hi = a.astype(BF16)
    lo = (a - hi.astype(F32)).astype(BF16)
    return hi, lo


def _mm1(a, b, dims="nn"):
    return _dot(a.astype(BF16), b.astype(BF16), dims)


def _mm3(a, b, dims="nn"):
    ah, al = _split(a)
    bh, bl = _split(b)
    return _dot(ah, bh, dims) + (_dot(ah, bl, dims) + _dot(al, bh, dims))


def _mm2r(a, e, dims="nn"):
    ah, al = _split(a)
    eb = e.astype(BF16)
    return _dot(ah, eb, dims) + _dot(al, eb, dims)


def _mm2l(e, a, dims="nn"):
    ah, al = _split(a)
    eb = e.astype(BF16)
    return _dot(eb, ah, dims) + _dot(eb, al, dims)


def _softplus(x):
    return jnp.maximum(x, 0.0) + jnp.log(1.0 + jnp.exp(-jnp.abs(x)))


def _rms(x, g):
    ms = jnp.mean(x * x, axis=-1, keepdims=True)
    return x * lax.rsqrt(ms + NORM_EPS) * g


def _lagged(x, prev8, lag):
    full = jnp.concatenate([prev8, x], axis=0)
    return pltpu.roll(full, lag, 0)[SUBLANE:, :]


def _tile_heads(x):
    return jnp.concatenate([x] * N_HEADS, axis=-2)


def _per(fn, *arrs):
    return jnp.stack([fn(*(a[i] for a in arrs)) for i in range(arrs[0].shape[0])])


def _rows(fn, x, w):
    lead = x.shape[:-1]
    return fn(x.reshape(-1, x.shape[-1]), w).reshape(*lead, -1)


def _bmm1(a, b, dims="nn"):
    return _per(lambda x, y: _mm1(x, y, dims), a, b)


def _batch_tile(b):
    return BATCH_TILE if b % BATCH_TILE == 0 else 1


def _head_masks(n):
    ri = lax.broadcasted_iota(jnp.int32, (n, n), 0)
    ci = lax.broadcasted_iota(jnp.int32, (n, n), 1)
    same = (ri >> 6) == (ci >> 6)
    rp = ri & 63
    cp = ci & 63
    return same, rp, cp, ri == ci


def _cparams(sem):
    return pltpu.CompilerParams(dimension_semantics=sem, vmem_limit_bytes=VMEM_LIMIT)


def _proj_body(x_ref, g_ref, w_ref, o_ref, h_ref):
    @pl.when(pl.program_id(1) == 0)
    def _():
        h_ref[...] = _rms(x_ref[...], g_ref[...]).astype(BF16)

    o_ref[...] = _dot(h_ref[...], w_ref[...]).astype(o_ref.dtype)


def _norm_proj(x2d, g, w, tm, tn):
    t, d = x2d.shape
    n = w.shape[1]
    return pl.pallas_call(
        _proj_body,
        grid=(t // tm, n // tn),
        in_specs=[
            pl.BlockSpec((tm, d), lambda i, j: (i, 0)),
            pl.BlockSpec((1, d), lambda i, j: (0, 0)),
            pl.BlockSpec((d, tn), lambda i, j: (0, j)),
        ],
        out_specs=pl.BlockSpec((tm, tn), lambda i, j: (i, j)),
        out_shape=jax.ShapeDtypeStruct((t, n), BF16),
        scratch_shapes=[pltpu.VMEM((tm, d), BF16)],
        compiler_params=_cparams(("parallel", "arbitrary")),
        name="norm_proj",
    )(x2d, g.reshape(1, d), w)


def _rwkv_body(has_vres, *refs):
    if has_vres:
        (slab_ref, vf_ref, mu_ref, w0_ref, w2_ref, a0_ref, a2_ref, g2_ref, kk_ref, ka_ref,
         rk_ref, gg_ref, gb_ref, v0_ref, v1_ref, v2_ref, y_ref, vo_ref, st_ref, prev_ref) = refs
    else:
        (slab_ref, mu_ref, w0_ref, w2_ref, a0_ref, a2_ref, g2_ref, kk_ref, ka_ref,
         rk_ref, gg_ref, gb_ref, y_ref, vo_ref, st_ref, prev_ref) = refs
    c = CHUNK

    @pl.when(pl.program_id(1) == 0)
    def _():
        st_ref[...] = jnp.zeros_like(st_ref)
        prev_ref[...] = jnp.zeros_like(prev_ref)

    slab = slab_ref[...].astype(F32)
    shifted = _per(lambda x, p8: _lagged(x, p8, 1), slab, prev_ref[...])
    prev_ref[...] = slab[:, c - SUBLANE:, :]
    xs = slab + mu_ref[...] * (shifted - slab)
    r = xs[:, :, 0:RW]
    k = xs[:, :, RW:2 * RW]
    v = xs[:, :, 2 * RW:3 * RW]
    xw = xs[:, :, 768:832]
    xa = xs[:, :, 832:896]
    xg = xs[:, :, 896:1024]
    wlog = -_softplus(-(w0_ref[...] + _rows(_mm3, jnp.tanh(xw), w2_ref[...]))) - 0.5
    alr = jax.nn.sigmoid(a0_ref[...] + _rows(_mm3, xa, a2_ref[...]))
    g = _rows(_mm3, jax.nn.sigmoid(xg), g2_ref[...])
    if has_vres:
        mix = jax.nn.sigmoid(v0_ref[...] + _rows(_mm3, _rows(_mm3, v, v1_ref[...]), v2_ref[...]))
        v = v + (vf_ref[...] - v) * mix
    vo_ref[...] = v

    n = N_HEADS * c
    same, rp, cp, eye = _head_masks(n)
    hm = same.astype(F32)
    strict = same & (rp > cp)
    incl = same & (rp >= cp)

    kk = k * kk_ref[...]
    kk = kk / jnp.maximum(jnp.sqrt(_rows(_mm2r, kk * kk, hm)), 1e-12)
    k2 = k * (1.0 + (alr - 1.0) * ka_ref[...])

    lw = -jnp.exp(wlog)
    ti = lax.broadcasted_iota(jnp.int32, (c, c), 0)
    tj = lax.broadcasted_iota(jnp.int32, (c, c), 1)
    tril = (ti >= tj).astype(F32)
    cs = _per(lambda x: _mm2l(tril, x), lw)
    cs_end = cs[:, c - 1:c, :]
    p_in = jnp.exp(cs)
    p_ex = jnp.exp(cs - lw)
    p_inv = jnp.exp(-cs)
    p_tail = jnp.exp(cs_end - cs)

    a_mt = _tile_heads(-kk * p_ex) * hm
    r_mt = _tile_heads(r * p_in) * hm
    b_t = _tile_heads(kk * alr * p_inv)
    k_t = _tile_heads(k2 * p_inv)
    v_mt = _tile_heads(v) * hm
    b_tail = _tile_heads(kk * alr * p_tail) * hm
    k_tail = _tile_heads(k2 * p_tail) * hm

    l_ab = jnp.where(strict, _bmm1(a_mt, b_t, "nt"), 0.0)
    l_ak = jnp.where(strict, _bmm1(a_mt, k_t, "nt"), 0.0)
    m_rb = jnp.where(incl, _bmm1(r_mt, b_t, "nt"), 0.0)
    m_rk = jnp.where(incl, _bmm1(r_mt, k_t, "nt"), 0.0)

    tinv = eye.astype(F32) + l_ab
    npow = l_ab
    for _ in range(5):
        npow = _bmm1(npow, npow)
        tinv = tinv + _bmm1(tinv, npow)

    st = st_ref[...]
    u = _bmm1(tinv, _bmm1(a_mt, st, "nt") + _bmm1(l_ak, v_mt))
    yy = _bmm1(r_mt, st, "nt") + _bmm1(m_rb, u) + _bmm1(m_rk, v_mt)
    y = yy[:, 0:c] + yy[:, c:2 * c] + yy[:, 2 * c:3 * c] + yy[:, 3 * c:4 * c]
    st_ref[...] = st * p_in[:, c - 1:c, :] + _bmm1(u, b_tail, "tn") + _bmm1(v_mt, k_tail, "tn")

    mean = _rows(_mm2r, y, hm) * (1.0 / HEAD_DIM)
    d = y - mean
    var = _rows(_mm2r, d * d, hm) * (1.0 / HEAD_DIM)
    yn = d * lax.rsqrt(var + RWKV_GN_EPS) * gg_ref[...] + gb_ref[...]
    bonus = _rows(_mm2r, r * k2 * rk_ref[...], hm) * v
    y_ref[...] = (yn + bonus) * g


def _rwkv(proj3, params, v_first3):
    b, s, _ = proj3.shape
    nc = s // CHUNK
    bt = _batch_tile(b)
    has_vres = v_first3 is not None
    row = lambda bi, ci: (bi, ci, 0)
    const = lambda bi, ci: (0, 0)
    in_specs = [pl.BlockSpec((bt, CHUNK, 1024), lambda bi, ci: (bi, ci, COL_RWKV // 1024))]
    args = [proj3]
    if has_vres:
        in_specs.append(pl.BlockSpec((bt, CHUNK, RW), row))
        args.append(v_first3)
    for prm in params:
        in_specs.append(pl.BlockSpec(prm.shape, const))
        args.append(prm)
    n = N_HEADS * CHUNK
    return pl.pallas_call(
        functools.partial(_rwkv_body, has_vres),
        grid=(b // bt, nc),
        in_specs=in_specs,
        out_specs=[pl.BlockSpec((bt, CHUNK, RW), row), pl.BlockSpec((bt, CHUNK, RW), row)],
        out_shape=[jax.ShapeDtypeStruct((b, s, RW), F32), jax.ShapeDtypeStruct((b, s, RW), F32)],
        scratch_shapes=[pltpu.VMEM((bt, n, n), F32), pltpu.VMEM((bt, SUBLANE, 1024), F32)],
        compiler_params=_cparams(("parallel", "arbitrary")),
        name="rwkv7_chunk",
    )(*args)


def _moba_body(nb, n_sel, q_ref, k_ref, v_ref, o_ref, m_ref, acc_ref, qa_ref):
    bl = MOBA_BLOCK
    qb = MOBA_QTILE_BLOCKS if nb % MOBA_QTILE_BLOCKS == 0 else 1
    qt = qb * bl
    nt = nb // qb
    nbp =-(-nb // SUBLANE) * SUBLANE
    scale = HEAD_DIM ** -0.5
    lane = lax.broadcasted_iota(jnp.int32, (bl, LANE), 1)
    head0 = lane < HEAD_DIM
    hmask = (head0, jnp.logical_not(head0))
    spare = (lane - HEAD_DIM, lane)
    blk = lax.broadcasted_iota(jnp.int32, (nbp, bl), 0)
    qpos = lax.broadcasted_iota(jnp.int32, (bl, bl), 0)
    kpos = lax.broadcasted_iota(jnp.int32, (bl, bl), 1)
    causal = kpos <= qpos
    e_row = lax.broadcasted_iota(jnp.int32, (nbp, LANE), 0)
    e_lane = lax.broadcasted_iota(jnp.int32, (nbp, LANE), 1)
    place = ((e_lane == e_row + HEAD_DIM).astype(BF16), (e_lane == e_row).astype(BF16))
    klane = e_lane < HEAD_DIM

    kmean = jnp.mean(k_ref[...].astype(F32).reshape(nb, bl, LANE), axis=1)
    if nbp > nb:
        kmean = jnp.concatenate([kmean, jnp.zeros((nbp - nb, LANE), F32)], axis=0)
    kmean_h = (jnp.where(klane, kmean, 0.0), jnp.where(klane, 0.0, kmean))

    def rows(i):
        if isinstance(i, int):
            return pl.ds(i * bl, bl)
        return pl.ds(pl.multiple_of(i * bl, bl), bl)

    def kv_tiles(j):
        kb = k_ref[rows(j), :].astype(F32) * (scale * LOG2E)
        vb = v_ref[rows(j), :]
        kp = [jnp.where(hmask[h], kb, (spare[h] == j).astype(F32)).astype(BF16) for h in range(2)]
        vp = [jnp.where(hmask[h], vb, 1.0).astype(BF16) for h in range(2)]
        return kp, vp

    dg = 2 if nb % 2 == 0 else 1

    def diag_body(ii, carry):
        chains = [(ii * dg + g, h) for g in range(dg) for h in range(2)]
        qf = {g: q_ref[rows(ii * dg + g), :].astype(F32) for g in range(dg)}
        kv = {g: kv_tiles(ii * dg + g) for g in range(dg)}
        qh = [jnp.where(hmask[h], qf[g], 0.0) for g in range(dg) for h in range(2)]
        s = [_dot(qh[c].astype(BF16), kv[c // 2][0][h], "nt") for c, (_, h) in enumerate(chains)]
        bs = [jnp.where(blk < i, _mm3(kmean_h[h], qf[c // 2], "nt"), MASK_VALUE)
              for c, (i, h) in enumerate(chains)]
        sel_t = []
        for c, (i, h) in enumerate(chains):
            rank = jnp.zeros((nbp, bl), jnp.int32)
            for jp in range(nb):
                row = bs[c][jp:jp + 1, :]
                beats = (row > bs[c]) | ((row == bs[c]) & (jp < blk))
                rank = rank + beats.astype(jnp.int32)
            sel_t.append(((rank < n_sel) & (blk < i)).astype(BF16))
        picked = [_dot(sel_t[c], place[h], "tn") for c, (_, h) in enumerate(chains)]
        for c, (i, h) in enumerate(chains):
            sm = jnp.where(causal, s[c], MASK_VALUE)
            m = jnp.max(sm, axis=1, keepdims=True)
            p = jnp.exp2(sm - m)
            m_ref[h, rows(i), :] = jnp.broadcast_to(m, (bl, LANE))
            acc_ref[h, rows(i), :] = _dot(p.astype(BF16), kv[c // 2][1][h])
            bias = jnp.where((spare[h] >= 0) & (spare[h] < nb) & (picked[c] < 0.5), MASK_VALUE, 0.0)
            qa_ref[h, rows(i), :] = (qh[c] + bias).astype(BF16)
        return carry

    lax.fori_loop(0, nb // dg, diag_body, 0)

    def q_tile(t, kp, vp):
        rs = pl.ds(t * qt, qt)
        s = [_dot(qa_ref[h, rs, :], kp[h], "nt") for h in range(2)]
        for h in range(2):
            m_prev = m_ref[h, rs, :]
            m_new = jnp.maximum(m_prev, jnp.max(s[h], axis=1, keepdims=True))
            p = jnp.exp2(s[h] - jnp.concatenate([m_new, m_new], axis=1))
            m_ref[h, rs, :] = m_new
            acc_ref[h, rs, :] = (acc_ref[h, rs, :] * jnp.exp2(m_prev - m_new)
                                 + _dot(p.astype(BF16), vp[h]))

    for t0 in range(nt):
        def key_body(j, carry, t0=t0):
            kp, vp = kv_tiles(j)
            for t in range(t0, nt):
                q_tile(t, kp, vp)
            return carry

        lax.fori_loop(max(0, t0 * qb - 1), min(nb - 1, (t0 + 1) * qb - 1), key_body, 0)

    def out_body(i, carry):
        a0 = acc_ref[0, rows(i), :]
        a1 = acc_ref[1, rows(i), :]
        o_ref[rows(i), :] = jnp.where(head0, a0 / pltpu.roll(a0, HEAD_DIM, 1), a1 / pltpu.roll(a1, HEAD_DIM, 1))
        return carry

    lax.fori_loop(0, nb, out_body, 0)


def _moba(proj, b, s):
    t = proj.shape[0]
    nb = s // MOBA_BLOCK
    n_sel = min(MOBA_TOPK, nb - 1)
    nhp = 512 // LANE
    qc, kc, vc = COL_MOBA // LANE, (COL_MOBA + 512) // LANE, (COL_MOBA + 1024) // LANE
    return pl.pallas_call(
        functools.partial(_moba_body, nb, n_sel),
        grid=(b, nhp),
        in_specs=[
            pl.BlockSpec((s, LANE), lambda bi, hp: (bi, qc + hp)),
            pl.BlockSpec((s, LANE), lambda bi, hp: (bi, kc + hp)),
            pl.BlockSpec((s, LANE), lambda bi, hp: (bi, vc + hp)),
        ],
        out_specs=pl.BlockSpec((s, LANE), lambda bi, hp: (bi, hp)),
        out_shape=jax.ShapeDtypeStruct((t, 512), F32),
        scratch_shapes=[pltpu.VMEM((2, s, LANE), F32), pltpu.VMEM((2, s, LANE), F32),
                        pltpu.VMEM((2, s, LANE), BF16)],
        compiler_params=_cparams(("parallel", "parallel")),
        name="moba_attn",
    )(proj, proj, proj)


def _mlstm_body(qk_ref, v_ref, o_ref, gate_ref, cw_ref, cb_ref, brow_ref, hng_ref,
                y_ref, cst_ref, n_ref, m_ref, prev_ref):
    c = CHUNK
    n = N_HEADS * c
    bt = qk_ref.shape[0]

    @pl.when(pl.program_id(1) == 0)
    def _():
        cst_ref[...] = jnp.zeros_like(cst_ref)
        n_ref[...] = jnp.zeros_like(n_ref)
        m_ref[...] = jnp.zeros_like(m_ref)
        prev_ref[...] = jnp.zeros_like(prev_ref)

    x = qk_ref[...].astype(F32)
    prev8 = prev_ref[...]
    cw = cw_ref[...]
    lag = lambda j: _per(lambda xx, p8: _lagged(xx, p8, j), x, prev8)
    conv = cb_ref[...] + cw[3:4] * x + cw[2:3] * lag(1) + cw[1:2] * lag(2) + cw[0:1] * lag(3)
    prev_ref[...] = x[:, c - SUBLANE:, :]
    qk = conv * jax.nn.sigmoid(conv)
    q = qk[:, :, 0:RW]
    k = qk[:, :, RW:2 * RW] * (HEAD_DIM ** -0.5)
    v = v_ref[...].astype(F32)

    same, rp, cp, eye = _head_masks(n)
    hm = same.astype(F32)
    incl = same & (rp >= cp)
    q_mt = _tile_heads(q) * hm
    k_t = _tile_heads(k)
    k_mt = k_t * hm
    v_mt = _tile_heads(v) * hm

    pos = lax.broadcasted_iota(jnp.int32, (c, n), 0)
    e_tile = (pos == (lax.broadcasted_iota(jnp.int32, (c, n), 1) & 63)).astype(F32)
    g8 = _per(lambda z: _mm2r(z, e_tile, "tn")[0:SUBLANE], gate_ref[...].astype(F32))
    sub = lax.broadcasted_iota(jnp.int32, (SUBLANE, n), 0)
    lane_head = lax.broadcasted_iota(jnp.int32, (SUBLANE, n), 1) >> 6
    brow = brow_ref[...]
    li_row = jnp.sum(jnp.where(sub == lane_head, g8, 0.0), axis=1, keepdims=True) + brow[0:1]
    lf_row = -_softplus(-(jnp.sum(jnp.where(sub == lane_head + N_HEADS, g8, 0.0), axis=1, keepdims=True)
                          + brow[1:2]))
    lf_rb = jnp.broadcast_to(lf_row, (bt, SUBLANE, n))
    to_col = jnp.concatenate([incl.astype(F32), hm, eye.astype(F32)], axis=0)
    r8 = jnp.where(sub == 0, lf_rb, jnp.where(sub == 1, li_row, 0.0))
    cols = _per(lambda z: _mm2l(to_col, z, "nt"), r8)
    b_col = cols[:, 0:n, 0:1]
    bend_col = cols[:, n:2 * n, 0:1]
    li_col = cols[:, 2 * n:3 * n, 1:2]
    b_row = _rows(_mm2r, lf_rb, (same & (rp <= cp)).astype(F32))[:, 0:1, :]
    bend_row = _rows(_mm2r, lf_rb, hm)[:, 0:1, :]

    m_col = m_ref[...]
    dmat = jnp.where(incl, b_col - b_row + li_row, -jnp.inf)
    inter = b_col + m_col
    m_t = jnp.maximum(inter, jnp.max(dmat, axis=2, keepdims=True))
    wts = jnp.exp(dmat - m_t)
    s_inter = jnp.exp(inter - m_t)
    cst = cst_ref[...]
    n_row = n_ref[...]
    qk_w = _bmm1(q_mt, k_t, "nt") * wts
    num = s_inter * _bmm1(q_mt, cst, "nt") + _bmm1(qk_w, v_mt)
    den = (s_inter * jnp.sum(q_mt * n_row, axis=2, keepdims=True)
           + jnp.sum(qk_w, axis=2, keepdims=True))
    hh = num / jnp.maximum(jnp.abs(den), jnp.exp(-m_t))
    y = hh[:, 0:c] + hh[:, c:2 * c] + hh[:, 2 * c:3 * c] + hh[:, 3 * c:4 * c]

    g_col = bend_col - b_col + li_col
    g_row = bend_row - b_row + li_row
    g_max = jnp.max(jnp.where(same, g_row, -jnp.inf), axis=2, keepdims=True)
    m_new = jnp.maximum(bend_col + m_col, g_max)
    w_col = jnp.exp(g_col - m_new)
    scale_col = jnp.exp(bend_col + m_col - m_new)
    scale_row = jnp.max(jnp.where(same, scale_col, 0.0), axis=1, keepdims=True)
    cst_ref[...] = scale_col * cst + _bmm1(v_mt * w_col, k_mt, "tn")
    n_ref[...] = scale_row * n_row + jnp.sum(k_mt * w_col, axis=1, keepdims=True)
    m_ref[...] = m_new

    mean = _rows(_mm2r, y, hm) * (1.0 / HEAD_DIM)
    d = y - mean
    var = _rows(_mm2r, d * d, hm) * (1.0 / HEAD_DIM)
    y_ref[...] = d * lax.rsqrt(var + NORM_EPS) * hng_ref[...] * jax.nn.sigmoid(o_ref[...].astype(F32))


def _mlstm(proj3, conv_w, conv_b, i_b, f_b, hn_g):
    b, s, _ = proj3.shape
    c = CHUNK
    nc = s // c
    n = N_HEADS * c
    bt = _batch_tile(b)
    bias = jnp.stack([jnp.repeat(i_b, c), jnp.repeat(f_b, c)], axis=0)
    row = lambda bi, ci: (bi, ci, 0)
    const = lambda bi, ci: (0, 0)
    return pl.pallas_call(
        _mlstm_body,
        grid=(b // bt, nc),
        in_specs=[
            pl.BlockSpec((bt, c, 2 * RW), lambda bi, ci: (bi, ci, COL_MLSTM_QK // (2 * RW))),
            pl.BlockSpec((bt, c, RW), lambda bi, ci: (bi, ci, COL_MLSTM_V // RW)),
            pl.BlockSpec((bt, c, RW), lambda bi, ci: (bi, ci, COL_MLSTM_O // RW)),
            pl.BlockSpec((bt, c, LANE), lambda bi, ci: (bi, ci, COL_MLSTM_G // LANE)),
            pl.BlockSpec(conv_w.shape, const),
            pl.BlockSpec((1, 2 * RW), const),
            pl.BlockSpec((2, n), const),
            pl.BlockSpec((1, RW), const),
        ],
        out_specs=pl.BlockSpec((bt, c, RW), row),
        out_shape=jax.ShapeDtypeStruct((b, s, RW), F32),
        scratch_shapes=[pltpu.VMEM((bt, n, n), F32), pltpu.VMEM((bt, 1, n), F32),
                        pltpu.VMEM((bt, n, 1), F32), pltpu.VMEM((bt, SUBLANE, 2 * RW), F32)],
        compiler_params=_cparams(("parallel", "arbitrary")),
        name="mlstm_chunk",
    )(proj3, proj3, proj3, proj3, conv_w, conv_b.reshape(1, -1), bias,
      hn_g.reshape(1, -1))


def _merge_body(x_ref, ya_ref, yb_ref, yc_ref, ga_ref, gb_ref, gc_ref, wa_ref, wb_ref, wc_ref,
                wo_ref, g_ref, o_ref):
    merged = (jax.nn.sigmoid(ga_ref[...].astype(F32)) * _mm1(ya_ref[...], wa_ref[...])
              + jax.nn.sigmoid(gb_ref[...].astype(F32)) * _mm1(yb_ref[...], wb_ref[...])
              + jax.nn.sigmoid(gc_ref[...].astype(F32)) * _mm1(yc_ref[...], wc_ref[...]))
    o_ref[...] = x_ref[...] + _rms(_mm1(merged, wo_ref[...]), g_ref[...])


def _merge(x2d, proj, ya, yb, yc, wa, wb, wc, wo, g, tm):
    t, d = x2d.shape
    row = lambda i: (i, 0)
    const = lambda i: (0, 0)
    return pl.pallas_call(
        _merge_body,
        grid=(t // tm,),
        in_specs=[
            pl.BlockSpec((tm, d), row),
            pl.BlockSpec((tm, ya.shape[1]), row),
            pl.BlockSpec((tm, yb.shape[1]), row),
            pl.BlockSpec((tm, yc.shape[1]), row),
            pl.BlockSpec((tm, d), lambda i: (i, 0)),
            pl.BlockSpec((tm, d), lambda i: (i, 1)),
            pl.BlockSpec((tm, d), lambda i: (i, 2)),
            pl.BlockSpec(wa.shape, const),
            pl.BlockSpec(wb.shape, const),
            pl.BlockSpec(wc.shape, const),
            pl.BlockSpec(wo.shape, const),
            pl.BlockSpec((1, d), const),
        ],
        out_specs=pl.BlockSpec((tm, d), row),
        out_shape=jax.ShapeDtypeStruct((t, d), F32),
        compiler_params=_cparams(("parallel",)),
        name="merge_out",
    )(x2d, ya, yb, yc, proj, proj, proj, wa, wb, wc, wo, g.reshape(1, d))


FFN_ROW_GROUPS = 2


def _ffn_body(nt_seq, n_ff, x_ref, g1_ref, upg_ref, upv_ref, cwg_ref, cwv_ref, cbg_ref, cbv_ref,
              down_ref, g2_ref, g3_ref, pg_ref, pp_ref, p_ref, o_ref, h_ref, acc_ref, ugp_ref, uvp_ref):
    i = pl.program_id(0)
    j = pl.program_id(1)
    tm = x_ref.shape[0]

    @pl.when(j == 0)
    def _():
        h_ref[...] = _rms(x_ref[...], g1_ref[...]).astype(BF16)
        acc_ref[...] = jnp.zeros_like(acc_ref)

    @pl.when(i % nt_seq == 0)
    def _():
        ugp_ref[j] = jnp.zeros(ugp_ref.shape[1:], F32)
        uvp_ref[j] = jnp.zeros(uvp_ref.shape[1:], F32)

    ngrp = FFN_ROW_GROUPS if tm % (FFN_ROW_GROUPS * SUBLANE) == 0 else 1
    rg = tm // ngrp
    upg = upg_ref[...]
    upv = upv_ref[...]
    hs = [h_ref[r * rg:(r + 1) * rg, :] for r in range(ngrp)]
    ug = [_dot(hr, upg) for hr in hs]
    uv = [_dot(hr, upv) for hr in hs]
    pg = [ugp_ref[j]] + [ug[r][rg - SUBLANE:, :] for r in range(ngrp - 1)]
    pv = [uvp_ref[j]] + [uv[r][rg - SUBLANE:, :] for r in range(ngrp - 1)]
    ugp_ref[j] = ug[-1][rg - SUBLANE:, :]
    uvp_ref[j] = uv[-1][rg - SUBLANE:, :]
    cwg = cwg_ref[...]
    cwv = cwv_ref[...]
    down = down_ref[...]
    for r in range(ngrp):
        cg = (cbg_ref[...] + cwg[2:3] * ug[r] + cwg[1:2] * _lagged(ug[r], pg[r], 1)
              + cwg[0:1] * _lagged(ug[r], pg[r], 2))
        cv = (cbv_ref[...] + cwv[2:3] * uv[r] + cwv[1:2] * _lagged(uv[r], pv[r], 1)
              + cwv[0:1] * _lagged(uv[r], pv[r], 2))
        act = jax.nn.gelu(cg, approximate=True) * cv
        acc_ref[r * rg:(r + 1) * rg, :] += _mm1(act, down)

    @pl.when(j == n_ff - 1)
    def _():
        x2 = x_ref[...] + _rms(acc_ref[...], g2_ref[...])
        gate = jax.nn.sigmoid(_mm1(_rms(x2, g3_ref[...]), pg_ref[...]))
        o_ref[...] = x2 + gate * _mm1(p_ref[...], pp_ref[...])


def _ffn(x2d, p2d, s, g1, up, cw, cb, down, g2, g3, pgate, pproj, tm, tf):
    t, d = x2d.shape
    dff = down.shape[0]
    n_ff = dff // tf
    nt_seq = s // tm
    ple = p2d.shape[1]
    row = lambda i, j: (i, 0)
    const = lambda i, j: (0, 0)
    cb2 = cb.reshape(1, -1)
    return pl.pallas_call(
        functools.partial(_ffn_body, nt_seq, n_ff),
        grid=(t // tm, n_ff),
        in_specs=[
            pl.BlockSpec((tm, d), row),
            pl.BlockSpec((1, d), const),
            pl.BlockSpec((d, tf), lambda i, j: (0, j)),
            pl.BlockSpec((d, tf), lambda i, j: (0, n_ff + j)),
            pl.BlockSpec((cw.shape[0], tf), lambda i, j: (0, j)),
            pl.BlockSpec((cw.shape[0], tf), lambda i, j: (0, n_ff + j)),
            pl.BlockSpec((1, tf), lambda i, j: (0, j)),
            pl.BlockSpec((1, tf), lambda i, j: (0, n_ff + j)),
            pl.BlockSpec((tf, d), lambda i, j: (j, 0)),
            pl.BlockSpec((1, d), const),
            pl.BlockSpec((1, d), const),
            pl.BlockSpec(pgate.shape, const),
            pl.BlockSpec(pproj.shape, const),
            pl.BlockSpec((tm, ple), row),
        ],
        out_specs=pl.BlockSpec((tm, d), row),
        out_shape=jax.ShapeDtypeStruct((t, d), F32),
        scratch_shapes=[pltpu.VMEM((tm, d), BF16), pltpu.VMEM((tm, d), F32),
                        pltpu.VMEM((n_ff, SUBLANE, tf), F32), pltpu.VMEM((n_ff, SUBLANE, tf), F32)],
        compiler_params=_cparams(("arbitrary", "arbitrary")),
        name="ffn_ple",
    )(x2d, g1.reshape(1, d), up, up, cw, cw, cb2, cb2, down, g2.reshape(1, d), g3.reshape(1, d),
      pgate, pproj, p2d)


def _pack_w_in(w):
    d = w.shape[0]
    rwkv = w[:, 0:1024]
    moba = w[:, 1024:2560]
    ml = w[:, 2560:3592]
    gate = w[:, 3592:6664]
    pad = jnp.zeros((d, PACKED_WIDTH - COL_MLSTM_G - 2 * N_HEADS), w.dtype)
    packed = jnp.concatenate([gate, rwkv, moba, ml, pad], axis=1)
    return packed.astype(BF16)


def _row_tile(t, want):
    return want if t % want == 0 else t


def kernel(x, p, ln_mix_pre, ln_mix_post, ln_ffn_pre, ln_ffn_post, ln_ple, w_in, rwkv_mu, rwkv_w0, rwkv_w2, rwkv_a0, rwkv_a2, rwkv_g2, rwkv_k_k, rwkv_k_a, rwkv_r_k, rwkv_gn_g, rwkv_gn_b, rwkv_v0, rwkv_v1, rwkv_v2, mlstm_conv_w, mlstm_conv_b, mlstm_i_b, mlstm_f_b, mlstm_hn_g, w_br_rwkv, w_br_moba, w_br_mlstm, w_out, ffn_up, ffn_conv_w, ffn_conv_b, ffn_down, ple_proj, ple_gate):
    b, s, d = x.shape
    depth = w_in.shape[0]
    t = b * s
    assert d == 1024 and s % MOBA_BLOCK == 0 and w_in.shape[2] == 6664
    xf = x.reshape(t, d)
    tm_proj = _row_tile(t, 2048)
    tm = min(512, s)
    tm_ffn = tm
    r2 = lambda a: a.reshape(1, -1)
    v_first = None
    for i in range(depth):
        proj = _norm_proj(xf, ln_mix_pre[i], _pack_w_in(w_in[i]), tm_proj, 1024)
        params = [r2(rwkv_mu[i]), r2(rwkv_w0[i]), rwkv_w2[i], r2(rwkv_a0[i]), rwkv_a2[i], rwkv_g2[i],
                  r2(rwkv_k_k[i]), r2(rwkv_k_a[i]), r2(rwkv_r_k[i]), r2(rwkv_gn_g[i]), r2(rwkv_gn_b[i])]
        if i > 0:
            params += [r2(rwkv_v0[i - 1]), rwkv_v1[i - 1], rwkv_v2[i - 1]]
        proj3 = proj.reshape(b, s, -1)
        y_a, v_cur = _rwkv(proj3, params, v_first if i > 0 else None)
        if i == 0:
            v_first = v_cur
        y_b = _moba(proj, b, s)
        y_c = _mlstm(proj3, mlstm_conv_w[i], mlstm_conv_b[i], mlstm_i_b[i], mlstm_f_b[i], mlstm_hn_g[i])
        xf = _merge(xf, proj, y_a.reshape(t, -1), y_b, y_c.reshape(t, -1), w_br_rwkv[i].astype(BF16), w_br_moba[i].astype(BF16),
                    w_br_mlstm[i].astype(BF16), w_out[i].astype(BF16), ln_mix_post[i], tm)
        xf = _ffn(xf, p[i].reshape(t, -1), s, ln_ffn_pre[i], ffn_up[i].astype(BF16), ffn_conv_w[i],
                  ffn_conv_b[i], ffn_down[i].astype(BF16), ln_ffn_post[i], ln_ple[i],
                  ple_gate[i].astype(BF16), ple_proj[i].astype(BF16), tm_ffn, 1408)
    return xf.reshape(b, s, d)
```

```python
import functools

import jax
import jax.numpy as jnp
from jax import lax
from jax.experimental import pallas as pl
from jax.experimental.pallas import tpu as pltpu

F32 = jnp.float32
BF16 = jnp.bfloat16

HEAD_DIM = 64
N_HEADS = 4
RW = N_HEADS * HEAD_DIM
BATCH_TILE = 8
CHUNK = 64
MOBA_BLOCK = 256
MOBA_TOPK = 3
MOBA_QTILE_BLOCKS = 4
NORM_EPS = 1e-6
RWKV_GN_EPS = 64e-5
MASK_VALUE = -1e30
LOG2E = 1.4426950408889634
LANE = 128
SUBLANE = 8
VMEM_LIMIT = 56 * 1024 * 1024

COL_GATE = 0
COL_RWKV = 3072
COL_MOBA = 4096
COL_MLSTM_QK = 5632
COL_MLSTM_V = 6144
COL_MLSTM_O = 6400
COL_MLSTM_G = 6656
PACKED_WIDTH = 7168

_DIMS = {
    "nn": (((1,), (0,)), ((), ())),
    "nt": (((1,), (1,)), ((), ())),
    "tn": (((0,), (0,)), ((), ())),
}


def _dot(a, b, dims="nn"):
    return lax.dot_general(a, b, _DIMS[dims], preferred_element_type=F32)


def _split(a):
    hi = a.astype(BF16)
    lo = (a - hi.astype(F32)).astype(BF16)
    return hi, lo


def _mm1(a, b, dims="nn"):
    return _dot(a.astype(BF16), b.astype(BF16), dims)


def _mm3(a, b, dims="nn"):
    ah, al = _split(a)
    bh, bl = _split(b)
    return _dot(ah, bh, dims) + (_dot(ah, bl, dims) + _dot(al, bh, dims))


def _mm2r(a, e, dims="nn"):
    ah, al = _split(a)
    eb = e.astype(BF16)
    return _dot(ah, eb, dims) + _dot(al, eb, dims)


def _mm2l(e, a, dims="nn"):
    ah, al = _split(a)
    eb = e.astype(BF16)
    return _dot(eb, ah, dims) + _dot(eb, al, dims)


def _softplus(x):
    return jnp.maximum(x, 0.0) + jnp.log(1.0 + jnp.exp(-jnp.abs(x)))


def _rms(x, g):
    ms = jnp.mean(x * x, axis=-1, keepdims=True)
    return x * lax.rsqrt(ms + NORM_EPS) * g


def _lagged(x, prev8, lag):
    full = jnp.concatenate([prev8, x], axis=0)
    return pltpu.roll(full, lag, 0)[SUBLANE:, :]


def _tile_heads(x):
    return jnp.concatenate([x] * N_HEADS, axis=-2)


def _per(fn, *arrs):
    return jnp.stack([fn(*(a[i] for a in arrs)) for i in range(arrs[0].shape[0])])


def _rows(fn, x, w):
    lead = x.shape[:-1]
    return fn(x.reshape(-1, x.shape[-1]), w).reshape(*lead, -1)


def _bmm1(a, b, dims="nn"):
    return _per(lambda x, y: _mm1(x, y, dims), a, b)


def _batch_tile(b):
    return BATCH_TILE if b % BATCH_TILE == 0 else 1


def _head_masks(n):
    ri = lax.broadcasted_iota(jnp.int32, (n, n), 0)
    ci = lax.broadcasted_iota(jnp.int32, (n, n), 1)
    same = (ri >> 6) == (ci >> 6)
    rp = ri & 63
    cp = ci & 63
    return same, rp, cp, ri == ci


def _cparams(sem):
    return pltpu.CompilerParams(dimension_semantics=sem, vmem_limit_bytes=VMEM_LIMIT)


def _proj_body(x_ref, g_ref, w_ref, o_ref, h_ref):
    @pl.when(pl.program_id(1) == 0)
    def _():
        h_ref[...] = _rms(x_ref[...], g_ref[...]).astype(BF16)

    o_ref[...] = _dot(h_ref[...], w_ref[...]).astype(o_ref.dtype)


def _norm_proj(x2d, g, w, tm, tn):
    t, d = x2d.shape
    n = w.shape[1]
    return pl.pallas_call(
        _proj_body,
        grid=(t // tm, n // tn),
        in_specs=[
            pl.BlockSpec((tm, d), lambda i, j: (i, 0)),
            pl.BlockSpec((1, d), lambda i, j: (0, 0)),
            pl.BlockSpec((d, tn), lambda i, j: (0, j)),
        ],
        out_specs=pl.BlockSpec((tm, tn), lambda i, j: (i, j)),
        out_shape=jax.ShapeDtypeStruct((t, n), BF16),
        scratch_shapes=[pltpu.VMEM((tm, d), BF16)],
        compiler_params=_cparams(("parallel", "arbitrary")),
        name="norm_proj",
    )(x2d, g.reshape(1, d), w)


def _rwkv_body(has_vres, *refs):
    if has_vres:
        (slab_ref, vf_ref, mu_ref, w0_ref, w2_ref, a0_ref, a2_ref, g2_ref, kk_ref, ka_ref,
         rk_ref, gg_ref, gb_ref, v0_ref, v1_ref, v2_ref, y_ref, vo_ref, st_ref, prev_ref) = refs
    else:
        (slab_ref, mu_ref, w0_ref, w2_ref, a0_ref, a2_ref, g2_ref, kk_ref, ka_ref,
         rk_ref, gg_ref, gb_ref, y_ref, vo_ref, st_ref, prev_ref) = refs
    c = CHUNK

    @pl.when(pl.program_id(1) == 0)
    def _():
        st_ref[...] = jnp.zeros_like(st_ref)
        prev_ref[...] = jnp.zeros_like(prev_ref)

    slab = slab_ref[...].astype(F32)
    shifted = _per(lambda x, p8: _lagged(x, p8, 1), slab, prev_ref[...])
    prev_ref[...] = slab[:, c - SUBLANE:, :]
    xs = slab + mu_ref[...] * (shifted - slab)
    r = xs[:, :, 0:RW]
    k = xs[:, :, RW:2 * RW]
    v = xs[:, :, 2 * RW:3 * RW]
    xw = xs[:, :, 768:832]
    xa = xs[:, :, 832:896]
    xg = xs[:, :, 896:1024]
    wlog = -_softplus(-(w0_ref[...] + _rows(_mm3, jnp.tanh(xw), w2_ref[...]))) - 0.5
    alr = jax.nn.sigmoid(a0_ref[...] + _rows(_mm3, xa, a2_ref[...]))
    g = _rows(_mm3, jax.nn.sigmoid(xg), g2_ref[...])
    if has_vres:
        mix = jax.nn.sigmoid(v0_ref[...] + _rows(_mm3, _rows(_mm3, v, v1_ref[...]), v2_ref[...]))
        v = v + (vf_ref[...] - v) * mix
    vo_ref[...] = v

    n = N_HEADS * c
    same, rp, cp, eye = _head_masks(n)
    hm = same.astype(F32)
    strict = same & (rp > cp)
    incl = same & (rp >= cp)

    kk = k * kk_ref[...]
    kk = kk / jnp.maximum(jnp.sqrt(_rows(_mm2r, kk * kk, hm)), 1e-12)
    k2 = k * (1.0 + (alr - 1.0) * ka_ref[...])

    lw = -jnp.exp(wlog)
    ti = lax.broadcasted_iota(jnp.int32, (c, c), 0)
    tj = lax.broadcasted_iota(jnp.int32, (c, c), 1)
    tril = (ti >= tj).astype(F32)
    cs = _per(lambda x: _mm2l(tril, x), lw)
    cs_end = cs[:, c - 1:c, :]
    p_in = jnp.exp(cs)
    p_ex = jnp.exp(cs - lw)
    p_inv = jnp.exp(-cs)
    p_tail = jnp.exp(cs_end - cs)

    a_mt = _tile_heads(-kk * p_ex) * hm
    r_mt = _tile_heads(r * p_in) * hm
    b_t = _tile_heads(kk * alr * p_inv)
    k_t = _tile_heads(k2 * p_inv)
    v_mt = _tile_heads(v) * hm
    b_tail = _tile_heads(kk * alr * p_tail) * hm
    k_tail = _tile_heads(k2 * p_tail) * hm

    l_ab = jnp.where(strict, _bmm1(a_mt, b_t, "nt"), 0.0)
    l_ak = jnp.where(strict, _bmm1(a_mt, k_t, "nt"), 0.0)
    m_rb = jnp.where(incl, _bmm1(r_mt, b_t, "nt"), 0.0)
    m_rk = jnp.where(incl, _bmm1(r_mt, k_t, "nt"), 0.0)

    tinv = eye.astype(F32) + l_ab
    npow = l_ab
    for _ in range(5):
        npow = _bmm1(npow, npow)
        tinv = tinv + _bmm1(tinv, npow)

    st = st_ref[...]
    u = _bmm1(tinv, _bmm1(a_mt, st, "nt") + _bmm1(l_ak, v_mt))
    yy = _bmm1(r_mt, st, "nt") + _bmm1(m_rb, u) + _bmm1(m_rk, v_mt)
    y = yy[:, 0:c] + yy[:, c:2 * c] + yy[:, 2 * c:3 * c] + yy[:, 3 * c:4 * c]
    st_ref[...] = st * p_in[:, c - 1:c, :] + _bmm1(u, b_tail, "tn") + _bmm1(v_mt, k_tail, "tn")

    mean = _rows(_mm2r, y, hm) * (1.0 / HEAD_DIM)
    d = y - mean
    var = _rows(_mm2r, d * d, hm) * (1.0 / HEAD_DIM)
    yn = d * lax.rsqrt(var + RWKV_GN_EPS) * gg_ref[...] + gb_ref[...]
    bonus = _rows(_mm2r, r * k2 * rk_ref[...], hm) * v
    y_ref[...] = ((yn + bonus) * g).astype(y_ref.dtype)


def _rwkv(proj3, params, v_first3):
    b, s, _ = proj3.shape
    nc = s // CHUNK
    bt = _batch_tile(b)
    has_vres = v_first3 is not None
    row = lambda bi, ci: (bi, ci, 0)
    const = lambda bi, ci: (0, 0)
    in_specs = [pl.BlockSpec((bt, CHUNK, 1024), lambda bi, ci: (bi, ci, COL_RWKV // 1024))]
    args = [proj3]
    if has_vres:
        in_specs.append(pl.BlockSpec((bt, CHUNK, RW), row))
        args.append(v_first3)
    for prm in params:
        in_specs.append(pl.BlockSpec(prm.shape, const))
        args.append(prm)
    n = N_HEADS * CHUNK
    return pl.pallas_call(
        functools.partial(_rwkv_body, has_vres),
        grid=(b // bt, nc),
        in_specs=in_specs,
        out_specs=[pl.BlockSpec((bt, CHUNK, RW), row), pl.BlockSpec((bt, CHUNK, RW), row)],
        out_shape=[jax.ShapeDtypeStruct((b, s, RW), BF16), jax.ShapeDtypeStruct((b, s, RW), F32)],
        scratch_shapes=[pltpu.VMEM((bt, n, n), F32), pltpu.VMEM((bt, SUBLANE, 1024), F32)],
        compiler_params=_cparams(("parallel", "arbitrary")),
        name="rwkv7_chunk",
    )(*args)


def _moba_body(nb, n_sel, q_ref, k_ref, v_ref, o_ref, m_ref, acc_ref, qa_ref):
    bl = MOBA_BLOCK
    qb = MOBA_QTILE_BLOCKS if nb % MOBA_QTILE_BLOCKS == 0 else 1
    qt = qb * bl
    nt = nb // qb
    nbp =-(-nb // SUBLANE) * SUBLANE
    scale = HEAD_DIM ** -0.5
    lane = lax.broadcasted_iota(jnp.int32, (bl, LANE), 1)
    head0 = lane < HEAD_DIM
    hmask = (head0, jnp.logical_not(head0))
    spare = (lane - HEAD_DIM, lane)
    blk = lax.broadcasted_iota(jnp.int32, (nbp, bl), 0)
    qpos = lax.broadcasted_iota(jnp.int32, (bl, bl), 0)
    kpos = lax.broadcasted_iota(jnp.int32, (bl, bl), 1)
    causal = kpos <= qpos
    e_row = lax.broadcasted_iota(jnp.int32, (nbp, LANE), 0)
    e_lane = lax.broadcasted_iota(jnp.int32, (nbp, LANE), 1)
    place = ((e_lane == e_row + HEAD_DIM).astype(BF16), (e_lane == e_row).astype(BF16))
    klane = e_lane < HEAD_DIM

    kmean = jnp.mean(k_ref[...].astype(F32).reshape(nb, bl, LANE), axis=1)
    if nbp > nb:
        kmean = jnp.concatenate([kmean, jnp.zeros((nbp - nb, LANE), F32)], axis=0)
    kmean_h = (jnp.where(klane, kmean, 0.0), jnp.where(klane, 0.0, kmean))

    def rows(i):
        if isinstance(i, int):
            return pl.ds(i * bl, bl)
        return pl.ds(pl.multiple_of(i * bl, bl), bl)

    def kv_tiles(j):
        kb = k_ref[rows(j), :].astype(F32) * (scale * LOG2E)
        vb = v_ref[rows(j), :]
        kp = [jnp.where(hmask[h], kb, (spare[h] == j).astype(F32)).astype(BF16) for h in range(2)]
        vp = [jnp.where(hmask[h], vb, 1.0).astype(BF16) for h in range(2)]
        return kp, vp

    dg = qb if qb > 1 else (2 if nb % 2 == 0 else 1)

    def diag_body(ii, carry):
        chains = [(ii * dg + g, h) for g in range(dg) for h in range(2)]
        qf = {g: q_ref[rows(ii * dg + g), :].astype(F32) for g in range(dg)}
        kv = {g: kv_tiles(ii * dg + g) for g in range(dg)}
        qh = [jnp.where(hmask[h], qf[g], 0.0) for g in range(dg) for h in range(2)]
        s = [_dot(qh[c].astype(BF16), kv[c // 2][0][h], "nt") for c, (_, h) in enumerate(chains)]
        bs = [jnp.where(blk < i, _mm3(kmean_h[h], qf[c // 2], "nt"), MASK_VALUE)
              for c, (i, h) in enumerate(chains)]
        sel_t = []
        for c, (i, h) in enumerate(chains):
            rank = jnp.zeros((nbp, bl), jnp.int32)
            for jp in range(nb):
                row = bs[c][jp:jp + 1, :]
                beats = (row > bs[c]) | ((row == bs[c]) & (jp < blk))
                rank = rank + beats.astype(jnp.int32)
            sel_t.append(((rank < n_sel) & (blk < i)).astype(BF16))
        picked = [_dot(sel_t[c], place[h], "tn") for c, (_, h) in enumerate(chains)]
        for c, (i, h) in enumerate(chains):
            sm = jnp.where(causal, s[c], MASK_VALUE)
            m = jnp.max(sm, axis=1, keepdims=True)
            p = jnp.exp2(sm - m)
            m_ref[h, rows(i), :] = jnp.broadcast_to(m, (bl, LANE))
            acc_ref[h, rows(i), :] = _dot(p.astype(BF16), kv[c // 2][1][h])
            bias = jnp.where((spare[h] >= 0) & (spare[h] < nb) & (picked[c] < 0.5), MASK_VALUE, 0.0)
            qa_ref[h, rows(i), :] = (qh[c] + bias).astype(BF16)
        return carry

    lax.fori_loop(0, nb // dg, diag_body, 0)

    def q_tile(t, kp, vp):
        rs = pl.ds(t * qt, qt)
        s = [_dot(qa_ref[h, rs, :], kp[h], "nt") for h in range(2)]
        for h in range(2):
            m_prev = m_ref[h, rs, :]
            m_new = jnp.maximum(m_prev, jnp.max(s[h], axis=1, keepdims=True))
            p = jnp.exp2(s[h] - jnp.concatenate([m_new, m_new], axis=1))
            m_ref[h, rs, :] = m_new
            acc_ref[h, rs, :] = (acc_ref[h, rs, :] * jnp.exp2(m_prev - m_new)
                                 + _dot(p.astype(BF16), vp[h]))

    for t0 in range(nt):
        def key_body(j, carry, t0=t0):
            kp, vp = kv_tiles(j)
            for t in range(t0, nt):
                q_tile(t, kp, vp)
            return carry

        lax.fori_loop(max(0, t0 * qb - 1), min(nb - 1, (t0 + 1) * qb - 1), key_body, 0)

    def out_body(t, carry):
        rs = pl.ds(pl.multiple_of(t * qt, qt), qt)
        a0 = acc_ref[0, rs, :]
        a1 = acc_ref[1, rs, :]
        first = lax.broadcasted_iota(jnp.int32, (qt, LANE), 1) < HEAD_DIM
        out = jnp.where(first, a0 / pltpu.roll(a0, HEAD_DIM, 1), a1 / pltpu.roll(a1, HEAD_DIM, 1))
        o_ref[rs, :] = out.astype(o_ref.dtype)
        return carry

    lax.fori_loop(0, nt, out_body, 0)


def _moba(proj, b, s):
    t = proj.shape[0]
    nb = s // MOBA_BLOCK
    n_sel = min(MOBA_TOPK, nb - 1)
    nhp = 512 // LANE
    qc, kc, vc = COL_MOBA // LANE, (COL_MOBA + 512) // LANE, (COL_MOBA + 1024) // LANE
    return pl.pallas_call(
        functools.partial(_moba_body, nb, n_sel),
        grid=(b, nhp),
        in_specs=[
            pl.BlockSpec((s, LANE), lambda bi, hp: (bi, qc + hp)),
            pl.BlockSpec((s, LANE), lambda bi, hp: (bi, kc + hp)),
            pl.BlockSpec((s, LANE), lambda bi, hp: (bi, vc + hp)),
        ],
        out_specs=pl.BlockSpec((s, LANE), lambda bi, hp: (bi, hp)),
        out_shape=jax.ShapeDtypeStruct((t, 512), BF16),
        scratch_shapes=[pltpu.VMEM((2, s, LANE), F32), pltpu.VMEM((2, s, LANE), F32),
                        pltpu.VMEM((2, s, LANE), BF16)],
        compiler_params=_cparams(("parallel", "parallel")),
        name="moba_attn",
    )(proj, proj, proj)


def _mlstm_body(qk_ref, v_ref, o_ref, gate_ref, cw_ref, cb_ref, brow_ref, hng_ref,
                y_ref, cst_ref, n_ref, m_ref, prev_ref):
    c = CHUNK
    n = N_HEADS * c
    bt = qk_ref.shape[0]

    @pl.when(pl.program_id(1) == 0)
    def _():
        cst_ref[...] = jnp.zeros_like(cst_ref)
        n_ref[...] = jnp.zeros_like(n_ref)
        m_ref[...] = jnp.zeros_like(m_ref)
        prev_ref[...] = jnp.zeros_like(prev_ref)

    x = qk_ref[...].astype(F32)
    prev8 = prev_ref[...]
    cw = cw_ref[...]
    lag = lambda j: _per(lambda xx, p8: _lagged(xx, p8, j), x, prev8)
    conv = cb_ref[...] + cw[3:4] * x + cw[2:3] * lag(1) + cw[1:2] * lag(2) + cw[0:1] * lag(3)
    prev_ref[...] = x[:, c - SUBLANE:, :]
    qk = conv * jax.nn.sigmoid(conv)
    q = qk[:, :, 0:RW]
    k = qk[:, :, RW:2 * RW] * (HEAD_DIM ** -0.5)
    v = v_ref[...].astype(F32)

    same, rp, cp, eye = _head_masks(n)
    hm = same.astype(F32)
    incl = same & (rp >= cp)
    q_mt = _tile_heads(q) * hm
    k_t = _tile_heads(k)
    k_mt = k_t * hm
    v_mt = _tile_heads(v) * hm

    pos = lax.broadcasted_iota(jnp.int32, (c, n), 0)
    e_tile = (pos == (lax.broadcasted_iota(jnp.int32, (c, n), 1) & 63)).astype(F32)
    g8 = _per(lambda z: _mm2r(z, e_tile, "tn")[0:SUBLANE], gate_ref[...].astype(F32))
    sub = lax.broadcasted_iota(jnp.int32, (SUBLANE, n), 0)
    lane_head = lax.broadcasted_iota(jnp.int32, (SUBLANE, n), 1) >> 6
    brow = brow_ref[...]
    li_row = jnp.sum(jnp.where(sub == lane_head, g8, 0.0), axis=1, keepdims=True) + brow[0:1]
    lf_row = -_softplus(-(jnp.sum(jnp.where(sub == lane_head + N_HEADS, g8, 0.0), axis=1, keepdims=True)
                          + brow[1:2]))
    lf_rb = jnp.broadcast_to(lf_row, (bt, SUBLANE, n))
    to_col = jnp.concatenate([incl.astype(F32), hm, eye.astype(F32)], axis=0)
    r8 = jnp.where(sub == 0, lf_rb, jnp.where(sub == 1, li_row, 0.0))
    cols = _per(lambda z: _mm2l(to_col, z, "nt"), r8)
    b_col = cols[:, 0:n, 0:1]
    bend_col = cols[:, n:2 * n, 0:1]
    li_col = cols[:, 2 * n:3 * n, 1:2]
    b_row = _rows(_mm2r, lf_rb, (same & (rp <= cp)).astype(F32))[:, 0:1, :]
    bend_row = _rows(_mm2r, lf_rb, hm)[:, 0:1, :]

    m_col = m_ref[...]
    dmat = jnp.where(incl, b_col - b_row + li_row, -jnp.inf)
    inter = b_col + m_col
    m_t = jnp.maximum(inter, jnp.max(dmat, axis=2, keepdims=True))
    wts = jnp.exp(dmat - m_t)
    s_inter = jnp.exp(inter - m_t)
    cst = cst_ref[...]
    n_row = n_ref[...]
    qk_w = _bmm1(q_mt, k_t, "nt") * wts
    num = s_inter * _bmm1(q_mt, cst, "nt") + _bmm1(qk_w, v_mt)
    den = (s_inter * jnp.sum(q_mt * n_row, axis=2, keepdims=True)
           + jnp.sum(qk_w, axis=2, keepdims=True))
    hh = num / jnp.maximum(jnp.abs(den), jnp.exp(-m_t))
    y = hh[:, 0:c] + hh[:, c:2 * c] + hh[:, 2 * c:3 * c] + hh[:, 3 * c:4 * c]

    g_col = bend_col - b_col + li_col
    g_row = bend_row - b_row + li_row
    g_max = jnp.max(jnp.where(same, g_row, -jnp.inf), axis=2, keepdims=True)
    m_new = jnp.maximum(bend_col + m_col, g_max)
    w_col = jnp.exp(g_col - m_new)
    scale_col = jnp.exp(bend_col + m_col - m_new)
    scale_row = jnp.max(jnp.where(same, scale_col, 0.0), axis=1, keepdims=True)
    cst_ref[...] = scale_col * cst + _bmm1(v_mt * w_col, k_mt, "tn")
    n_ref[...] = scale_row * n_row + jnp.sum(k_mt * w_col, axis=1, keepdims=True)
    m_ref[...] = m_new

    mean = _rows(_mm2r, y, hm) * (1.0 / HEAD_DIM)
    d = y - mean
    var = _rows(_mm2r, d * d, hm) * (1.0 / HEAD_DIM)
    y = d * lax.rsqrt(var + NORM_EPS) * hng_ref[...] * jax.nn.sigmoid(o_ref[...].astype(F32))
    y_ref[...] = y.astype(y_ref.dtype)


def _mlstm(proj3, conv_w, conv_b, i_b, f_b, hn_g):
    b, s, _ = proj3.shape
    c = CHUNK
    nc = s // c
    n = N_HEADS * c
    bt = _batch_tile(b)
    bias = jnp.stack([jnp.repeat(i_b, c), jnp.repeat(f_b, c)], axis=0)
    row = lambda bi, ci: (bi, ci, 0)
    const = lambda bi, ci: (0, 0)
    return pl.pallas_call(
        _mlstm_body,
        grid=(b // bt, nc),
        in_specs=[
            pl.BlockSpec((bt, c, 2 * RW), lambda bi, ci: (bi, ci, COL_MLSTM_QK // (2 * RW))),
            pl.BlockSpec((bt, c, RW), lambda bi, ci: (bi, ci, COL_MLSTM_V // RW)),
            pl.BlockSpec((bt, c, RW), lambda bi, ci: (bi, ci, COL_MLSTM_O // RW)),
            pl.BlockSpec((bt, c, LANE), lambda bi, ci: (bi, ci, COL_MLSTM_G // LANE)),
            pl.BlockSpec(conv_w.shape, const),
            pl.BlockSpec((1, 2 * RW), const),
            pl.BlockSpec((2, n), const),
            pl.BlockSpec((1, RW), const),
        ],
        out_specs=pl.BlockSpec((bt, c, RW), row),
        out_shape=jax.ShapeDtypeStruct((b, s, RW), BF16),
        scratch_shapes=[pltpu.VMEM((bt, n, n), F32), pltpu.VMEM((bt, 1, n), F32),
                        pltpu.VMEM((bt, n, 1), F32), pltpu.VMEM((bt, SUBLANE, 2 * RW), F32)],
        compiler_params=_cparams(("parallel", "arbitrary")),
        name="mlstm_chunk",
    )(proj3, proj3, proj3, proj3, conv_w, conv_b.reshape(1, -1), bias,
      hn_g.reshape(1, -1))


def _merge_body(x_ref, ya_ref, yb_ref, yc_ref, ga_ref, gb_ref, gc_ref, wa_ref, wb_ref, wc_ref,
                wo_ref, g_ref, o_ref):
    merged = (jax.nn.sigmoid(ga_ref[...].astype(F32)) * _mm1(ya_ref[...], wa_ref[...])
              + jax.nn.sigmoid(gb_ref[...].astype(F32)) * _mm1(yb_ref[...], wb_ref[...])
              + jax.nn.sigmoid(gc_ref[...].astype(F32)) * _mm1(yc_ref[...], wc_ref[...]))
    o_ref[...] = x_ref[...] + _rms(_mm1(merged, wo_ref[...]), g_ref[...])


def _merge(x2d, proj, ya, yb, yc, wa, wb, wc, wo, g, tm):
    t, d = x2d.shape
    row = lambda i: (i, 0)
    const = lambda i: (0, 0)
    return pl.pallas_call(
        _merge_body,
        grid=(t // tm,),
        in_specs=[
            pl.BlockSpec((tm, d), row),
            pl.BlockSpec((tm, ya.shape[1]), row),
            pl.BlockSpec((tm, yb.shape[1]), row),
            pl.BlockSpec((tm, yc.shape[1]), row),
            pl.BlockSpec((tm, d), lambda i: (i, 0)),
            pl.BlockSpec((tm, d), lambda i: (i, 1)),
            pl.BlockSpec((tm, d), lambda i: (i, 2)),
            pl.BlockSpec(wa.shape, const),
            pl.BlockSpec(wb.shape, const),
            pl.BlockSpec(wc.shape, const),
            pl.BlockSpec(wo.shape, const),
            pl.BlockSpec((1, d), const),
        ],
        out_specs=pl.BlockSpec((tm, d), row),
        out_shape=jax.ShapeDtypeStruct((t, d), F32),
        compiler_params=_cparams(("parallel",)),
        name="merge_out",
    )(x2d, ya, yb, yc, proj, proj, proj, wa, wb, wc, wo, g.reshape(1, d))


FFN_ROW_GROUPS = 2


def _ffn_body(nt_seq, n_ff, x_ref, g1_ref, upg_ref, upv_ref, cwg_ref, cwv_ref, cbg_ref, cbv_ref,
              down_ref, g2_ref, g3_ref, pg_ref, pp_ref, p_ref, o_ref, h_ref, acc_ref, ugp_ref, uvp_ref):
    i = pl.program_id(0)
    j = pl.program_id(1)
    tm = x_ref.shape[0]

    @pl.when(j == 0)
    def _():
        h_ref[...] = _rms(x_ref[...], g1_ref[...]).astype(BF16)
        acc_ref[...] = jnp.zeros_like(acc_ref)

    @pl.when(i % nt_seq == 0)
    def _():
        ugp_ref[j] = jnp.zeros(ugp_ref.shape[1:], F32)
        uvp_ref[j] = jnp.zeros(uvp_ref.shape[1:], F32)

    ngrp = FFN_ROW_GROUPS if tm % (FFN_ROW_GROUPS * SUBLANE) == 0 else 1
    rg = tm // ngrp
    upg = upg_ref[...]
    upv = upv_ref[...]
    hs = [h_ref[r * rg:(r + 1) * rg, :] for r in range(ngrp)]
    ug = [_dot(hr, upg) for hr in hs]
    uv = [_dot(hr, upv) for hr in hs]
    pg = [ugp_ref[j]] + [ug[r][rg - SUBLANE:, :] for r in range(ngrp - 1)]
    pv = [uvp_ref[j]] + [uv[r][rg - SUBLANE:, :] for r in range(ngrp - 1)]
    ugp_ref[j] = ug[-1][rg - SUBLANE:, :]
    uvp_ref[j] = uv[-1][rg - SUBLANE:, :]
    cwg = cwg_ref[...]
    cwv = cwv_ref[...]
    down = down_ref[...]
    for r in range(ngrp):
        cg = (cbg_ref[...] + cwg[2:3] * ug[r] + cwg[1:2] * _lagged(ug[r], pg[r], 1)
              + cwg[0:1] * _lagged(ug[r], pg[r], 2))
        cv = (cbv_ref[...] + cwv[2:3] * uv[r] + cwv[1:2] * _lagged(uv[r], pv[r], 1)
              + cwv[0:1] * _lagged(uv[r], pv[r], 2))
        act = jax.nn.gelu(cg, approximate=True) * cv
        acc_ref[r * rg:(r + 1) * rg, :] += _mm1(act, down)

    @pl.when(j == n_ff - 1)
    def _():
        x2 = x_ref[...] + _rms(acc_ref[...], g2_ref[...])
        gate = jax.nn.sigmoid(_mm1(_rms(x2, g3_ref[...]), pg_ref[...]))
        o_ref[...] = x2 + gate * _mm1(p_ref[...], pp_ref[...])


def _ffn(x2d, p2d, s, g1, up, cw, cb, down, g2, g3, pgate, pproj, tm, tf):
    t, d = x2d.shape
    dff = down.shape[0]
    n_ff = dff // tf
    nt_seq = s // tm
    ple = p2d.shape[1]
    row = lambda i, j: (i, 0)
    const = lambda i, j: (0, 0)
    cb2 = cb.reshape(1, -1)
    return pl.pallas_call(
        functools.partial(_ffn_body, nt_seq, n_ff),
        grid=(t // tm, n_ff),
        in_specs=[
            pl.BlockSpec((tm, d), row),
            pl.BlockSpec((1, d), const),
            pl.BlockSpec((d, tf), lambda i, j: (0, j)),
            pl.BlockSpec((d, tf), lambda i, j: (0, n_ff + j)),
            pl.BlockSpec((cw.shape[0], tf), lambda i, j: (0, j)),
            pl.BlockSpec((cw.shape[0], tf), lambda i, j: (0, n_ff + j)),
            pl.BlockSpec((1, tf), lambda i, j: (0, j)),
            pl.BlockSpec((1, tf), lambda i, j: (0, n_ff + j)),
            pl.BlockSpec((tf, d), lambda i, j: (j, 0)),
            pl.BlockSpec((1, d), const),
            pl.BlockSpec((1, d), const),
            pl.BlockSpec(pgate.shape, const),
            pl.BlockSpec(pproj.shape, const),
            pl.BlockSpec((tm, ple), row),
        ],
        out_specs=pl.BlockSpec((tm, d), row),
        out_shape=jax.ShapeDtypeStruct((t, d), F32),
        scratch_shapes=[pltpu.VMEM((tm, d), BF16), pltpu.VMEM((tm, d), F32),
                        pltpu.VMEM((n_ff, SUBLANE, tf), F32), pltpu.VMEM((n_ff, SUBLANE, tf), F32)],
        compiler_params=_cparams(("arbitrary", "arbitrary")),
        name="ffn_ple",
    )(x2d, g1.reshape(1, d), up, up, cw, cw, cb2, cb2, down, g2.reshape(1, d), g3.reshape(1, d),
      pgate, pproj, p2d)


def _pack_w_in(w):
    d = w.shape[0]
    rwkv = w[:, 0:1024]
    moba = w[:, 1024:2560]
    ml = w[:, 2560:3592]
    gate = w[:, 3592:6664]
    pad = jnp.zeros((d, PACKED_WIDTH - COL_MLSTM_G - 2 * N_HEADS), w.dtype)
    packed = jnp.concatenate([gate, rwkv, moba, ml, pad], axis=1)
    return packed.astype(BF16)


def _row_tile(t, want):
    return want if t % want == 0 else t


def kernel(x, p, ln_mix_pre, ln_mix_post, ln_ffn_pre, ln_ffn_post, ln_ple, w_in, rwkv_mu, rwkv_w0, rwkv_w2, rwkv_a0, rwkv_a2, rwkv_g2, rwkv_k_k, rwkv_k_a, rwkv_r_k, rwkv_gn_g, rwkv_gn_b, rwkv_v0, rwkv_v1, rwkv_v2, mlstm_conv_w, mlstm_conv_b, mlstm_i_b, mlstm_f_b, mlstm_hn_g, w_br_rwkv, w_br_moba, w_br_mlstm, w_out, ffn_up, ffn_conv_w, ffn_conv_b, ffn_down, ple_proj, ple_gate):
    b, s, d = x.shape
    depth = w_in.shape[0]
    t = b * s
    assert d == 1024 and s % MOBA_BLOCK == 0 and w_in.shape[2] == 6664
    xf = x.reshape(t, d)
    tm_proj = _row_tile(t, 2048)
    tm = min(512, s)
    tm_ffn = tm
    r2 = lambda a: a.reshape(1, -1)
    v_first = None
    for i in range(depth):
        proj = _norm_proj(xf, ln_mix_pre[i], _pack_w_in(w_in[i]), tm_proj, 1024)
        params = [r2(rwkv_mu[i]), r2(rwkv_w0[i]), rwkv_w2[i], r2(rwkv_a0[i]), rwkv_a2[i], rwkv_g2[i],
                  r2(rwkv_k_k[i]), r2(rwkv_k_a[i]), r2(rwkv_r_k[i]), r2(rwkv_gn_g[i]), r2(rwkv_gn_b[i])]
        if i > 0:
            params += [r2(rwkv_v0[i - 1]), rwkv_v1[i - 1], rwkv_v2[i - 1]]
        proj3 = proj.reshape(b, s, -1)
        y_a, v_cur = _rwkv(proj3, params, v_first if i > 0 else None)
        if i == 0:
            v_first = v_cur
        y_b = _moba(proj, b, s)
        y_c = _mlstm(proj3, mlstm_conv_w[i], mlstm_conv_b[i], mlstm_i_b[i], mlstm_f_b[i], mlstm_hn_g[i])
        xf = _merge(xf, proj, y_a.reshape(t, -1), y_b, y_c.reshape(t, -1), w_br_rwkv[i].astype(BF16), w_br_moba[i].astype(BF16),
                    w_br_mlstm[i].astype(BF16), w_out[i].astype(BF16), ln_mix_post[i], tm)
        xf = _ffn(xf, p[i].reshape(t, -1), s, ln_ffn_pre[i], ffn_up[i].astype(BF16), ffn_conv_w[i],
                  ffn_conv_b[i], ffn_down[i].astype(BF16), ln_ffn_post[i], ln_ple[i],
                  ple_gate[i].astype(BF16), ple_proj[i].astype(BF16), tm_ffn, 1408)
    return xf.reshape(b, s, d)
```

```python
import functools

import jax
import jax.numpy as jnp
from jax import lax
from jax.experimental import pallas as pl
from jax.experimental.pallas import tpu as pltpu

F32 = jnp.float32
BF16 = jnp.bfloat16

HEAD_DIM = 64
N_HEADS = 4
RW = N_HEADS * HEAD_DIM
BATCH_TILE = 8
CHUNK = 64
MOBA_BLOCK = 256
MOBA_TOPK = 3
MOBA_QTILE_BLOCKS = 4
NORM_EPS = 1e-6
RWKV_GN_EPS = 64e-5
MASK_VALUE = -1e30
LOG2E = 1.4426950408889634
LANE = 128
SUBLANE = 8
VMEM_LIMIT = 56 * 1024 * 1024

COL_GATE = 0
COL_RWKV = 3072
COL_MOBA = 4096
COL_MLSTM_QK = 5632
COL_MLSTM_V = 6144
COL_MLSTM_O = 6400
COL_MLSTM_G = 6656
PACKED_WIDTH = 7168

_DIMS = {
    "nn": (((1,), (0,)), ((), ())),
    "nt": (((1,), (1,)), ((), ())),
    "tn": (((0,), (0,)), ((), ())),
}


def _dot(a, b, dims="nn"):
    return lax.dot_general(a, b, _DIMS[dims], preferred_element_type=F32)


def _split(a):
    hi = a.astype(BF16)
    lo = (a - hi.astype(F32)).astype(BF16)
    return hi, lo


def _mm1(a, b, dims="nn"):
    return _dot(a.astype(BF16), b.astype(BF16), dims)


def _mm3(a, b, dims="nn"):
    ah, al = _split(a)
    bh, bl = _split(b)
    return _dot(ah, bh, dims) + (_dot(ah, bl, dims) + _dot(al, bh, dims))


def _mm2r(a, e, dims="nn"):
    ah, al = _split(a)
    eb = e.astype(BF16)
    return _dot(ah, eb, dims) + _dot(al, eb, dims)


def _mm2l(e, a, dims="nn"):
    ah, al = _split(a)
    eb = e.astype(BF16)
    return _dot(eb, ah, dims) + _dot(eb, al, dims)


def _softplus(x):
    return jnp.maximum(x, 0.0) + jnp.log(1.0 + jnp.exp(-jnp.abs(x)))


def _rms(x, g):
    ms = jnp.mean(x * x, axis=-1, keepdims=True)
    return x * lax.rsqrt(ms + NORM_EPS) * g


def _lagged(x, prev8, lag):
    full = jnp.concatenate([prev8, x], axis=0)
    return pltpu.roll(full, lag, 0)[SUBLANE:, :]


def _tile_heads(x):
    return jnp.concatenate([x] * N_HEADS, axis=-2)


def _per(fn, *arrs):
    return jnp.stack([fn(*(a[i] for a in arrs)) for i in range(arrs[0].shape[0])])


def _rows(fn, x, w):
    lead = x.shape[:-1]
    return fn(x.reshape(-1, x.shape[-1]), w).reshape(*lead, -1)


def _bmm1(a, b, dims="nn"):
    return _per(lambda x, y: _mm1(x, y, dims), a, b)


def _batch_tile(b):
    return BATCH_TILE if b % BATCH_TILE == 0 else 1


def _head_masks(n):
    ri = lax.broadcasted_iota(jnp.int32, (n, n), 0)
    ci = lax.broadcasted_iota(jnp.int32, (n, n), 1)
    same = (ri >> 6) == (ci >> 6)
    rp = ri & 63
    cp = ci & 63
    return same, rp, cp, ri == ci


def _cparams(sem):
    return pltpu.CompilerParams(dimension_semantics=sem, vmem_limit_bytes=VMEM_LIMIT)


def _proj_body(x_ref, g_ref, w_ref, o_ref, h_ref):
    @pl.when(pl.program_id(1) == 0)
    def _():
        h_ref[...] = _rms(x_ref[...], g_ref[...]).astype(BF16)

    o_ref[...] = _dot(h_ref[...], w_ref[...]).astype(o_ref.dtype)


def _norm_proj(x2d, g, w, tm, tn):
    t, d = x2d.shape
    n = w.shape[1]
    return pl.pallas_call(
        _proj_body,
        grid=(t // tm, n // tn),
        in_specs=[
            pl.BlockSpec((tm, d), lambda i, j: (i, 0)),
            pl.BlockSpec((1, d), lambda i, j: (0, 0)),
            pl.BlockSpec((d, tn), lambda i, j: (0, j)),
        ],
        out_specs=pl.BlockSpec((tm, tn), lambda i, j: (i, j)),
        out_shape=jax.ShapeDtypeStruct((t, n), BF16),
        scratch_shapes=[pltpu.VMEM((tm, d), BF16)],
        compiler_params=_cparams(("parallel", "arbitrary")),
        name="norm_proj",
    )(x2d, g.reshape(1, d), w)


def _rwkv_body(has_vres, *refs):
    if has_vres:
        (slab_ref, vf_ref, mu_ref, w0_ref, w2_ref, a0_ref, a2_ref, g2_ref, kk_ref, ka_ref,
         rk_ref, gg_ref, gb_ref, v0_ref, v1_ref, v2_ref, y_ref, vo_ref, st_ref, prev_ref) = refs
    else:
        (slab_ref, mu_ref, w0_ref, w2_ref, a0_ref, a2_ref, g2_ref, kk_ref, ka_ref,
         rk_ref, gg_ref, gb_ref, y_ref, vo_ref, st_ref, prev_ref) = refs
    c = CHUNK

    @pl.when(pl.program_id(1) == 0)
    def _():
        st_ref[...] = jnp.zeros_like(st_ref)
        prev_ref[...] = jnp.zeros_like(prev_ref)

    slab = slab_ref[...].astype(F32)
    shifted = _per(lambda x, p8: _lagged(x, p8, 1), slab, prev_ref[...])
    prev_ref[...] = slab[:, c - SUBLANE:, :]
    xs = slab + mu_ref[...] * (shifted - slab)
    r = xs[:, :, 0:RW]
    k = xs[:, :, RW:2 * RW]
    v = xs[:, :, 2 * RW:3 * RW]
    xw = xs[:, :, 768:832]
    xa = xs[:, :, 832:896]
    xg = xs[:, :, 896:1024]
    wlog = -_softplus(-(w0_ref[...] + _rows(_mm3, jnp.tanh(xw), w2_ref[...]))) - 0.5
    alr = jax.nn.sigmoid(a0_ref[...] + _rows(_mm3, xa, a2_ref[...]))
    g = _rows(_mm3, jax.nn.sigmoid(xg), g2_ref[...])
    if has_vres:
        mix = jax.nn.sigmoid(v0_ref[...] + _rows(_mm3, _rows(_mm3, v, v1_ref[...]), v2_ref[...]))
        v = v + (vf_ref[...] - v) * mix
    vo_ref[...] = v

    n = N_HEADS * c
    same, _, _, _ = _head_masks(n)
    hm = same.astype(F32)
    pos = lax.broadcasted_iota(jnp.int32, (c, n), 0)
    lane_pos = lax.broadcasted_iota(jnp.int32, (c, n), 1) & 63

    kk = k * kk_ref[...]
    kk = kk / jnp.maximum(jnp.sqrt(_rows(_mm2r, kk * kk, hm)), 1e-12)
    k2 = k * (1.0 + (alr - 1.0) * ka_ref[...])

    lw = -jnp.exp(wlog)
    ti = lax.broadcasted_iota(jnp.int32, (c, c), 0)
    tj = lax.broadcasted_iota(jnp.int32, (c, c), 1)
    tril = (ti >= tj).astype(F32)
    cs = _per(lambda x: _mm2l(tril, x), lw)
    cs_end = cs[:, c - 1:c, :]
    p_in = jnp.exp(cs)
    p_ex = jnp.exp(cs - lw)
    p_inv = jnp.exp(-cs)
    p_tail = jnp.exp(cs_end - cs)

    hm16 = same.astype(BF16)

    def expand(x):
        return _tile_heads(x.astype(BF16)) * hm16

    a_c = -kk * p_ex
    r_c = r * p_in
    b_bd = expand(kk * alr * p_inv)
    k_bd = expand(k2 * p_inv)
    v_bd = expand(v)
    strict = pos > lane_pos
    incl = pos >= lane_pos
    l_ab = jnp.where(strict, _bmm1(a_c, b_bd, "nt"), 0.0)
    l_ak = jnp.where(strict, _bmm1(a_c, k_bd, "nt"), 0.0)
    m_rb = jnp.where(incl, _bmm1(r_c, b_bd, "nt"), 0.0)
    m_rk = jnp.where(incl, _bmm1(r_c, k_bd, "nt"), 0.0)

    tinv = (pos == lane_pos).astype(F32) + l_ab
    npow = l_ab
    for _ in range(5):
        npow = _bmm1(npow, expand(npow))
        tinv = tinv + _bmm1(tinv, expand(npow))

    st = st_ref[...]
    u = _bmm1(tinv, expand(_bmm1(a_c, st, "nt") + _bmm1(l_ak, v_bd)))
    y = _bmm1(r_c, st, "nt") + _bmm1(m_rb, expand(u)) + _bmm1(m_rk, v_bd)
    upd = _bmm1(u, kk * alr * p_tail, "tn") + _bmm1(v, k2 * p_tail, "tn")
    st_ref[...] = st * p_in[:, c - 1:c, :] + jnp.where(same, upd, 0.0)

    mean = _rows(_mm2r, y, hm) * (1.0 / HEAD_DIM)
    d = y - mean
    var = _rows(_mm2r, d * d, hm) * (1.0 / HEAD_DIM)
    yn = d * lax.rsqrt(var + RWKV_GN_EPS) * gg_ref[...] + gb_ref[...]
    bonus = _rows(_mm2r, r * k2 * rk_ref[...], hm) * v
    y_ref[...] = ((yn + bonus) * g).astype(y_ref.dtype)


def _rwkv(proj3, params, v_first3):
    b, s, _ = proj3.shape
    nc = s // CHUNK
    bt = _batch_tile(b)
    has_vres = v_first3 is not None
    row = lambda bi, ci: (bi, ci, 0)
    const = lambda bi, ci: (0, 0)
    in_specs = [pl.BlockSpec((bt, CHUNK, 1024), lambda bi, ci: (bi, ci, COL_RWKV // 1024))]
    args = [proj3]
    if has_vres:
        in_specs.append(pl.BlockSpec((bt, CHUNK, RW), row))
        args.append(v_first3)
    for prm in params:
        in_specs.append(pl.BlockSpec(prm.shape, const))
        args.append(prm)
    n = N_HEADS * CHUNK
    return pl.pallas_call(
        functools.partial(_rwkv_body, has_vres),
        grid=(b // bt, nc),
        in_specs=in_specs,
        out_specs=[pl.BlockSpec((bt, CHUNK, RW), row), pl.BlockSpec((bt, CHUNK, RW), row)],
        out_shape=[jax.ShapeDtypeStruct((b, s, RW), BF16), jax.ShapeDtypeStruct((b, s, RW), F32)],
        scratch_shapes=[pltpu.VMEM((bt, n, n), F32), pltpu.VMEM((bt, SUBLANE, 1024), F32)],
        compiler_params=_cparams(("parallel", "arbitrary")),
        name="rwkv7_chunk",
    )(*args)


def _moba_body(nb, n_sel, q_ref, k_ref, v_ref, o_ref, m_ref, acc_ref, qa_ref):
    bl = MOBA_BLOCK
    qb = MOBA_QTILE_BLOCKS if nb % MOBA_QTILE_BLOCKS == 0 else 1
    qt = qb * bl
    nt = nb // qb
    nbp =-(-nb // SUBLANE) * SUBLANE
    scale = HEAD_DIM ** -0.5
    lane = lax.broadcasted_iota(jnp.int32, (bl, LANE), 1)
    head0 = lane < HEAD_DIM
    hmask = (head0, jnp.logical_not(head0))
    spare = (lane - HEAD_DIM, lane)
    blk = lax.broadcasted_iota(jnp.int32, (nbp, bl), 0)
    qpos = lax.broadcasted_iota(jnp.int32, (bl, bl), 0)
    kpos = lax.broadcasted_iota(jnp.int32, (bl, bl), 1)
    causal = kpos <= qpos
    e_row = lax.broadcasted_iota(jnp.int32, (nbp, LANE), 0)
    e_lane = lax.broadcasted_iota(jnp.int32, (nbp, LANE), 1)
    place = ((e_lane == e_row + HEAD_DIM).astype(BF16), (e_lane == e_row).astype(BF16))
    klane = e_lane < HEAD_DIM

    kmean = jnp.mean(k_ref[...].astype(F32).reshape(nb, bl, LANE), axis=1)
    if nbp > nb:
        kmean = jnp.concatenate([kmean, jnp.zeros((nbp - nb, LANE), F32)], axis=0)
    kmean_h = (jnp.where(klane, kmean, 0.0), jnp.where(klane, 0.0, kmean))

    def rows(i):
        if isinstance(i, int):
            return pl.ds(i * bl, bl)
        return pl.ds(pl.multiple_of(i * bl, bl), bl)

    def kv_tiles(j):
        kb = k_ref[rows(j), :].astype(F32) * (scale * LOG2E)
        vb = v_ref[rows(j), :]
        kp = [jnp.where(hmask[h], kb, (spare[h] == j).astype(F32)).astype(BF16) for h in range(2)]
        vp = [jnp.where(hmask[h], vb, 1.0).astype(BF16) for h in range(2)]
        return kp, vp

    dg = qb if qb > 1 else (2 if nb % 2 == 0 else 1)

    def diag_body(ii, carry):
        chains = [(ii * dg + g, h) for g in range(dg) for h in range(2)]
        qf = {g: q_ref[rows(ii * dg + g), :].astype(F32) for g in range(dg)}
        kv = {g: kv_tiles(ii * dg + g) for g in range(dg)}
        qh = [jnp.where(hmask[h], qf[g], 0.0) for g in range(dg) for h in range(2)]
        s = [_dot(qh[c].astype(BF16), kv[c // 2][0][h], "nt") for c, (_, h) in enumerate(chains)]
        bs = [jnp.where(blk < i, _mm3(kmean_h[h], qf[c // 2], "nt"), MASK_VALUE)
              for c, (i, h) in enumerate(chains)]
        sel_t = []
        for c, (i, h) in enumerate(chains):
            rank = jnp.zeros((nbp, bl), jnp.int32)
            for jp in range(nb):
                row = bs[c][jp:jp + 1, :]
                beats = (row > bs[c]) | ((row == bs[c]) & (jp < blk))
                rank = rank + beats.astype(jnp.int32)
            sel_t.append(((rank < n_sel) & (blk < i)).astype(BF16))
        picked = [_dot(sel_t[c], place[h], "tn") for c, (_, h) in enumerate(chains)]
        for c, (i, h) in enumerate(chains):
            sm = jnp.where(causal, s[c], MASK_VALUE)
            m = jnp.max(sm, axis=1, keepdims=True)
            p = jnp.exp2(sm - m)
            m_ref[h, rows(i), :] = jnp.broadcast_to(m, (bl, LANE))
            acc_ref[h, rows(i), :] = _dot(p.astype(BF16), kv[c // 2][1][h])
            bias = jnp.where((spare[h] >= 0) & (spare[h] < nb) & (picked[c] < 0.5), MASK_VALUE, 0.0)
            qa_ref[h, rows(i), :] = (qh[c] + bias).astype(BF16)
        return carry

    lax.fori_loop(0, nb // dg, diag_body, 0)

    def q_tile(t, kp, vp):
        rs = pl.ds(t * qt, qt)
        s = [_dot(qa_ref[h, rs, :], kp[h], "nt") for h in range(2)]
        for h in range(2):
            m_prev = m_ref[h, rs, :]
            m_new = jnp.maximum(m_prev, jnp.max(s[h], axis=1, keepdims=True))
            p = jnp.exp2(s[h] - jnp.concatenate([m_new, m_new], axis=1))
            m_ref[h, rs, :] = m_new
            acc_ref[h, rs, :] = (acc_ref[h, rs, :] * jnp.exp2(m_prev - m_new)
                                 + _dot(p.astype(BF16), vp[h]))

    for t0 in range(nt):
        def key_body(j, carry, t0=t0):
            kp, vp = kv_tiles(j)
            for t in range(t0, nt):
                q_tile(t, kp, vp)
            return carry

        lax.fori_loop(max(0, t0 * qb - 1), min(nb - 1, (t0 + 1) * qb - 1), key_body, 0)

    def out_body(t, carry):
        rs = pl.ds(pl.multiple_of(t * qt, qt), qt)
        a0 = acc_ref[0, rs, :]
        a1 = acc_ref[1, rs, :]
        first = lax.broadcasted_iota(jnp.int32, (qt, LANE), 1) < HEAD_DIM
        out = jnp.where(first, a0 / pltpu.roll(a0, HEAD_DIM, 1), a1 / pltpu.roll(a1, HEAD_DIM, 1))
        o_ref[rs, :] = out.astype(o_ref.dtype)
        return carry

    lax.fori_loop(0, nt, out_body, 0)


def _moba(proj, b, s):
    t = proj.shape[0]
    nb = s // MOBA_BLOCK
    n_sel = min(MOBA_TOPK, nb - 1)
    nhp = 512 // LANE
    qc, kc, vc = COL_MOBA // LANE, (COL_MOBA + 512) // LANE, (COL_MOBA + 1024) // LANE
    return pl.pallas_call(
        functools.partial(_moba_body, nb, n_sel),
        grid=(b, nhp),
        in_specs=[
            pl.BlockSpec((s, LANE), lambda bi, hp: (bi, qc + hp)),
            pl.BlockSpec((s, LANE), lambda bi, hp: (bi, kc + hp)),
            pl.BlockSpec((s, LANE), lambda bi, hp: (bi, vc + hp)),
        ],
        out_specs=pl.BlockSpec((s, LANE), lambda bi, hp: (bi, hp)),
        out_shape=jax.ShapeDtypeStruct((t, 512), BF16),
        scratch_shapes=[pltpu.VMEM((2, s, LANE), F32), pltpu.VMEM((2, s, LANE), F32),
                        pltpu.VMEM((2, s, LANE), BF16)],
        compiler_params=_cparams(("parallel", "parallel")),
        name="moba_attn",
    )(proj, proj, proj)


def _mlstm_body(qk_ref, v_ref, o_ref, gate_ref, cw_ref, cb_ref, gbias_ref, hng_ref,
                y_ref, cst_ref, n_ref, m_ref, prev_ref):
    c = CHUNK
    n = N_HEADS * c

    @pl.when(pl.program_id(1) == 0)
    def _():
        cst_ref[...] = jnp.zeros_like(cst_ref)
        n_ref[...] = jnp.zeros_like(n_ref)
        m_ref[...] = jnp.zeros_like(m_ref)
        prev_ref[...] = jnp.zeros_like(prev_ref)

    x = qk_ref[...].astype(F32)
    prev8 = prev_ref[...]
    cw = cw_ref[...]
    lag = lambda j: _per(lambda xx, p8: _lagged(xx, p8, j), x, prev8)
    conv = cb_ref[...] + cw[3:4] * x + cw[2:3] * lag(1) + cw[1:2] * lag(2) + cw[0:1] * lag(3)
    prev_ref[...] = x[:, c - SUBLANE:, :]
    qk = conv * jax.nn.sigmoid(conv)
    q = qk[:, :, 0:RW]
    k = qk[:, :, RW:2 * RW] * (HEAD_DIM ** -0.5)
    v = v_ref[...].astype(F32)

    same, _, _, _ = _head_masks(n)
    hm = same.astype(F32)
    hm16 = same.astype(BF16)
    k_mt = _tile_heads(k.astype(BF16)) * hm16
    v_mt = _tile_heads(v.astype(BF16)) * hm16

    g = gate_ref[...].astype(F32) + gbias_ref[...]
    glane = lax.broadcasted_iota(jnp.int32, (c, LANE), 1)
    lg = jnp.where(glane < N_HEADS, g, -_softplus(-g))
    ti = lax.broadcasted_iota(jnp.int32, (c, c), 0)
    tj = lax.broadcasted_iota(jnp.int32, (c, c), 1)
    tril = (ti >= tj).astype(F32)
    cum = _per(lambda z: _mm2l(tril, z), lg)
    src = lax.broadcasted_iota(jnp.int32, (LANE, n), 0)
    dst_head = lax.broadcasted_iota(jnp.int32, (LANE, n), 1) >> 6
    li_e = _rows(_mm2r, lg, (src == dst_head).astype(F32))
    b_e = _rows(_mm2r, cum, (src == dst_head + N_HEADS).astype(F32))
    pos = lax.broadcasted_iota(jnp.int32, (c, n), 0)
    lane_pos = lax.broadcasted_iota(jnp.int32, (c, n), 1) & 63
    lane_head = lax.broadcasted_iota(jnp.int32, (c, n), 1) >> 6
    li_row = jnp.sum(jnp.where(pos == lane_pos, li_e, 0.0), axis=1, keepdims=True)
    b_row = jnp.sum(jnp.where(pos == lane_pos, b_e, 0.0), axis=1, keepdims=True)
    bend = b_e[:, c - 1:c, :]

    m_row = m_ref[...]
    dmat = jnp.where(pos >= lane_pos, b_e - b_row + li_row, -jnp.inf)
    inter = b_e + m_row
    m_t = inter
    for h in range(N_HEADS):
        in_head = lane_head == h
        mh = jnp.max(jnp.where(in_head, dmat, -jnp.inf), axis=2, keepdims=True)
        m_t = jnp.where(in_head, jnp.maximum(m_t, mh), m_t)
    wts = jnp.exp(dmat - m_t)
    s_inter = jnp.exp(inter - m_t)
    cst = cst_ref[...]
    n_row = n_ref[...]
    qk_w = _bmm1(q, k_mt, "nt") * wts
    num = s_inter * _bmm1(q, cst, "nt") + _bmm1(qk_w, v_mt)
    den = _rows(_mm2r, s_inter * (q * n_row) + qk_w, hm)
    y = num / jnp.maximum(jnp.abs(den), jnp.exp(-m_t))

    g_e = bend - b_e + li_e
    m_new = jnp.maximum(bend + m_row, jnp.max(g_e, axis=1, keepdims=True))
    w_e = jnp.exp(g_e - m_new)
    scale = jnp.exp(bend + m_row - m_new)
    cst_ref[...] = scale * cst + jnp.where(same, _bmm1(v * w_e, k, "tn"), 0.0)
    n_ref[...] = scale * n_row + jnp.sum(k * w_e, axis=1, keepdims=True)
    m_ref[...] = m_new

    mean = _rows(_mm2r, y, hm) * (1.0 / HEAD_DIM)
    d = y - mean
    var = _rows(_mm2r, d * d, hm) * (1.0 / HEAD_DIM)
    y = d * lax.rsqrt(var + NORM_EPS) * hng_ref[...] * jax.nn.sigmoid(o_ref[...].astype(F32))
    y_ref[...] = y.astype(y_ref.dtype)


def _mlstm(proj3, conv_w, conv_b, i_b, f_b, hn_g):
    b, s, _ = proj3.shape
    c = CHUNK
    nc = s // c
    n = N_HEADS * c
    bt = _batch_tile(b)
    gbias = jnp.concatenate([i_b, f_b, jnp.zeros((LANE - 2 * N_HEADS,), F32)]).reshape(1, LANE)
    row = lambda bi, ci: (bi, ci, 0)
    const = lambda bi, ci: (0, 0)
    return pl.pallas_call(
        _mlstm_body,
        grid=(b // bt, nc),
        in_specs=[
            pl.BlockSpec((bt, c, 2 * RW), lambda bi, ci: (bi, ci, COL_MLSTM_QK // (2 * RW))),
            pl.BlockSpec((bt, c, RW), lambda bi, ci: (bi, ci, COL_MLSTM_V // RW)),
            pl.BlockSpec((bt, c, RW), lambda bi, ci: (bi, ci, COL_MLSTM_O // RW)),
            pl.BlockSpec((bt, c, LANE), lambda bi, ci: (bi, ci, COL_MLSTM_G // LANE)),
            pl.BlockSpec(conv_w.shape, const),
            pl.BlockSpec((1, 2 * RW), const),
            pl.BlockSpec((1, LANE), const),
            pl.BlockSpec((1, RW), const),
        ],
        out_specs=pl.BlockSpec((bt, c, RW), row),
        out_shape=jax.ShapeDtypeStruct((b, s, RW), BF16),
        scratch_shapes=[pltpu.VMEM((bt, n, n), F32), pltpu.VMEM((bt, 1, n), F32),
                        pltpu.VMEM((bt, 1, n), F32), pltpu.VMEM((bt, SUBLANE, 2 * RW), F32)],
        compiler_params=_cparams(("parallel", "arbitrary")),
        name="mlstm_chunk",
    )(proj3, proj3, proj3, proj3, conv_w, conv_b.reshape(1, -1), gbias,
      hn_g.reshape(1, -1))


def _merge_body(x_ref, ya_ref, yb_ref, yc_ref, ga_ref, gb_ref, gc_ref, wa_ref, wb_ref, wc_ref,
                wo_ref, g_ref, o_ref):
    merged = (jax.nn.sigmoid(ga_ref[...].astype(F32)) * _mm1(ya_ref[...], wa_ref[...])
              + jax.nn.sigmoid(gb_ref[...].astype(F32)) * _mm1(yb_ref[...], wb_ref[...])
              + jax.nn.sigmoid(gc_ref[...].astype(F32)) * _mm1(yc_ref[...], wc_ref[...]))
    o_ref[...] = x_ref[...] + _rms(_mm1(merged, wo_ref[...]), g_ref[...])


def _merge(x2d, proj, ya, yb, yc, wa, wb, wc, wo, g, tm):
    t, d = x2d.shape
    row = lambda i: (i, 0)
    const = lambda i: (0, 0)
    return pl.pallas_call(
        _merge_body,
        grid=(t // tm,),
        in_specs=[
            pl.BlockSpec((tm, d), row),
            pl.BlockSpec((tm, ya.shape[1]), row),
            pl.BlockSpec((tm, yb.shape[1]), row),
            pl.BlockSpec((tm, yc.shape[1]), row),
            pl.BlockSpec((tm, d), lambda i: (i, 0)),
            pl.BlockSpec((tm, d), lambda i: (i, 1)),
            pl.BlockSpec((tm, d), lambda i: (i, 2)),
            pl.BlockSpec(wa.shape, const),
            pl.BlockSpec(wb.shape, const),
            pl.BlockSpec(wc.shape, const),
            pl.BlockSpec(wo.shape, const),
            pl.BlockSpec((1, d), const),
        ],
        out_specs=pl.BlockSpec((tm, d), row),
        out_shape=jax.ShapeDtypeStruct((t, d), F32),
        compiler_params=_cparams(("parallel",)),
        name="merge_out",
    )(x2d, ya, yb, yc, proj, proj, proj, wa, wb, wc, wo, g.reshape(1, d))


FFN_ROW_GROUPS = 2


def _ffn_body(nt_seq, n_ff, x_ref, g1_ref, upg_ref, upv_ref, cwg_ref, cwv_ref, cbg_ref, cbv_ref,
              down_ref, g2_ref, g3_ref, pg_ref, pp_ref, p_ref, o_ref, h_ref, acc_ref, ugp_ref, uvp_ref):
    i = pl.program_id(0)
    j = pl.program_id(1)
    tm = x_ref.shape[0]

    @pl.when(j == 0)
    def _():
        h_ref[...] = _rms(x_ref[...], g1_ref[...]).astype(BF16)
        acc_ref[...] = jnp.zeros_like(acc_ref)

    @pl.when(i % nt_seq == 0)
    def _():
        ugp_ref[j] = jnp.zeros(ugp_ref.shape[1:], F32)
        uvp_ref[j] = jnp.zeros(uvp_ref.shape[1:], F32)

    ngrp = FFN_ROW_GROUPS if tm % (FFN_ROW_GROUPS * SUBLANE) == 0 else 1
    rg = tm // ngrp
    upg = upg_ref[...]
    upv = upv_ref[...]
    hs = [h_ref[r * rg:(r + 1) * rg, :] for r in range(ngrp)]
    ug = [_dot(hr, upg) for hr in hs]
    uv = [_dot(hr, upv) for hr in hs]
    pg = [ugp_ref[j]] + [ug[r][rg - SUBLANE:, :] for r in range(ngrp - 1)]
    pv = [uvp_ref[j]] + [uv[r][rg - SUBLANE:, :] for r in range(ngrp - 1)]
    ugp_ref[j] = ug[-1][rg - SUBLANE:, :]
    uvp_ref[j] = uv[-1][rg - SUBLANE:, :]
    cwg = cwg_ref[...]
    cwv = cwv_ref[...]
    down = down_ref[...]
    for r in range(ngrp):
        cg = (cbg_ref[...] + cwg[2:3] * ug[r] + cwg[1:2] * _lagged(ug[r], pg[r], 1)
              + cwg[0:1] * _lagged(ug[r], pg[r], 2))
        cv = (cbv_ref[...] + cwv[2:3] * uv[r] + cwv[1:2] * _lagged(uv[r], pv[r], 1)
              + cwv[0:1] * _lagged(uv[r], pv[r], 2))
        act = jax.nn.gelu(cg, approximate=True) * cv
        acc_ref[r * rg:(r + 1) * rg, :] += _mm1(act, down)

    @pl.when(j == n_ff - 1)
    def _():
        x2 = x_ref[...] + _rms(acc_ref[...], g2_ref[...])
        gate = jax.nn.sigmoid(_mm1(_rms(x2, g3_ref[...]), pg_ref[...]))
        o_ref[...] = x2 + gate * _mm1(p_ref[...], pp_ref[...])


def _ffn(x2d, p2d, s, g1, up, cw, cb, down, g2, g3, pgate, pproj, tm, tf):
    t, d = x2d.shape
    dff = down.shape[0]
    n_ff = dff // tf
    nt_seq = s // tm
    ple = p2d.shape[1]
    row = lambda i, j: (i, 0)
    const = lambda i, j: (0, 0)
    cb2 = cb.reshape(1, -1)
    return pl.pallas_call(
        functools.partial(_ffn_body, nt_seq, n_ff),
        grid=(t // tm, n_ff),
        in_specs=[
            pl.BlockSpec((tm, d), row),
            pl.BlockSpec((1, d), const),
            pl.BlockSpec((d, tf), lambda i, j: (0, j)),
            pl.BlockSpec((d, tf), lambda i, j: (0, n_ff + j)),
            pl.BlockSpec((cw.shape[0], tf), lambda i, j: (0, j)),
            pl.BlockSpec((cw.shape[0], tf), lambda i, j: (0, n_ff + j)),
            pl.BlockSpec((1, tf), lambda i, j: (0, j)),
            pl.BlockSpec((1, tf), lambda i, j: (0, n_ff + j)),
            pl.BlockSpec((tf, d), lambda i, j: (j, 0)),
            pl.BlockSpec((1, d), const),
            pl.BlockSpec((1, d), const),
            pl.BlockSpec(pgate.shape, const),
            pl.BlockSpec(pproj.shape, const),
            pl.BlockSpec((tm, ple), row),
        ],
        out_specs=pl.BlockSpec((tm, d), row),
        out_shape=jax.ShapeDtypeStruct((t, d), F32),
        scratch_shapes=[pltpu.VMEM((tm, d), BF16), pltpu.VMEM((tm, d), F32),
                        pltpu.VMEM((n_ff, SUBLANE, tf), F32), pltpu.VMEM((n_ff, SUBLANE, tf), F32)],
        compiler_params=_cparams(("arbitrary", "arbitrary")),
        name="ffn_ple",
    )(x2d, g1.reshape(1, d), up, up, cw, cw, cb2, cb2, down, g2.reshape(1, d), g3.reshape(1, d),
      pgate, pproj, p2d)


def _pack_w_in(w):
    d = w.shape[0]
    rwkv = w[:, 0:1024]
    moba = w[:, 1024:2560]
    ml = w[:, 2560:3592]
    gate = w[:, 3592:6664]
    pad = jnp.zeros((d, PACKED_WIDTH - COL_MLSTM_G - 2 * N_HEADS), w.dtype)
    packed = jnp.concatenate([gate, rwkv, moba, ml, pad], axis=1)
    return packed.astype(BF16)


def _row_tile(t, want):
    return want if t % want == 0 else t


def kernel(x, p, ln_mix_pre, ln_mix_post, ln_ffn_pre, ln_ffn_post, ln_ple, w_in, rwkv_mu, rwkv_w0, rwkv_w2, rwkv_a0, rwkv_a2, rwkv_g2, rwkv_k_k, rwkv_k_a, rwkv_r_k, rwkv_gn_g, rwkv_gn_b, rwkv_v0, rwkv_v1, rwkv_v2, mlstm_conv_w, mlstm_conv_b, mlstm_i_b, mlstm_f_b, mlstm_hn_g, w_br_rwkv, w_br_moba, w_br_mlstm, w_out, ffn_up, ffn_conv_w, ffn_conv_b, ffn_down, ple_proj, ple_gate):
    b, s, d = x.shape
    depth = w_in.shape[0]
    t = b * s
    assert d == 1024 and s % MOBA_BLOCK == 0 and w_in.shape[2] == 6664
    xf = x.reshape(t, d)
    tm_proj = _row_tile(t, 2048)
    tm = min(512, s)
    tm_ffn = tm
    r2 = lambda a: a.reshape(1, -1)
    v_first = None
    for i in range(depth):
        proj = _norm_proj(xf, ln_mix_pre[i], _pack_w_in(w_in[i]), tm_proj, 1024)
        params = [r2(rwkv_mu[i]), r2(rwkv_w0[i]), rwkv_w2[i], r2(rwkv_a0[i]), rwkv_a2[i], rwkv_g2[i],
                  r2(rwkv_k_k[i]), r2(rwkv_k_a[i]), r2(rwkv_r_k[i]), r2(rwkv_gn_g[i]), r2(rwkv_gn_b[i])]
        if i > 0:
            params += [r2(rwkv_v0[i - 1]), rwkv_v1[i - 1], rwkv_v2[i - 1]]
        proj3 = proj.reshape(b, s, -1)
        y_a, v_cur = _rwkv(proj3, params, v_first if i > 0 else None)
        if i == 0:
            v_first = v_cur
        y_b = _moba(proj, b, s)
        y_c = _mlstm(proj3, mlstm_conv_w[i], mlstm_conv_b[i], mlstm_i_b[i], mlstm_f_b[i], mlstm_hn_g[i])
        xf = _merge(xf, proj, y_a.reshape(t, -1), y_b, y_c.reshape(t, -1), w_br_rwkv[i].astype(BF16), w_br_moba[i].astype(BF16),
                    w_br_mlstm[i].astype(BF16), w_out[i].astype(BF16), ln_mix_post[i], tm)
        xf = _ffn(xf, p[i].reshape(t, -1), s, ln_ffn_pre[i], ffn_up[i].astype(BF16), ffn_conv_w[i],
                  ffn_conv_b[i], ffn_down[i].astype(BF16), ln_ffn_post[i], ln_ple[i],
                  ple_gate[i].astype(BF16), ple_proj[i].astype(BF16), tm_ffn, 1408)
    return xf.reshape(b, s, d)
```

```python
import functools

import jax
import jax.numpy as jnp
from jax import lax
from jax.experimental import pallas as pl
from jax.experimental.pallas import tpu as pltpu

F32 = jnp.float32
BF16 = jnp.bfloat16

HEAD_DIM = 64
N_HEADS = 4
RW = N_HEADS * HEAD_DIM
BATCH_TILE = 8
CHUNK = 64
MOBA_BLOCK = 256
MOBA_TOPK = 3
MOBA_QTILE_BLOCKS = 4
NORM_EPS = 1e-6
RWKV_GN_EPS = 64e-5
MASK_VALUE = -1e30
LOG2E = 1.4426950408889634
LANE = 128
SUBLANE = 8
VMEM_LIMIT = 56 * 1024 * 1024

COL_GATE = 0
COL_RWKV = 3072
COL_MOBA = 4096
COL_MLSTM_QK = 5632
COL_MLSTM_V = 6144
COL_MLSTM_O = 6400
COL_MLSTM_G = 6656
PACKED_WIDTH = 7168

_DIMS = {
    "nn": (((1,), (0,)), ((), ())),
    "nt": (((1,), (1,)), ((), ())),
    "tn": (((0,), (0,)), ((), ())),
}


def _dot(a, b, dims="nn"):
    return lax.dot_general(a, b, _DIMS[dims], preferred_element_type=F32)


def _split(a):
    hi = a.astype(BF16)
    lo = (a - hi.astype(F32)).astype(BF16)
    return hi, lo


def _mm1(a, b, dims="nn"):
    return _dot(a.astype(BF16), b.astype(BF16), dims)


def _mm3(a, b, dims="nn"):
    ah, al = _split(a)
    bh, bl = _split(b)
    return _dot(ah, bh, dims) + (_dot(ah, bl, dims) + _dot(al, bh, dims))


def _mm2r(a, e, dims="nn"):
    ah, al = _split(a)
    eb = e.astype(BF16)
    return _dot(ah, eb, dims) + _dot(al, eb, dims)


def _mm2l(e, a, dims="nn"):
    ah, al = _split(a)
    eb = e.astype(BF16)
    return _dot(eb, ah, dims) + _dot(eb, al, dims)


def _softplus(x):
    return jnp.maximum(x, 0.0) + jnp.log(1.0 + jnp.exp(-jnp.abs(x)))


def _rms(x, g):
    ms = jnp.mean(x * x, axis=-1, keepdims=True)
    return x * lax.rsqrt(ms + NORM_EPS) * g


def _lagged(x, prev8, lag):
    full = jnp.concatenate([prev8, x], axis=0)
    return pltpu.roll(full, lag, 0)[SUBLANE:, :]


def _tile_heads(x):
    return jnp.concatenate([x] * N_HEADS, axis=-2)


def _per(fn, *arrs):
    return jnp.stack([fn(*(a[i] for a in arrs)) for i in range(arrs[0].shape[0])])


def _rows(fn, x, w):
    lead = x.shape[:-1]
    return fn(x.reshape(-1, x.shape[-1]), w).reshape(*lead, -1)


def _bmm1(a, b, dims="nn"):
    return _per(lambda x, y: _mm1(x, y, dims), a, b)


def _batch_tile(b):
    return BATCH_TILE if b % BATCH_TILE == 0 else 1


def _head_masks(n):
    ri = lax.broadcasted_iota(jnp.int32, (n, n), 0)
    ci = lax.broadcasted_iota(jnp.int32, (n, n), 1)
    same = (ri >> 6) == (ci >> 6)
    rp = ri & 63
    cp = ci & 63
    return same, rp, cp, ri == ci


def _cparams(sem):
    return pltpu.CompilerParams(dimension_semantics=sem, vmem_limit_bytes=VMEM_LIMIT)


def _proj_body(x_ref, g_ref, w_ref, o_ref, h_ref):
    @pl.when(pl.program_id(1) == 0)
    def _():
        h_ref[...] = _rms(x_ref[...], g_ref[...]).astype(BF16)

    o_ref[...] = _dot(h_ref[...], w_ref[...]).astype(o_ref.dtype)


def _norm_proj(x2d, g, w, tm, tn):
    t, d = x2d.shape
    n = w.shape[1]
    return pl.pallas_call(
        _proj_body,
        grid=(t // tm, n // tn),
        in_specs=[
            pl.BlockSpec((tm, d), lambda i, j: (i, 0)),
            pl.BlockSpec((1, d), lambda i, j: (0, 0)),
            pl.BlockSpec((d, tn), lambda i, j: (0, j)),
        ],
        out_specs=pl.BlockSpec((tm, tn), lambda i, j: (i, j)),
        out_shape=jax.ShapeDtypeStruct((t, n), BF16),
        scratch_shapes=[pltpu.VMEM((tm, d), BF16)],
        compiler_params=_cparams(("parallel", "arbitrary")),
        name="norm_proj",
    )(x2d, g.reshape(1, d), w)


def _rwkv_body(has_vres, *refs):
    if has_vres:
        (slab_ref, vf_ref, mu_ref, w0_ref, w2_ref, a0_ref, a2_ref, g2_ref, kk_ref, ka_ref,
         rk_ref, gg_ref, gb_ref, v0_ref, v1_ref, v2_ref, y_ref, vo_ref, st_ref, prev_ref) = refs
    else:
        (slab_ref, mu_ref, w0_ref, w2_ref, a0_ref, a2_ref, g2_ref, kk_ref, ka_ref,
         rk_ref, gg_ref, gb_ref, y_ref, vo_ref, st_ref, prev_ref) = refs
    c = CHUNK

    @pl.when(pl.program_id(1) == 0)
    def _():
        st_ref[...] = jnp.zeros_like(st_ref)
        prev_ref[...] = jnp.zeros_like(prev_ref)

    slab = slab_ref[...].astype(F32)
    shifted = _per(lambda x, p8: _lagged(x, p8, 1), slab, prev_ref[...])
    prev_ref[...] = slab[:, c - SUBLANE:, :]
    xs = slab + mu_ref[...] * (shifted - slab)
    r = xs[:, :, 0:RW]
    k = xs[:, :, RW:2 * RW]
    v = xs[:, :, 2 * RW:3 * RW]
    xw = xs[:, :, 768:832]
    xa = xs[:, :, 832:896]
    xg = xs[:, :, 896:1024]
    wlog = -_softplus(-(w0_ref[...] + _rows(_mm3, jnp.tanh(xw), w2_ref[...]))) - 0.5
    alr = jax.nn.sigmoid(a0_ref[...] + _rows(_mm3, xa, a2_ref[...]))
    g = _rows(_mm3, jax.nn.sigmoid(xg), g2_ref[...])
    if has_vres:
        mix = jax.nn.sigmoid(v0_ref[...] + _rows(_mm3, _rows(_mm3, v, v1_ref[...]), v2_ref[...]))
        v = v + (vf_ref[...] - v) * mix
    vo_ref[...] = v

    n = N_HEADS * c
    same, _, _, _ = _head_masks(n)
    hm = same.astype(F32)
    pos = lax.broadcasted_iota(jnp.int32, (c, n), 0)
    lane_pos = lax.broadcasted_iota(jnp.int32, (c, n), 1) & 63

    kk = k * kk_ref[...]
    kk = kk * lax.rsqrt(jnp.maximum(_rows(_mm2r, kk * kk, hm), 1e-24))
    k2 = k * (1.0 + (alr - 1.0) * ka_ref[...])

    lw = -jnp.exp(wlog)
    ti = lax.broadcasted_iota(jnp.int32, (c, c), 0)
    tj = lax.broadcasted_iota(jnp.int32, (c, c), 1)
    tril = (ti >= tj).astype(F32)
    cs = _per(lambda x: _mm2l(tril, x), lw)
    cs_end = cs[:, c - 1:c, :]
    p_in = jnp.exp(cs)
    p_ex = jnp.exp(cs - lw)
    p_inv = jnp.exp(-cs)
    p_tail = jnp.exp(cs_end - cs)

    hm16 = same.astype(BF16)

    def expand(x):
        return _tile_heads(x.astype(BF16)) * hm16

    a_c = -kk * p_ex
    r_c = r * p_in
    b_bd = expand(kk * alr * p_inv)
    k_bd = expand(k2 * p_inv)
    v_bd = expand(v)
    strict = pos > lane_pos
    incl = pos >= lane_pos
    l_ab = jnp.where(strict, _bmm1(a_c, b_bd, "nt"), 0.0)
    l_ak = jnp.where(strict, _bmm1(a_c, k_bd, "nt"), 0.0)
    m_rb = jnp.where(incl, _bmm1(r_c, b_bd, "nt"), 0.0)
    m_rk = jnp.where(incl, _bmm1(r_c, k_bd, "nt"), 0.0)

    tinv = (pos == lane_pos).astype(F32) + l_ab
    npow = l_ab
    for _ in range(5):
        npow = _bmm1(npow, expand(npow))
        tinv = tinv + _bmm1(tinv, expand(npow))

    st = st_ref[...]
    u = _bmm1(tinv, expand(_bmm1(a_c, st, "nt") + _bmm1(l_ak, v_bd)))
    y = _bmm1(r_c, st, "nt") + _bmm1(m_rb, expand(u)) + _bmm1(m_rk, v_bd)
    upd = _bmm1(u, kk * alr * p_tail, "tn") + _bmm1(v, k2 * p_tail, "tn")
    st_ref[...] = st * p_in[:, c - 1:c, :] + jnp.where(same, upd, 0.0)

    mean = _rows(_mm2r, y, hm) * (1.0 / HEAD_DIM)
    d = y - mean
    var = _rows(_mm2r, d * d, hm) * (1.0 / HEAD_DIM)
    yn = d * lax.rsqrt(var + RWKV_GN_EPS) * gg_ref[...] + gb_ref[...]
    bonus = _rows(_mm2r, r * k2 * rk_ref[...], hm) * v
    y_ref[...] = ((yn + bonus) * g).astype(y_ref.dtype)


def _rwkv(proj3, params, v_first3):
    b, s, _ = proj3.shape
    nc = s // CHUNK
    bt = _batch_tile(b)
    has_vres = v_first3 is not None
    row = lambda bi, ci: (bi, ci, 0)
    const = lambda bi, ci: (0, 0)
    in_specs = [pl.BlockSpec((bt, CHUNK, 1024), lambda bi, ci: (bi, ci, COL_RWKV // 1024))]
    args = [proj3]
    if has_vres:
        in_specs.append(pl.BlockSpec((bt, CHUNK, RW), row))
        args.append(v_first3)
    for prm in params:
        in_specs.append(pl.BlockSpec(prm.shape, const))
        args.append(prm)
    n = N_HEADS * CHUNK
    return pl.pallas_call(
        functools.partial(_rwkv_body, has_vres),
        grid=(b // bt, nc),
        in_specs=in_specs,
        out_specs=[pl.BlockSpec((bt, CHUNK, RW), row), pl.BlockSpec((bt, CHUNK, RW), row)],
        out_shape=[jax.ShapeDtypeStruct((b, s, RW), BF16), jax.ShapeDtypeStruct((b, s, RW), F32)],
        scratch_shapes=[pltpu.VMEM((bt, n, n), F32), pltpu.VMEM((bt, SUBLANE, 1024), F32)],
        compiler_params=_cparams(("parallel", "arbitrary")),
        name="rwkv7_chunk",
    )(*args)


def _moba_body(nb, n_sel, q_ref, k_ref, v_ref, o_ref, m_ref, acc_ref, qa_ref):
    bl = MOBA_BLOCK
    qb = MOBA_QTILE_BLOCKS if nb % MOBA_QTILE_BLOCKS == 0 else 1
    qt = qb * bl
    nt = nb // qb
    nbp =-(-nb // SUBLANE) * SUBLANE
    scale = HEAD_DIM ** -0.5
    lane = lax.broadcasted_iota(jnp.int32, (bl, LANE), 1)
    head0 = lane < HEAD_DIM
    hmask = (head0, jnp.logical_not(head0))
    spare = (lane - HEAD_DIM, lane)
    blk = lax.broadcasted_iota(jnp.int32, (nbp, bl), 0)
    qpos = lax.broadcasted_iota(jnp.int32, (bl, bl), 0)
    kpos = lax.broadcasted_iota(jnp.int32, (bl, bl), 1)
    causal = kpos <= qpos
    e_row = lax.broadcasted_iota(jnp.int32, (nbp, LANE), 0)
    e_lane = lax.broadcasted_iota(jnp.int32, (nbp, LANE), 1)
    place = ((e_lane == e_row + HEAD_DIM).astype(BF16), (e_lane == e_row).astype(BF16))
    klane = e_lane < HEAD_DIM

    kmean = jnp.mean(k_ref[...].astype(F32).reshape(nb, bl, LANE), axis=1)
    if nbp > nb:
        kmean = jnp.concatenate([kmean, jnp.zeros((nbp - nb, LANE), F32)], axis=0)
    kmean_h = (jnp.where(klane, kmean, 0.0), jnp.where(klane, 0.0, kmean))

    def rows(i):
        if isinstance(i, int):
            return pl.ds(i * bl, bl)
        return pl.ds(pl.multiple_of(i * bl, bl), bl)

    def kv_tiles(j):
        kb = k_ref[rows(j), :].astype(F32) * (scale * LOG2E)
        vb = v_ref[rows(j), :]
        kp = [jnp.where(hmask[h], kb, (spare[h] == j).astype(F32)).astype(BF16) for h in range(2)]
        vp = [jnp.where(hmask[h], vb, 1.0).astype(BF16) for h in range(2)]
        return kp, vp

    dg = qb if qb > 1 else (2 if nb % 2 == 0 else 1)

    def diag_body(ii, carry):
        chains = [(ii * dg + g, h) for g in range(dg) for h in range(2)]
        qf = {g: q_ref[rows(ii * dg + g), :].astype(F32) for g in range(dg)}
        kv = {g: kv_tiles(ii * dg + g) for g in range(dg)}
        qh = [jnp.where(hmask[h], qf[g], 0.0) for g in range(dg) for h in range(2)]
        s = [_dot(qh[c].astype(BF16), kv[c // 2][0][h], "nt") for c, (_, h) in enumerate(chains)]
        bs = [jnp.where(blk < i, _mm3(kmean_h[h], qf[c // 2], "nt"), MASK_VALUE)
              for c, (i, h) in enumerate(chains)]
        sel_t = []
        for c, (i, h) in enumerate(chains):
            rank = jnp.zeros((nbp, bl), jnp.int32)
            for jp in range(nb):
                row = bs[c][jp:jp + 1, :]
                beats = (row > bs[c]) | ((row == bs[c]) & (jp < blk))
                rank = rank + beats.astype(jnp.int32)
            sel_t.append(((rank < n_sel) & (blk < i)).astype(BF16))
        picked = [_dot(sel_t[c], place[h], "tn") for c, (_, h) in enumerate(chains)]
        for c, (i, h) in enumerate(chains):
            sm = jnp.where(causal, s[c], MASK_VALUE)
            m = jnp.max(sm, axis=1, keepdims=True)
            p = jnp.exp2(sm - m)
            m_ref[h, rows(i), :] = jnp.broadcast_to(m, (bl, LANE))
            acc_ref[h, rows(i), :] = _dot(p.astype(BF16), kv[c // 2][1][h])
            bias = jnp.where((spare[h] >= 0) & (spare[h] < nb) & (picked[c] < 0.5), MASK_VALUE, 0.0)
            qa_ref[h, rows(i), :] = (qh[c] + bias).astype(BF16)
        return carry

    lax.fori_loop(0, nb // dg, diag_body, 0)

    def q_tile(t, kp, vp):
        rs = pl.ds(t * qt, qt)
        s = [_dot(qa_ref[h, rs, :], kp[h], "nt") for h in range(2)]
        for h in range(2):
            m_prev = m_ref[h, rs, :]
            m_new = jnp.maximum(m_prev, jnp.max(s[h], axis=1, keepdims=True))
            p = jnp.exp2(s[h] - jnp.concatenate([m_new, m_new], axis=1))
            m_ref[h, rs, :] = m_new
            acc_ref[h, rs, :] = (acc_ref[h, rs, :] * jnp.exp2(m_prev - m_new)
                                 + _dot(p.astype(BF16), vp[h]))

    for t0 in range(nt):
        def key_body(j, carry, t0=t0):
            kp, vp = kv_tiles(j)
            for t in range(t0, nt):
                q_tile(t, kp, vp)
            return carry

        lax.fori_loop(max(0, t0 * qb - 1), min(nb - 1, (t0 + 1) * qb - 1), key_body, 0)

    def out_body(t, carry):
        rs = pl.ds(pl.multiple_of(t * qt, qt), qt)
        a0 = acc_ref[0, rs, :]
        a1 = acc_ref[1, rs, :]
        first = lax.broadcasted_iota(jnp.int32, (qt, LANE), 1) < HEAD_DIM
        out = jnp.where(first, a0 / pltpu.roll(a0, HEAD_DIM, 1), a1 / pltpu.roll(a1, HEAD_DIM, 1))
        o_ref[rs, :] = out.astype(o_ref.dtype)
        return carry

    lax.fori_loop(0, nt, out_body, 0)


def _moba(proj, b, s):
    t = proj.shape[0]
    nb = s // MOBA_BLOCK
    n_sel = min(MOBA_TOPK, nb - 1)
    nhp = 512 // LANE
    qc, kc, vc = COL_MOBA // LANE, (COL_MOBA + 512) // LANE, (COL_MOBA + 1024) // LANE
    return pl.pallas_call(
        functools.partial(_moba_body, nb, n_sel),
        grid=(b, nhp),
        in_specs=[
            pl.BlockSpec((s, LANE), lambda bi, hp: (bi, qc + hp)),
            pl.BlockSpec((s, LANE), lambda bi, hp: (bi, kc + hp)),
            pl.BlockSpec((s, LANE), lambda bi, hp: (bi, vc + hp)),
        ],
        out_specs=pl.BlockSpec((s, LANE), lambda bi, hp: (bi, hp)),
        out_shape=jax.ShapeDtypeStruct((t, 512), BF16),
        scratch_shapes=[pltpu.VMEM((2, s, LANE), F32), pltpu.VMEM((2, s, LANE), F32),
                        pltpu.VMEM((2, s, LANE), BF16)],
        compiler_params=_cparams(("parallel", "parallel")),
        name="moba_attn",
    )(proj, proj, proj)


def _mlstm_body(qk_ref, v_ref, o_ref, gate_ref, cw_ref, cb_ref, gbias_ref, hng_ref,
                y_ref, cst_ref, n_ref, m_ref, prev_ref):
    c = CHUNK
    n = N_HEADS * c

    @pl.when(pl.program_id(1) == 0)
    def _():
        cst_ref[...] = jnp.zeros_like(cst_ref)
        n_ref[...] = jnp.zeros_like(n_ref)
        m_ref[...] = jnp.zeros_like(m_ref)
        prev_ref[...] = jnp.zeros_like(prev_ref)

    x = qk_ref[...].astype(F32)
    prev8 = prev_ref[...]
    cw = cw_ref[...]
    lag = lambda j: _per(lambda xx, p8: _lagged(xx, p8, j), x, prev8)
    conv = cb_ref[...] + cw[3:4] * x + cw[2:3] * lag(1) + cw[1:2] * lag(2) + cw[0:1] * lag(3)
    prev_ref[...] = x[:, c - SUBLANE:, :]
    qk = conv * jax.nn.sigmoid(conv)
    q = qk[:, :, 0:RW]
    k = qk[:, :, RW:2 * RW] * (HEAD_DIM ** -0.5)
    v = v_ref[...].astype(F32)

    same, _, _, _ = _head_masks(n)
    hm = same.astype(F32)
    hm16 = same.astype(BF16)
    k_mt = _tile_heads(k.astype(BF16)) * hm16
    v_mt = _tile_heads(v.astype(BF16)) * hm16

    g = gate_ref[...].astype(F32) + gbias_ref[...]
    glane = lax.broadcasted_iota(jnp.int32, (c, LANE), 1)
    lg = jnp.where(glane < N_HEADS, g, -_softplus(-g))
    ti = lax.broadcasted_iota(jnp.int32, (c, c), 0)
    tj = lax.broadcasted_iota(jnp.int32, (c, c), 1)
    tril = (ti >= tj).astype(F32)
    cum = _per(lambda z: _mm2l(tril, z), lg)
    src = lax.broadcasted_iota(jnp.int32, (LANE, n), 0)
    dst_head = lax.broadcasted_iota(jnp.int32, (LANE, n), 1) >> 6
    li_e = _rows(_mm2r, lg, (src == dst_head).astype(F32))
    b_e = _rows(_mm2r, cum, (src == dst_head + N_HEADS).astype(F32))
    pos = lax.broadcasted_iota(jnp.int32, (c, n), 0)
    lane_pos = lax.broadcasted_iota(jnp.int32, (c, n), 1) & 63
    lane_head = lax.broadcasted_iota(jnp.int32, (c, n), 1) >> 6
    li_row = jnp.sum(jnp.where(pos == lane_pos, li_e, 0.0), axis=1, keepdims=True)
    b_row = jnp.sum(jnp.where(pos == lane_pos, b_e, 0.0), axis=1, keepdims=True)
    bend = b_e[:, c - 1:c, :]

    m_row = m_ref[...]
    dmat = jnp.where(pos >= lane_pos, b_e - b_row + li_row, -jnp.inf)
    inter = b_e + m_row
    m_t = inter
    for h in range(N_HEADS):
        in_head = lane_head == h
        mh = jnp.max(jnp.where(in_head, dmat, -jnp.inf), axis=2, keepdims=True)
        m_t = jnp.where(in_head, jnp.maximum(m_t, mh), m_t)
    wts = jnp.exp(dmat - m_t)
    s_inter = jnp.exp(inter - m_t)
    cst = cst_ref[...]
    n_row = n_ref[...]
    qk_w = _bmm1(q, k_mt, "nt") * wts
    num = s_inter * _bmm1(q, cst, "nt") + _bmm1(qk_w, v_mt)
    den = _rows(_mm2r, s_inter * (q * n_row) + qk_w, hm)
    y = num / jnp.maximum(jnp.abs(den), jnp.exp(-m_t))

    g_e = bend - b_e + li_e
    m_new = jnp.maximum(bend + m_row, jnp.max(g_e, axis=1, keepdims=True))
    w_e = jnp.exp(g_e - m_new)
    scale = jnp.exp(bend + m_row - m_new)
    cst_ref[...] = scale * cst + jnp.where(same, _bmm1(v * w_e, k, "tn"), 0.0)
    n_ref[...] = scale * n_row + jnp.sum(k * w_e, axis=1, keepdims=True)
    m_ref[...] = m_new

    mean = _rows(_mm2r, y, hm) * (1.0 / HEAD_DIM)
    d = y - mean
    var = _rows(_mm2r, d * d, hm) * (1.0 / HEAD_DIM)
    y = d * lax.rsqrt(var + NORM_EPS) * hng_ref[...] * jax.nn.sigmoid(o_ref[...].astype(F32))
    y_ref[...] = y.astype(y_ref.dtype)


def _mlstm(proj3, conv_w, conv_b, i_b, f_b, hn_g):
    b, s, _ = proj3.shape
    c = CHUNK
    nc = s // c
    n = N_HEADS * c
    bt = _batch_tile(b)
    gbias = jnp.concatenate([i_b, f_b, jnp.zeros((LANE - 2 * N_HEADS,), F32)]).reshape(1, LANE)
    row = lambda bi, ci: (bi, ci, 0)
    const = lambda bi, ci: (0, 0)
    return pl.pallas_call(
        _mlstm_body,
        grid=(b // bt, nc),
        in_specs=[
            pl.BlockSpec((bt, c, 2 * RW), lambda bi, ci: (bi, ci, COL_MLSTM_QK // (2 * RW))),
            pl.BlockSpec((bt, c, RW), lambda bi, ci: (bi, ci, COL_MLSTM_V // RW)),
            pl.BlockSpec((bt, c, RW), lambda bi, ci: (bi, ci, COL_MLSTM_O // RW)),
            pl.BlockSpec((bt, c, LANE), lambda bi, ci: (bi, ci, COL_MLSTM_G // LANE)),
            pl.BlockSpec(conv_w.shape, const),
            pl.BlockSpec((1, 2 * RW), const),
            pl.BlockSpec((1, LANE), const),
            pl.BlockSpec((1, RW), const),
        ],
        out_specs=pl.BlockSpec((bt, c, RW), row),
        out_shape=jax.ShapeDtypeStruct((b, s, RW), BF16),
        scratch_shapes=[pltpu.VMEM((bt, n, n), F32), pltpu.VMEM((bt, 1, n), F32),
                        pltpu.VMEM((bt, 1, n), F32), pltpu.VMEM((bt, SUBLANE, 2 * RW), F32)],
        compiler_params=_cparams(("parallel", "arbitrary")),
        name="mlstm_chunk",
    )(proj3, proj3, proj3, proj3, conv_w, conv_b.reshape(1, -1), gbias,
      hn_g.reshape(1, -1))


def _merge_body(x_ref, ya_ref, yb_ref, yc_ref, ga_ref, gb_ref, gc_ref, wa_ref, wb_ref, wc_ref,
                wo_ref, g_ref, o_ref):
    merged = (jax.nn.sigmoid(ga_ref[...].astype(F32)) * _mm1(ya_ref[...], wa_ref[...])
              + jax.nn.sigmoid(gb_ref[...].astype(F32)) * _mm1(yb_ref[...], wb_ref[...])
              + jax.nn.sigmoid(gc_ref[...].astype(F32)) * _mm1(yc_ref[...], wc_ref[...]))
    o_ref[...] = x_ref[...] + _rms(_mm1(merged, wo_ref[...]), g_ref[...])


def _merge(x2d, proj, ya, yb, yc, wa, wb, wc, wo, g, tm):
    t, d = x2d.shape
    row = lambda i: (i, 0)
    const = lambda i: (0, 0)
    return pl.pallas_call(
        _merge_body,
        grid=(t // tm,),
        in_specs=[
            pl.BlockSpec((tm, d), row),
            pl.BlockSpec((tm, ya.shape[1]), row),
            pl.BlockSpec((tm, yb.shape[1]), row),
            pl.BlockSpec((tm, yc.shape[1]), row),
            pl.BlockSpec((tm, d), lambda i: (i, 0)),
            pl.BlockSpec((tm, d), lambda i: (i, 1)),
            pl.BlockSpec((tm, d), lambda i: (i, 2)),
            pl.BlockSpec(wa.shape, const),
            pl.BlockSpec(wb.shape, const),
            pl.BlockSpec(wc.shape, const),
            pl.BlockSpec(wo.shape, const),
            pl.BlockSpec((1, d), const),
        ],
        out_specs=pl.BlockSpec((tm, d), row),
        out_shape=jax.ShapeDtypeStruct((t, d), F32),
        compiler_params=_cparams(("parallel",)),
        name="merge_out",
    )(x2d, ya, yb, yc, proj, proj, proj, wa, wb, wc, wo, g.reshape(1, d))


FFN_ROW_GROUPS = 2


def _ffn_body(nt_seq, n_ff, x_ref, g1_ref, upg_ref, upv_ref, cwg_ref, cwv_ref, cbg_ref, cbv_ref,
              down_ref, g2_ref, g3_ref, pg_ref, pp_ref, p_ref, o_ref, h_ref, acc_ref, ugp_ref, uvp_ref):
    i = pl.program_id(0)
    j = pl.program_id(1)
    tm = x_ref.shape[0]

    @pl.when(j == 0)
    def _():
        h_ref[...] = _rms(x_ref[...], g1_ref[...]).astype(BF16)
        acc_ref[...] = jnp.zeros_like(acc_ref)

    @pl.when(i % nt_seq == 0)
    def _():
        ugp_ref[j] = jnp.zeros(ugp_ref.shape[1:], F32)
        uvp_ref[j] = jnp.zeros(uvp_ref.shape[1:], F32)

    ngrp = FFN_ROW_GROUPS if tm % (FFN_ROW_GROUPS * SUBLANE) == 0 else 1
    rg = tm // ngrp
    upg = upg_ref[...]
    upv = upv_ref[...]
    hs = [h_ref[r * rg:(r + 1) * rg, :] for r in range(ngrp)]
    ug = [_dot(hr, upg) for hr in hs]
    uv = [_dot(hr, upv) for hr in hs]
    pg = [ugp_ref[j]] + [ug[r][rg - SUBLANE:, :] for r in range(ngrp - 1)]
    pv = [uvp_ref[j]] + [uv[r][rg - SUBLANE:, :] for r in range(ngrp - 1)]
    ugp_ref[j] = ug[-1][rg - SUBLANE:, :]
    uvp_ref[j] = uv[-1][rg - SUBLANE:, :]
    cwg = cwg_ref[...]
    cwv = cwv_ref[...]
    down = down_ref[...]
    for r in range(ngrp):
        cg = (cbg_ref[...] + cwg[2:3] * ug[r] + cwg[1:2] * _lagged(ug[r], pg[r], 1)
              + cwg[0:1] * _lagged(ug[r], pg[r], 2))
        cv = (cbv_ref[...] + cwv[2:3] * uv[r] + cwv[1:2] * _lagged(uv[r], pv[r], 1)
              + cwv[0:1] * _lagged(uv[r], pv[r], 2))
        act = jax.nn.gelu(cg, approximate=True) * cv
        acc_ref[r * rg:(r + 1) * rg, :] += _mm1(act, down)

    @pl.when(j == n_ff - 1)
    def _():
        emb = _mm1(p_ref[...], pp_ref[...])
        x2 = x_ref[...] + _rms(acc_ref[...], g2_ref[...])
        gate = jax.nn.sigmoid(_mm1(_rms(x2, g3_ref[...]), pg_ref[...]))
        o_ref[...] = x2 + gate * emb


def _ffn(x2d, p2d, s, g1, up, cw, cb, down, g2, g3, pgate, pproj, tm, tf):
    t, d = x2d.shape
    dff = down.shape[0]
    n_ff = dff // tf
    nt_seq = s // tm
    ple = p2d.shape[1]
    row = lambda i, j: (i, 0)
    const = lambda i, j: (0, 0)
    cb2 = cb.reshape(1, -1)
    return pl.pallas_call(
        functools.partial(_ffn_body, nt_seq, n_ff),
        grid=(t // tm, n_ff),
        in_specs=[
            pl.BlockSpec((tm, d), row),
            pl.BlockSpec((1, d), const),
            pl.BlockSpec((d, tf), lambda i, j: (0, j)),
            pl.BlockSpec((d, tf), lambda i, j: (0, n_ff + j)),
            pl.BlockSpec((cw.shape[0], tf), lambda i, j: (0, j)),
            pl.BlockSpec((cw.shape[0], tf), lambda i, j: (0, n_ff + j)),
            pl.BlockSpec((1, tf), lambda i, j: (0, j)),
            pl.BlockSpec((1, tf), lambda i, j: (0, n_ff + j)),
            pl.BlockSpec((tf, d), lambda i, j: (j, 0)),
            pl.BlockSpec((1, d), const),
            pl.BlockSpec((1, d), const),
            pl.BlockSpec(pgate.shape, const),
            pl.BlockSpec(pproj.shape, const),
            pl.BlockSpec((tm, ple), row),
        ],
        out_specs=pl.BlockSpec((tm, d), row),
        out_shape=jax.ShapeDtypeStruct((t, d), F32),
        scratch_shapes=[pltpu.VMEM((tm, d), BF16), pltpu.VMEM((tm, d), F32),
                        pltpu.VMEM((n_ff, SUBLANE, tf), F32), pltpu.VMEM((n_ff, SUBLANE, tf), F32)],
        compiler_params=_cparams(("arbitrary", "arbitrary")),
        name="ffn_ple",
    )(x2d, g1.reshape(1, d), up, up, cw, cw, cb2, cb2, down, g2.reshape(1, d), g3.reshape(1, d),
      pgate, pproj, p2d)


def _pack_w_in(w):
    d = w.shape[0]
    rwkv = w[:, 0:1024]
    moba = w[:, 1024:2560]
    ml = w[:, 2560:3592]
    gate = w[:, 3592:6664]
    pad = jnp.zeros((d, PACKED_WIDTH - COL_MLSTM_G - 2 * N_HEADS), BF16)
    return jnp.concatenate([gate.astype(BF16), rwkv.astype(BF16), moba.astype(BF16), ml.astype(BF16), pad],
                           axis=1)


def _row_tile(t, want):
    return want if t % want == 0 else t


def kernel(x, p, ln_mix_pre, ln_mix_post, ln_ffn_pre, ln_ffn_post, ln_ple, w_in, rwkv_mu, rwkv_w0, rwkv_w2, rwkv_a0, rwkv_a2, rwkv_g2, rwkv_k_k, rwkv_k_a, rwkv_r_k, rwkv_gn_g, rwkv_gn_b, rwkv_v0, rwkv_v1, rwkv_v2, mlstm_conv_w, mlstm_conv_b, mlstm_i_b, mlstm_f_b, mlstm_hn_g, w_br_rwkv, w_br_moba, w_br_mlstm, w_out, ffn_up, ffn_conv_w, ffn_conv_b, ffn_down, ple_proj, ple_gate):
    b, s, d = x.shape
    depth = w_in.shape[0]
    t = b * s
    assert d == 1024 and s % MOBA_BLOCK == 0 and w_in.shape[2] == 6664
    xf = x.reshape(t, d)
    tm_proj = _row_tile(t, 2048)
    tm_ffn = min(512, s)
    tm_merge = _row_tile(t, 1024)
    r2 = lambda a: a.reshape(1, -1)
    v_first = None
    for i in range(depth):
        proj = _norm_proj(xf, ln_mix_pre[i], _pack_w_in(w_in[i]), tm_proj, 1024)
        params = [r2(rwkv_mu[i]), r2(rwkv_w0[i]), rwkv_w2[i], r2(rwkv_a0[i]), rwkv_a2[i], rwkv_g2[i],
                  r2(rwkv_k_k[i]), r2(rwkv_k_a[i]), r2(rwkv_r_k[i]), r2(rwkv_gn_g[i]), r2(rwkv_gn_b[i])]
        if i > 0:
            params += [r2(rwkv_v0[i - 1]), rwkv_v1[i - 1], rwkv_v2[i - 1]]
        proj3 = proj.reshape(b, s, -1)
        y_a, v_cur = _rwkv(proj3, params, v_first if i > 0 else None)
        if i == 0:
            v_first = v_cur
        y_b = _moba(proj, b, s)
        y_c = _mlstm(proj3, mlstm_conv_w[i], mlstm_conv_b[i], mlstm_i_b[i], mlstm_f_b[i], mlstm_hn_g[i])
        xf = _merge(xf, proj, y_a.reshape(t, -1), y_b, y_c.reshape(t, -1), w_br_rwkv[i].astype(BF16), w_br_moba[i].astype(BF16),
                    w_br_mlstm[i].astype(BF16), w_out[i].astype(BF16), ln_mix_post[i], tm_merge)
        xf = _ffn(xf, p[i].reshape(t, -1), s, ln_ffn_pre[i], ffn_up[i].astype(BF16), ffn_conv_w[i],
                  ffn_conv_b[i], ffn_down[i].astype(BF16), ln_ffn_post[i], ln_ple[i],
                  ple_gate[i].astype(BF16), ple_proj[i].astype(BF16), tm_ffn, 1408)
    return xf.reshape(b, s, d)
```

```python
import functools

import jax
import jax.numpy as jnp
from jax import lax
from jax.experimental import pallas as pl
from jax.experimental.pallas import tpu as pltpu

F32 = jnp.float32
BF16 = jnp.bfloat16

HEAD_DIM = 64
N_HEADS = 4
RW = N_HEADS * HEAD_DIM
BATCH_TILE = 8
CHUNK = HEAD_DIM
HEAD_SHIFT = HEAD_DIM.bit_length() - 1
MOBA_BLOCK = 256
MOBA_TOPK = 3
MOBA_QTILE_BLOCKS = 4
NORM_EPS = 1e-6
RWKV_GN_EPS = 64e-5
MASK_VALUE = -1e30
LOG2E = 1.4426950408889634
LANE = 128
SUBLANE = 8
VMEM_LIMIT = 56 * 1024 * 1024

COL_RWKV = 3072
COL_MOBA = 4096
COL_MLSTM_QK = 5632
COL_MLSTM_V = 6144
COL_MLSTM_O = 6400
COL_MLSTM_G = 6656
PACKED_WIDTH = 7168

_DIMS = {
    "nn": (((1,), (0,)), ((), ())),
    "nt": (((1,), (1,)), ((), ())),
    "tn": (((0,), (0,)), ((), ())),
}


def _dot(a, b, dims="nn"):
    return lax.dot_general(a, b, _DIMS[dims], preferred_element_type=F32)


def _split(a):
    hi = a.astype(BF16)
    lo = (a - hi.astype(F32)).astype(BF16)
    return hi, lo


def _mm1(a, b, dims="nn"):
    return _dot(a.astype(BF16), b.astype(BF16), dims)


def _mm3(a, b, dims="nn"):
    ah, al = _split(a)
    bh, bl = _split(b)
    return _dot(ah, bh, dims) + (_dot(ah, bl, dims) + _dot(al, bh, dims))


def _mm2r(a, e, dims="nn"):
    ah, al = _split(a)
    eb = e.astype(BF16)
    return _dot(ah, eb, dims) + _dot(al, eb, dims)


def _mm2l(e, a, dims="nn"):
    ah, al = _split(a)
    eb = e.astype(BF16)
    return _dot(eb, ah, dims) + _dot(eb, al, dims)


def _softplus(x):
    return jnp.maximum(x, 0.0) + jnp.log(1.0 + jnp.exp(-jnp.abs(x)))


def _rms(x, g):
    ms = jnp.mean(x * x, axis=-1, keepdims=True)
    return x * lax.rsqrt(ms + NORM_EPS) * g


def _lagged(x, prev8, lag):
    full = jnp.concatenate([prev8, x], axis=0)
    return pltpu.roll(full, lag, 0)[SUBLANE:, :]


def _tile_heads(x):
    return jnp.concatenate([x] * N_HEADS, axis=-2)


def _per(fn, *arrs):
    return jnp.stack([fn(*(a[i] for a in arrs)) for i in range(arrs[0].shape[0])])


def _rows(fn, x, w):
    lead = x.shape[:-1]
    return fn(x.reshape(-1, x.shape[-1]), w).reshape(*lead, -1)


def _bmm1(a, b, dims="nn"):
    return _per(lambda x, y: _mm1(x, y, dims), a, b)


def _batch_tile(b):
    return BATCH_TILE if b % BATCH_TILE == 0 else 1


def _same_head(n):
    ri = lax.broadcasted_iota(jnp.int32, (n, n), 0)
    ci = lax.broadcasted_iota(jnp.int32, (n, n), 1)
    return (ri >> HEAD_SHIFT) == (ci >> HEAD_SHIFT)


def _lane_head_pos(rows, n):
    lane = lax.broadcasted_iota(jnp.int32, (rows, n), 1)
    return lane >> HEAD_SHIFT, lane & (HEAD_DIM - 1)


def _cparams(sem):
    return pltpu.CompilerParams(dimension_semantics=sem, vmem_limit_bytes=VMEM_LIMIT)


def _proj_body(x_ref, g_ref, w_ref, o_ref, h_ref):
    @pl.when(pl.program_id(1) == 0)
    def _():
        h_ref[...] = _rms(x_ref[...], g_ref[...]).astype(BF16)

    o_ref[...] = _dot(h_ref[...], w_ref[...]).astype(o_ref.dtype)


def _norm_proj(x2d, g, w, tm, tn):
    t, d = x2d.shape
    n = w.shape[1]
    return pl.pallas_call(
        _proj_body,
        grid=(t // tm, n // tn),
        in_specs=[
            pl.BlockSpec((tm, d), lambda i, j: (i, 0)),
            pl.BlockSpec((1, d), lambda i, j: (0, 0)),
            pl.BlockSpec((d, tn), lambda i, j: (0, j)),
        ],
        out_specs=pl.BlockSpec((tm, tn), lambda i, j: (i, j)),
        out_shape=jax.ShapeDtypeStruct((t, n), BF16),
        scratch_shapes=[pltpu.VMEM((tm, d), BF16)],
        compiler_params=_cparams(("parallel", "arbitrary")),
        name="norm_proj",
    )(x2d, g.reshape(1, d), w)


def _rwkv_body(has_vres, *refs):
    if has_vres:
        (slab_ref, vf_ref, mu_ref, w0_ref, w2_ref, a0_ref, a2_ref, g2_ref, kk_ref, ka_ref,
         rk_ref, gg_ref, gb_ref, v0_ref, v1_ref, v2_ref, y_ref, vo_ref, st_ref, prev_ref) = refs
    else:
        (slab_ref, mu_ref, w0_ref, w2_ref, a0_ref, a2_ref, g2_ref, kk_ref, ka_ref,
         rk_ref, gg_ref, gb_ref, y_ref, vo_ref, st_ref, prev_ref) = refs
    c = CHUNK

    @pl.when(pl.program_id(1) == 0)
    def _():
        st_ref[...] = jnp.zeros_like(st_ref)
        prev_ref[...] = jnp.zeros_like(prev_ref)

    slab = slab_ref[...].astype(F32)
    shifted = _per(lambda x, p8: _lagged(x, p8, 1), slab, prev_ref[...])
    prev_ref[...] = slab[:, c - SUBLANE:, :]
    xs = slab + mu_ref[...] * (shifted - slab)
    r = xs[:, :, 0:RW]
    k = xs[:, :, RW:2 * RW]
    v = xs[:, :, 2 * RW:3 * RW]
    o_w = 3 * RW
    o_a = o_w + w2_ref.shape[0]
    o_g = o_a + a2_ref.shape[0]
    xw = xs[:, :, o_w:o_a]
    xa = xs[:, :, o_a:o_g]
    xg = xs[:, :, o_g:o_g + g2_ref.shape[0]]
    wlog = -_softplus(-(w0_ref[...] + _rows(_mm3, jnp.tanh(xw), w2_ref[...]))) - 0.5
    alr = jax.nn.sigmoid(a0_ref[...] + _rows(_mm3, xa, a2_ref[...]))
    g = _rows(_mm3, jax.nn.sigmoid(xg), g2_ref[...])
    if has_vres:
        mix = jax.nn.sigmoid(v0_ref[...] + _rows(_mm3, _rows(_mm3, v, v1_ref[...]), v2_ref[...]))
        v = v + (vf_ref[...] - v) * mix
    vo_ref[...] = v

    n = N_HEADS * c
    same = _same_head(n)
    hm = same.astype(F32)
    pos = lax.broadcasted_iota(jnp.int32, (c, n), 0)
    _, lane_pos = _lane_head_pos(c, n)

    kk = k * kk_ref[...]
    kk = kk / jnp.maximum(jnp.sqrt(_rows(_mm2r, kk * kk, hm)), 1e-12)
    k2 = k * (1.0 + (alr - 1.0) * ka_ref[...])

    lw = -jnp.exp(wlog)
    ti = lax.broadcasted_iota(jnp.int32, (c, c), 0)
    tj = lax.broadcasted_iota(jnp.int32, (c, c), 1)
    tril = (ti >= tj).astype(F32)
    cs = _per(lambda x: _mm2l(tril, x), lw)
    cs_end = cs[:, c - 1:c, :]
    p_in = jnp.exp(cs)
    p_ex = jnp.exp(cs - lw)
    p_inv = jnp.exp(-cs)
    p_tail = jnp.exp(cs_end - cs)

    hm16 = same.astype(BF16)

    def expand(x):
        return _tile_heads(x.astype(BF16)) * hm16

    a_c = -kk * p_ex
    r_c = r * p_in
    b_bd = expand(kk * alr * p_inv)
    k_bd = expand(k2 * p_inv)
    v_bd = expand(v)
    strict = pos > lane_pos
    incl = pos >= lane_pos
    l_ab = jnp.where(strict, _bmm1(a_c, b_bd, "nt"), 0.0)
    l_ak = jnp.where(strict, _bmm1(a_c, k_bd, "nt"), 0.0)
    m_rb = jnp.where(incl, _bmm1(r_c, b_bd, "nt"), 0.0)
    m_rk = jnp.where(incl, _bmm1(r_c, k_bd, "nt"), 0.0)

    tinv = (pos == lane_pos).astype(F32) + l_ab
    npow = l_ab
    for _ in range(5):
        npow = _bmm1(npow, expand(npow))
        tinv = tinv + _bmm1(tinv, expand(npow))

    st = st_ref[...]
    u = _bmm1(tinv, expand(_bmm1(a_c, st, "nt") + _bmm1(l_ak, v_bd)))
    y = _bmm1(r_c, st, "nt") + _bmm1(m_rb, expand(u)) + _bmm1(m_rk, v_bd)
    upd = _bmm1(u, kk * alr * p_tail, "tn") + _bmm1(v, k2 * p_tail, "tn")
    st_ref[...] = st * p_in[:, c - 1:c, :] + jnp.where(same, upd, 0.0)

    mean = _rows(_mm2r, y, hm) * (1.0 / HEAD_DIM)
    d = y - mean
    var = _rows(_mm2r, d * d, hm) * (1.0 / HEAD_DIM)
    yn = d * lax.rsqrt(var + RWKV_GN_EPS) * gg_ref[...] + gb_ref[...]
    bonus = _rows(_mm2r, r * k2 * rk_ref[...], hm) * v
    y_ref[...] = ((yn + bonus) * g).astype(y_ref.dtype)


def _rwkv(proj3, params, v_first3):
    b, s, _ = proj3.shape
    nc = s // CHUNK
    bt = _batch_tile(b)
    has_vres = v_first3 is not None
    row = lambda bi, ci: (bi, ci, 0)
    const = lambda bi, ci: (0, 0)
    in_specs = [pl.BlockSpec((bt, CHUNK, 1024), lambda bi, ci: (bi, ci, COL_RWKV // 1024))]
    args = [proj3]
    if has_vres:
        in_specs.append(pl.BlockSpec((bt, CHUNK, RW), row))
        args.append(v_first3)
    for prm in params:
        in_specs.append(pl.BlockSpec(prm.shape, const))
        args.append(prm)
    n = N_HEADS * CHUNK
    return pl.pallas_call(
        functools.partial(_rwkv_body, has_vres),
        grid=(b // bt, nc),
        in_specs=in_specs,
        out_specs=[pl.BlockSpec((bt, CHUNK, RW), row), pl.BlockSpec((bt, CHUNK, RW), row)],
        out_shape=[jax.ShapeDtypeStruct((b, s, RW), BF16), jax.ShapeDtypeStruct((b, s, RW), F32)],
        scratch_shapes=[pltpu.VMEM((bt, n, n), F32), pltpu.VMEM((bt, SUBLANE, 1024), F32)],
        compiler_params=_cparams(("parallel", "arbitrary")),
        name="rwkv7_chunk",
    )(*args)


def _moba_body(nb, n_sel, q_ref, k_ref, v_ref, o_ref, m_ref, acc_ref, qa_ref):
    bl = MOBA_BLOCK
    qb = MOBA_QTILE_BLOCKS if nb % MOBA_QTILE_BLOCKS == 0 else 1
    qt = qb * bl
    nt = nb // qb
    nbp =-(-nb // SUBLANE) * SUBLANE
    scale = HEAD_DIM ** -0.5
    lane = lax.broadcasted_iota(jnp.int32, (bl, LANE), 1)
    head0 = lane < HEAD_DIM
    hmask = (head0, jnp.logical_not(head0))
    spare = (lane - HEAD_DIM, lane)
    blk = lax.broadcasted_iota(jnp.int32, (nbp, bl), 0)
    qpos = lax.broadcasted_iota(jnp.int32, (bl, bl), 0)
    kpos = lax.broadcasted_iota(jnp.int32, (bl, bl), 1)
    causal = kpos <= qpos
    e_row = lax.broadcasted_iota(jnp.int32, (nbp, LANE), 0)
    e_lane = lax.broadcasted_iota(jnp.int32, (nbp, LANE), 1)
    place = ((e_lane == e_row + HEAD_DIM).astype(BF16), (e_lane == e_row).astype(BF16))
    klane = e_lane < HEAD_DIM

    kmean = jnp.mean(k_ref[...].astype(F32).reshape(nb, bl, LANE), axis=1)
    if nbp > nb:
        kmean = jnp.concatenate([kmean, jnp.zeros((nbp - nb, LANE), F32)], axis=0)
    kmean_h = (jnp.where(klane, kmean, 0.0), jnp.where(klane, 0.0, kmean))

    def rows(i):
        if isinstance(i, int):
            return pl.ds(i * bl, bl)
        return pl.ds(pl.multiple_of(i * bl, bl), bl)

    def kv_tiles(j):
        kb = k_ref[rows(j), :].astype(F32) * (scale * LOG2E)
        vb = v_ref[rows(j), :]
        kp = [jnp.where(hmask[h], kb, (spare[h] == j).astype(F32)).astype(BF16) for h in range(2)]
        vp = [jnp.where(hmask[h], vb, 1.0).astype(BF16) for h in range(2)]
        return kp, vp

    dg = qb if qb > 1 else (2 if nb % 2 == 0 else 1)

    def diag_body(ii, carry):
        chains = [(ii * dg + g, h) for g in range(dg) for h in range(2)]
        qf = {g: q_ref[rows(ii * dg + g), :].astype(F32) for g in range(dg)}
        kv = {g: kv_tiles(ii * dg + g) for g in range(dg)}
        qh = [jnp.where(hmask[h], qf[g], 0.0) for g in range(dg) for h in range(2)]
        s = [_dot(qh[c].astype(BF16), kv[c // 2][0][h], "nt") for c, (_, h) in enumerate(chains)]
        bs = [jnp.where(blk < i, _mm3(kmean_h[h], qf[c // 2], "nt"), MASK_VALUE)
              for c, (i, h) in enumerate(chains)]
        sel_t = []
        for c, (i, h) in enumerate(chains):
            rank = jnp.zeros((nbp, bl), jnp.int32)
            for jp in range(nb):
                row = bs[c][jp:jp + 1, :]
                beats = (row > bs[c]) | ((row == bs[c]) & (jp < blk))
                rank = rank + beats.astype(jnp.int32)
            sel_t.append(((rank < n_sel) & (blk < i)).astype(BF16))
        picked = [_dot(sel_t[c], place[h], "tn") for c, (_, h) in enumerate(chains)]
        for c, (i, h) in enumerate(chains):
            sm = jnp.where(causal, s[c], MASK_VALUE)
            m = jnp.max(sm, axis=1, keepdims=True)
            p = jnp.exp2(sm - m)
            m_ref[h, rows(i), :] = jnp.broadcast_to(m, (bl, LANE))
            acc_ref[h, rows(i), :] = _dot(p.astype(BF16), kv[c // 2][1][h])
            bias = jnp.where((spare[h] >= 0) & (spare[h] < nb) & (picked[c] < 0.5), MASK_VALUE, 0.0)
            qa_ref[h, rows(i), :] = (qh[c] + bias).astype(BF16)
        return carry

    lax.fori_loop(0, nb // dg, diag_body, 0)

    def q_tile(t, kp, vp):
        rs = pl.ds(t * qt, qt)
        s = [_dot(qa_ref[h, rs, :], kp[h], "nt") for h in range(2)]
        for h in range(2):
            m_prev = m_ref[h, rs, :]
            m_new = jnp.maximum(m_prev, jnp.max(s[h], axis=1, keepdims=True))
            p = jnp.exp2(s[h] - jnp.concatenate([m_new, m_new], axis=1))
            m_ref[h, rs, :] = m_new
            acc_ref[h, rs, :] = (acc_ref[h, rs, :] * jnp.exp2(m_prev - m_new)
                                 + _dot(p.astype(BF16), vp[h]))

    for t0 in range(nt):
        def key_body(j, carry, t0=t0):
            kp, vp = kv_tiles(j)
            for t in range(t0, nt):
                q_tile(t, kp, vp)
            return carry

        lax.fori_loop(max(0, t0 * qb - 1), min(nb - 1, (t0 + 1) * qb - 1), key_body, 0)

    def out_body(t, carry):
        rs = pl.ds(pl.multiple_of(t * qt, qt), qt)
        a0 = acc_ref[0, rs, :]
        a1 = acc_ref[1, rs, :]
        first = lax.broadcasted_iota(jnp.int32, (qt, LANE), 1) < HEAD_DIM
        out = jnp.where(first, a0 / pltpu.roll(a0, HEAD_DIM, 1), a1 / pltpu.roll(a1, HEAD_DIM, 1))
        o_ref[rs, :] = out.astype(o_ref.dtype)
        return carry

    lax.fori_loop(0, nt, out_body, 0)


def _moba(proj, b, s):
    t = proj.shape[0]
    nb = s // MOBA_BLOCK
    n_sel = min(MOBA_TOPK, nb - 1)
    nhp = 512 // LANE
    qc, kc, vc = COL_MOBA // LANE, (COL_MOBA + 512) // LANE, (COL_MOBA + 1024) // LANE
    return pl.pallas_call(
        functools.partial(_moba_body, nb, n_sel),
        grid=(b, nhp),
        in_specs=[
            pl.BlockSpec((s, LANE), lambda bi, hp: (bi, qc + hp)),
            pl.BlockSpec((s, LANE), lambda bi, hp: (bi, kc + hp)),
            pl.BlockSpec((s, LANE), lambda bi, hp: (bi, vc + hp)),
        ],
        out_specs=pl.BlockSpec((s, LANE), lambda bi, hp: (bi, hp)),
        out_shape=jax.ShapeDtypeStruct((t, 512), BF16),
        scratch_shapes=[pltpu.VMEM((2, s, LANE), F32), pltpu.VMEM((2, s, LANE), F32),
                        pltpu.VMEM((2, s, LANE), BF16)],
        compiler_params=_cparams(("parallel", "parallel")),
        name="moba_attn",
    )(proj, proj, proj)


def _mlstm_body(qk_ref, v_ref, o_ref, gate_ref, cw_ref, cb_ref, gbias_ref, hng_ref,
                y_ref, cst_ref, n_ref, m_ref, prev_ref):
    c = CHUNK
    n = N_HEADS * c

    @pl.when(pl.program_id(1) == 0)
    def _():
        cst_ref[...] = jnp.zeros_like(cst_ref)
        n_ref[...] = jnp.zeros_like(n_ref)
        m_ref[...] = jnp.zeros_like(m_ref)
        prev_ref[...] = jnp.zeros_like(prev_ref)

    x = qk_ref[...].astype(F32)
    prev8 = prev_ref[...]
    cw = cw_ref[...]
    lag = lambda j: _per(lambda xx, p8: _lagged(xx, p8, j), x, prev8)
    conv = cb_ref[...] + cw[3:4] * x + cw[2:3] * lag(1) + cw[1:2] * lag(2) + cw[0:1] * lag(3)
    prev_ref[...] = x[:, c - SUBLANE:, :]
    qk = conv * jax.nn.sigmoid(conv)
    q = qk[:, :, 0:RW]
    k = qk[:, :, RW:2 * RW] * (HEAD_DIM ** -0.5)
    v = v_ref[...].astype(F32)

    same = _same_head(n)
    hm = same.astype(F32)
    hm16 = same.astype(BF16)
    k_mt = _tile_heads(k.astype(BF16)) * hm16
    v_mt = _tile_heads(v.astype(BF16)) * hm16

    g = gate_ref[...].astype(F32) + gbias_ref[...]
    glane = lax.broadcasted_iota(jnp.int32, (c, LANE), 1)
    lg = jnp.where(glane < N_HEADS, g, -_softplus(-g))
    ti = lax.broadcasted_iota(jnp.int32, (c, c), 0)
    tj = lax.broadcasted_iota(jnp.int32, (c, c), 1)
    tril = (ti >= tj).astype(F32)
    cum = _per(lambda z: _mm2l(tril, z), lg)
    src = lax.broadcasted_iota(jnp.int32, (LANE, n), 0)
    dst_head, _ = _lane_head_pos(LANE, n)
    li_e = _rows(_mm2r, lg, (src == dst_head).astype(F32))
    b_e = _rows(_mm2r, cum, (src == dst_head + N_HEADS).astype(F32))
    pos = lax.broadcasted_iota(jnp.int32, (c, n), 0)
    lane_head, lane_pos = _lane_head_pos(c, n)
    li_row = jnp.sum(jnp.where(pos == lane_pos, li_e, 0.0), axis=1, keepdims=True)
    b_row = jnp.sum(jnp.where(pos == lane_pos, b_e, 0.0), axis=1, keepdims=True)
    bend = b_e[:, c - 1:c, :]

    m_row = m_ref[...]
    dmat = jnp.where(pos >= lane_pos, b_e - b_row + li_row, -jnp.inf)
    inter = b_e + m_row
    m_t = inter
    for h in range(N_HEADS):
        in_head = lane_head == h
        mh = jnp.max(jnp.where(in_head, dmat, -jnp.inf), axis=2, keepdims=True)
        m_t = jnp.where(in_head, jnp.maximum(m_t, mh), m_t)
    wts = jnp.exp(dmat - m_t)
    s_inter = jnp.exp(inter - m_t)
    cst = cst_ref[...]
    n_row = n_ref[...]
    qk_w = _bmm1(q, k_mt, "nt") * wts
    num = s_inter * _bmm1(q, cst, "nt") + _bmm1(qk_w, v_mt)
    den = _rows(_mm2r, s_inter * (q * n_row) + qk_w, hm)
    y = num / jnp.maximum(jnp.abs(den), jnp.exp(-m_t))

    g_e = bend - b_e + li_e
    m_new = jnp.maximum(bend + m_row, jnp.max(g_e, axis=1, keepdims=True))
    w_e = jnp.exp(g_e - m_new)
    scale = jnp.exp(bend + m_row - m_new)
    cst_ref[...] = scale * cst + jnp.where(same, _bmm1(v * w_e, k, "tn"), 0.0)
    n_ref[...] = scale * n_row + jnp.sum(k * w_e, axis=1, keepdims=True)
    m_ref[...] = m_new

    mean = _rows(_mm2r, y, hm) * (1.0 / HEAD_DIM)
    d = y - mean
    var = _rows(_mm2r, d * d, hm) * (1.0 / HEAD_DIM)
    y = d * lax.rsqrt(var + NORM_EPS) * hng_ref[...] * jax.nn.sigmoid(o_ref[...].astype(F32))
    y_ref[...] = y.astype(y_ref.dtype)


def _mlstm(proj3, conv_w, conv_b, i_b, f_b, hn_g):
    b, s, _ = proj3.shape
    c = CHUNK
    nc = s // c
    n = N_HEADS * c
    bt = _batch_tile(b)
    gbias = jnp.concatenate([i_b, f_b, jnp.zeros((LANE - 2 * N_HEADS,), F32)]).reshape(1, LANE)
    row = lambda bi, ci: (bi, ci, 0)
    const = lambda bi, ci: (0, 0)
    return pl.pallas_call(
        _mlstm_body,
        grid=(b // bt, nc),
        in_specs=[
            pl.BlockSpec((bt, c, 2 * RW), lambda bi, ci: (bi, ci, COL_MLSTM_QK // (2 * RW))),
            pl.BlockSpec((bt, c, RW), lambda bi, ci: (bi, ci, COL_MLSTM_V // RW)),
            pl.BlockSpec((bt, c, RW), lambda bi, ci: (bi, ci, COL_MLSTM_O // RW)),
            pl.BlockSpec((bt, c, LANE), lambda bi, ci: (bi, ci, COL_MLSTM_G // LANE)),
            pl.BlockSpec(conv_w.shape, const),
            pl.BlockSpec((1, 2 * RW), const),
            pl.BlockSpec((1, LANE), const),
            pl.BlockSpec((1, RW), const),
        ],
        out_specs=pl.BlockSpec((bt, c, RW), row),
        out_shape=jax.ShapeDtypeStruct((b, s, RW), BF16),
        scratch_shapes=[pltpu.VMEM((bt, n, n), F32), pltpu.VMEM((bt, 1, n), F32),
                        pltpu.VMEM((bt, 1, n), F32), pltpu.VMEM((bt, SUBLANE, 2 * RW), F32)],
        compiler_params=_cparams(("parallel", "arbitrary")),
        name="mlstm_chunk",
    )(proj3, proj3, proj3, proj3, conv_w, conv_b.reshape(1, -1), gbias,
      hn_g.reshape(1, -1))


def _merge_body(x_ref, ya_ref, yb_ref, yc_ref, ga_ref, gb_ref, gc_ref, wa_ref, wb_ref, wc_ref,
                wo_ref, g_ref, o_ref):
    merged = (jax.nn.sigmoid(ga_ref[...].astype(F32)) * _mm1(ya_ref[...], wa_ref[...])
              + jax.nn.sigmoid(gb_ref[...].astype(F32)) * _mm1(yb_ref[...], wb_ref[...])
              + jax.nn.sigmoid(gc_ref[...].astype(F32)) * _mm1(yc_ref[...], wc_ref[...]))
    o_ref[...] = x_ref[...] + _rms(_mm1(merged, wo_ref[...]), g_ref[...])


def _merge(x2d, proj, ya, yb, yc, wa, wb, wc, wo, g, tm):
    t, d = x2d.shape
    row = lambda i: (i, 0)
    const = lambda i: (0, 0)
    return pl.pallas_call(
        _merge_body,
        grid=(t // tm,),
        in_specs=[
            pl.BlockSpec((tm, d), row),
            pl.BlockSpec((tm, ya.shape[1]), row),
            pl.BlockSpec((tm, yb.shape[1]), row),
            pl.BlockSpec((tm, yc.shape[1]), row),
            pl.BlockSpec((tm, d), lambda i: (i, 0)),
            pl.BlockSpec((tm, d), lambda i: (i, 1)),
            pl.BlockSpec((tm, d), lambda i: (i, 2)),
            pl.BlockSpec(wa.shape, const),
            pl.BlockSpec(wb.shape, const),
            pl.BlockSpec(wc.shape, const),
            pl.BlockSpec(wo.shape, const),
            pl.BlockSpec((1, d), const),
        ],
        out_specs=pl.BlockSpec((tm, d), row),
        out_shape=jax.ShapeDtypeStruct((t, d), F32),
        compiler_params=_cparams(("parallel",)),
        name="merge_out",
    )(x2d, ya, yb, yc, proj, proj, proj, wa, wb, wc, wo, g.reshape(1, d))


FFN_ROW_GROUPS = 2


def _ffn_body(nt_seq, n_ff, x_ref, g1_ref, upg_ref, upv_ref, cwg_ref, cwv_ref, cbg_ref, cbv_ref,
              down_ref, g2_ref, g3_ref, pg_ref, pp_ref, p_ref, o_ref, h_ref, acc_ref, ugp_ref, uvp_ref):
    i = pl.program_id(0)
    j = pl.program_id(1)
    tm = x_ref.shape[0]

    @pl.when(j == 0)
    def _():
        h_ref[...] = _rms(x_ref[...], g1_ref[...]).astype(BF16)
        acc_ref[...] = jnp.zeros_like(acc_ref)

    @pl.when(i % nt_seq == 0)
    def _():
        ugp_ref[j] = jnp.zeros(ugp_ref.shape[1:], F32)
        uvp_ref[j] = jnp.zeros(uvp_ref.shape[1:], F32)

    ngrp = FFN_ROW_GROUPS if tm % (FFN_ROW_GROUPS * SUBLANE) == 0 else 1
    rg = tm // ngrp
    upg = upg_ref[...]
    upv = upv_ref[...]
    hs = [h_ref[r * rg:(r + 1) * rg, :] for r in range(ngrp)]
    ug = [_dot(hr, upg) for hr in hs]
    uv = [_dot(hr, upv) for hr in hs]
    pg = [ugp_ref[j]] + [ug[r][rg - SUBLANE:, :] for r in range(ngrp - 1)]
    pv = [uvp_ref[j]] + [uv[r][rg - SUBLANE:, :] for r in range(ngrp - 1)]
    ugp_ref[j] = ug[-1][rg - SUBLANE:, :]
    uvp_ref[j] = uv[-1][rg - SUBLANE:, :]
    cwg = cwg_ref[...]
    cwv = cwv_ref[...]
    down = down_ref[...]
    for r in range(ngrp):
        cg = (cbg_ref[...] + cwg[2:3] * ug[r] + cwg[1:2] * _lagged(ug[r], pg[r], 1)
              + cwg[0:1] * _lagged(ug[r], pg[r], 2))
        cv = (cbv_ref[...] + cwv[2:3] * uv[r] + cwv[1:2] * _lagged(uv[r], pv[r], 1)
              + cwv[0:1] * _lagged(uv[r], pv[r], 2))
        act = jax.nn.gelu(cg, approximate=True) * cv
        acc_ref[r * rg:(r + 1) * rg, :] += _mm1(act, down)

    @pl.when(j == n_ff - 1)
    def _():
        x2 = x_ref[...] + _rms(acc_ref[...], g2_ref[...])
        gate = jax.nn.sigmoid(_mm1(_rms(x2, g3_ref[...]), pg_ref[...]))
        o_ref[...] = x2 + gate * _mm1(p_ref[...], pp_ref[...])


def _ffn(x2d, p2d, s, g1, up, cw, cb, down, g2, g3, pgate, pproj, tm, tf):
    t, d = x2d.shape
    dff = down.shape[0]
    n_ff = dff // tf
    nt_seq = s // tm
    ple = p2d.shape[1]
    row = lambda i, j: (i, 0)
    const = lambda i, j: (0, 0)
    cb2 = cb.reshape(1, -1)
    return pl.pallas_call(
        functools.partial(_ffn_body, nt_seq, n_ff),
        grid=(t // tm, n_ff),
        in_specs=[
            pl.BlockSpec((tm, d), row),
            pl.BlockSpec((1, d), const),
            pl.BlockSpec((d, tf), lambda i, j: (0, j)),
            pl.BlockSpec((d, tf), lambda i, j: (0, n_ff + j)),
            pl.BlockSpec((cw.shape[0], tf), lambda i, j: (0, j)),
            pl.BlockSpec((cw.shape[0], tf), lambda i, j: (0, n_ff + j)),
            pl.BlockSpec((1, tf), lambda i, j: (0, j)),
            pl.BlockSpec((1, tf), lambda i, j: (0, n_ff + j)),
            pl.BlockSpec((tf, d), lambda i, j: (j, 0)),
            pl.BlockSpec((1, d), const),
            pl.BlockSpec((1, d), const),
            pl.BlockSpec(pgate.shape, const),
            pl.BlockSpec(pproj.shape, const),
            pl.BlockSpec((tm, ple), row),
        ],
        out_specs=pl.BlockSpec((tm, d), row),
        out_shape=jax.ShapeDtypeStruct((t, d), F32),
        scratch_shapes=[pltpu.VMEM((tm, d), BF16), pltpu.VMEM((tm, d), F32),
                        pltpu.VMEM((n_ff, SUBLANE, tf), F32), pltpu.VMEM((n_ff, SUBLANE, tf), F32)],
        compiler_params=_cparams(("arbitrary", "arbitrary")),
        name="ffn_ple",
    )(x2d, g1.reshape(1, d), up, up, cw, cw, cb2, cb2, down, g2.reshape(1, d), g3.reshape(1, d),
      pgate, pproj, p2d)


def _pack_w_in(w):
    d = w.shape[0]
    rwkv = w[:, 0:1024]
    moba = w[:, 1024:2560]
    ml = w[:, 2560:3592]
    gate = w[:, 3592:6664]
    pad = jnp.zeros((d, PACKED_WIDTH - COL_MLSTM_G - 2 * N_HEADS), w.dtype)
    packed = jnp.concatenate([gate, rwkv, moba, ml, pad], axis=1)
    return packed.astype(BF16)


def _row_tile(t, want):
    return want if t % want == 0 else t


def kernel(x, p, ln_mix_pre, ln_mix_post, ln_ffn_pre, ln_ffn_post, ln_ple, w_in, rwkv_mu, rwkv_w0, rwkv_w2, rwkv_a0, rwkv_a2, rwkv_g2, rwkv_k_k, rwkv_k_a, rwkv_r_k, rwkv_gn_g, rwkv_gn_b, rwkv_v0, rwkv_v1, rwkv_v2, mlstm_conv_w, mlstm_conv_b, mlstm_i_b, mlstm_f_b, mlstm_hn_g, w_br_rwkv, w_br_moba, w_br_mlstm, w_out, ffn_up, ffn_conv_w, ffn_conv_b, ffn_down, ple_proj, ple_gate):
    b, s, d = x.shape
    depth = w_in.shape[0]
    t = b * s
    assert d == 1024 and s % MOBA_BLOCK == 0 and w_in.shape[2] == 6664
    xf = x.reshape(t, d)
    tm_proj = _row_tile(t, 2048)
    tm = min(512, s)
    tm_ffn = tm
    r2 = lambda a: a.reshape(1, -1)
    v_first = None
    for i in range(depth):
        proj = _norm_proj(xf, ln_mix_pre[i], _pack_w_in(w_in[i]), tm_proj, 1024)
        params = [r2(rwkv_mu[i]), r2(rwkv_w0[i]), rwkv_w2[i], r2(rwkv_a0[i]), rwkv_a2[i], rwkv_g2[i],
                  r2(rwkv_k_k[i]), r2(rwkv_k_a[i]), r2(rwkv_r_k[i]), r2(rwkv_gn_g[i]), r2(rwkv_gn_b[i])]
        if i > 0:
            params += [r2(rwkv_v0[i - 1]), rwkv_v1[i - 1], rwkv_v2[i - 1]]
        proj3 = proj.reshape(b, s, -1)
        y_a, v_cur = _rwkv(proj3, params, v_first if i > 0 else None)
        if i == 0:
            v_first = v_cur
        y_b = _moba(proj, b, s)
        y_c = _mlstm(proj3, mlstm_conv_w[i], mlstm_conv_b[i], mlstm_i_b[i], mlstm_f_b[i], mlstm_hn_g[i])
        xf = _merge(xf, proj, y_a.reshape(t, -1), y_b, y_c.reshape(t, -1), w_br_rwkv[i].astype(BF16), w_br_moba[i].astype(BF16),
                    w_br_mlstm[i].astype(BF16), w_out[i].astype(BF16), ln_mix_post[i], tm)
        xf = _ffn(xf, p[i].reshape(t, -1), s, ln_ffn_pre[i], ffn_up[i].astype(BF16), ffn_conv_w[i],
                  ffn_conv_b[i], ffn_down[i].astype(BF16), ln_ffn_post[i], ln_ple[i],
                  ple_gate[i].astype(BF16), ple_proj[i].astype(BF16), tm_ffn, 1408)
    return xf.reshape(b, s, d)
```

```python
import functools

import jax
import jax.numpy as jnp
from jax import lax
from jax.experimental import pallas as pl
from jax.experimental.pallas import tpu as pltpu

F32 = jnp.float32
BF16 = jnp.bfloat16

HEAD_DIM = 64
N_HEADS = 4
RW = N_HEADS * HEAD_DIM
BATCH_TILE = 8
CHUNK = HEAD_DIM
HEAD_SHIFT = HEAD_DIM.bit_length() - 1
MOBA_BLOCK = 256
MOBA_TOPK = 3
MOBA_QTILE_BLOCKS = 4
NORM_EPS = 1e-6
RWKV_GN_EPS = 64e-5
MASK_VALUE = -1e30
LOG2E = 1.4426950408889634
LANE = 128
SUBLANE = 8
VMEM_LIMIT = 56 * 1024 * 1024

COL_RWKV = 3072
COL_MOBA = 4096
COL_MLSTM_QK = 5632
COL_MLSTM_V = 6144
COL_MLSTM_O = 6400
COL_MLSTM_G = 6656
PACKED_WIDTH = 7168

_DIMS = {
    "nn": (((1,), (0,)), ((), ())),
    "nt": (((1,), (1,)), ((), ())),
    "tn": (((0,), (0,)), ((), ())),
}


def _dot(a, b, dims="nn"):
    return lax.dot_general(a, b, _DIMS[dims], preferred_element_type=F32)


def _split(a):
    hi = a.astype(BF16)
    lo = (a - hi.astype(F32)).astype(BF16)
    return hi, lo


def _mm1(a, b, dims="nn"):
    return _dot(a.astype(BF16), b.astype(BF16), dims)


def _mm3(a, b, dims="nn"):
    ah, al = _split(a)
    bh, bl = _split(b)
    return _dot(ah, bh, dims) + (_dot(ah, bl, dims) + _dot(al, bh, dims))


def _mm2r(a, e, dims="nn"):
    ah, al = _split(a)
    eb = e.astype(BF16)
    return _dot(ah, eb, dims) + _dot(al, eb, dims)


def _mm2l(e, a, dims="nn"):
    ah, al = _split(a)
    eb = e.astype(BF16)
    return _dot(eb, ah, dims) + _dot(eb, al, dims)


def _softplus(x):
    return jnp.maximum(x, 0.0) + jnp.log(1.0 + jnp.exp(-jnp.abs(x)))


def _rms(x, g):
    ms = jnp.mean(x * x, axis=-1, keepdims=True)
    return x * lax.rsqrt(ms + NORM_EPS) * g


def _lagged(x, prev8, lag):
    full = jnp.concatenate([prev8, x], axis=0)
    return pltpu.roll(full, lag, 0)[SUBLANE:, :]


def _tile_heads(x):
    return jnp.concatenate([x] * N_HEADS, axis=-2)


def _per(fn, *arrs):
    return jnp.stack([fn(*(a[i] for a in arrs)) for i in range(arrs[0].shape[0])])


def _rows(fn, x, w):
    lead = x.shape[:-1]
    return fn(x.reshape(-1, x.shape[-1]), w).reshape(*lead, -1)


def _bmm1(a, b, dims="nn"):
    return _per(lambda x, y: _mm1(x, y, dims), a, b)


def _batch_tile(b):
    return BATCH_TILE if b % BATCH_TILE == 0 else 1


def _same_head(n):
    ri = lax.broadcasted_iota(jnp.int32, (n, n), 0)
    ci = lax.broadcasted_iota(jnp.int32, (n, n), 1)
    return (ri >> HEAD_SHIFT) == (ci >> HEAD_SHIFT)


def _lane_head_pos(rows, n):
    lane = lax.broadcasted_iota(jnp.int32, (rows, n), 1)
    return lane >> HEAD_SHIFT, lane & (HEAD_DIM - 1)


def _cparams(sem):
    return pltpu.CompilerParams(dimension_semantics=sem, vmem_limit_bytes=VMEM_LIMIT)


def _proj_body(x_ref, g_ref, w_ref, o_ref, h_ref):
    @pl.when(pl.program_id(1) == 0)
    def _():
        h_ref[...] = _rms(x_ref[...], g_ref[...]).astype(BF16)

    o_ref[...] = _dot(h_ref[...], w_ref[...]).astype(o_ref.dtype)


def _norm_proj(x2d, g, w, tm, tn):
    t, d = x2d.shape
    n = w.shape[1]
    return pl.pallas_call(
        _proj_body,
        grid=(t // tm, n // tn),
        in_specs=[
            pl.BlockSpec((tm, d), lambda i, j: (i, 0)),
            pl.BlockSpec((1, d), lambda i, j: (0, 0)),
            pl.BlockSpec((d, tn), lambda i, j: (0, j)),
        ],
        out_specs=pl.BlockSpec((tm, tn), lambda i, j: (i, j)),
        out_shape=jax.ShapeDtypeStruct((t, n), BF16),
        scratch_shapes=[pltpu.VMEM((tm, d), BF16)],
        compiler_params=_cparams(("parallel", "arbitrary")),
        name="norm_proj",
    )(x2d, g.reshape(1, d), w)


def _rwkv_body(has_vres, *refs):
    if has_vres:
        (slab_ref, vf_ref, mu_ref, w0_ref, w2_ref, a0_ref, a2_ref, g2_ref, kk_ref, ka_ref,
         rk_ref, gg_ref, gb_ref, v0_ref, v1_ref, v2_ref, y_ref, vo_ref, st_ref, prev_ref) = refs
    else:
        (slab_ref, mu_ref, w0_ref, w2_ref, a0_ref, a2_ref, g2_ref, kk_ref, ka_ref,
         rk_ref, gg_ref, gb_ref, y_ref, vo_ref, st_ref, prev_ref) = refs
    c = CHUNK

    @pl.when(pl.program_id(1) == 0)
    def _():
        st_ref[...] = jnp.zeros_like(st_ref)
        prev_ref[...] = jnp.zeros_like(prev_ref)

    slab = slab_ref[...].astype(F32)
    shifted = _per(lambda x, p8: _lagged(x, p8, 1), slab, prev_ref[...])
    prev_ref[...] = slab[:, c - SUBLANE:, :]
    xs = slab + mu_ref[...] * (shifted - slab)
    r = xs[:, :, 0:RW]
    k = xs[:, :, RW:2 * RW]
    v = xs[:, :, 2 * RW:3 * RW]
    o_w = 3 * RW
    o_a = o_w + w2_ref.shape[0]
    o_g = o_a + a2_ref.shape[0]
    xw = xs[:, :, o_w:o_a]
    xa = xs[:, :, o_a:o_g]
    xg = xs[:, :, o_g:o_g + g2_ref.shape[0]]
    wlog = -_softplus(-(w0_ref[...] + _rows(_mm3, jnp.tanh(xw), w2_ref[...]))) - 0.5
    alr = jax.nn.sigmoid(a0_ref[...] + _rows(_mm1, xa, a2_ref[...]))
    g = _rows(_mm1, jax.nn.sigmoid(xg), g2_ref[...])
    if has_vres:
        mix = jax.nn.sigmoid(v0_ref[...] + _rows(_mm1, _rows(_mm1, v, v1_ref[...]), v2_ref[...]))
        v = v + (vf_ref[...] - v) * mix
    vo_ref[...] = v

    n = N_HEADS * c
    same = _same_head(n)
    hm = same.astype(F32)
    pos = lax.broadcasted_iota(jnp.int32, (c, n), 0)
    _, lane_pos = _lane_head_pos(c, n)

    kk = k * kk_ref[...]
    kk = kk / jnp.maximum(jnp.sqrt(_rows(_mm2r, kk * kk, hm)), 1e-12)
    k2 = k * (1.0 + (alr - 1.0) * ka_ref[...])

    lw = -jnp.exp(wlog)
    ti = lax.broadcasted_iota(jnp.int32, (c, c), 0)
    tj = lax.broadcasted_iota(jnp.int32, (c, c), 1)
    tril = (ti >= tj).astype(F32)
    cs = _per(lambda x: _mm2l(tril, x), lw)
    cs_end = cs[:, c - 1:c, :]
    p_in = jnp.exp(cs)
    p_ex = jnp.exp(cs - lw)
    p_inv = jnp.exp(-cs)
    p_tail = jnp.exp(cs_end - cs)

    hm16 = same.astype(BF16)

    def expand(x):
        return _tile_heads(x.astype(BF16)) * hm16

    a_c = -kk * p_ex
    r_c = r * p_in
    b_bd = expand(kk * alr * p_inv)
    k_bd = expand(k2 * p_inv)
    v_bd = expand(v)
    strict = pos > lane_pos
    incl = pos >= lane_pos
    l_ab = jnp.where(strict, _bmm1(a_c, b_bd, "nt"), 0.0)
    l_ak = jnp.where(strict, _bmm1(a_c, k_bd, "nt"), 0.0)
    m_rb = jnp.where(incl, _bmm1(r_c, b_bd, "nt"), 0.0)
    m_rk = jnp.where(incl, _bmm1(r_c, k_bd, "nt"), 0.0)

    tinv = (pos == lane_pos).astype(F32) + l_ab
    npow = l_ab
    for _ in range(5):
        npow = _bmm1(npow, expand(npow))
        tinv = tinv + _bmm1(tinv, expand(npow))

    st = st_ref[...]
    u = _bmm1(tinv, expand(_bmm1(a_c, st, "nt") + _bmm1(l_ak, v_bd)))
    y = _bmm1(r_c, st, "nt") + _bmm1(m_rb, expand(u)) + _bmm1(m_rk, v_bd)
    upd = _bmm1(u, kk * alr * p_tail, "tn") + _bmm1(v, k2 * p_tail, "tn")
    st_ref[...] = st * p_in[:, c - 1:c, :] + jnp.where(same, upd, 0.0)

    mean = _rows(_mm2r, y, hm) * (1.0 / HEAD_DIM)
    d = y - mean
    var = _rows(_mm2r, d * d, hm) * (1.0 / HEAD_DIM)
    yn = d * lax.rsqrt(var + RWKV_GN_EPS) * gg_ref[...] + gb_ref[...]
    bonus = _rows(_mm2r, r * k2 * rk_ref[...], hm) * v
    y_ref[...] = ((yn + bonus) * g).astype(y_ref.dtype)


def _rwkv(proj3, params, v_first3):
    b, s, _ = proj3.shape
    nc = s // CHUNK
    bt = _batch_tile(b)
    has_vres = v_first3 is not None
    row = lambda bi, ci: (bi, ci, 0)
    const = lambda bi, ci: (0, 0)
    in_specs = [pl.BlockSpec((bt, CHUNK, 1024), lambda bi, ci: (bi, ci, COL_RWKV // 1024))]
    args = [proj3]
    if has_vres:
        in_specs.append(pl.BlockSpec((bt, CHUNK, RW), row))
        args.append(v_first3)
    for prm in params:
        in_specs.append(pl.BlockSpec(prm.shape, const))
        args.append(prm)
    n = N_HEADS * CHUNK
    return pl.pallas_call(
        functools.partial(_rwkv_body, has_vres),
        grid=(b // bt, nc),
        in_specs=in_specs,
        out_specs=[pl.BlockSpec((bt, CHUNK, RW), row), pl.BlockSpec((bt, CHUNK, RW), row)],
        out_shape=[jax.ShapeDtypeStruct((b, s, RW), BF16), jax.ShapeDtypeStruct((b, s, RW), F32)],
        scratch_shapes=[pltpu.VMEM((bt, n, n), F32), pltpu.VMEM((bt, SUBLANE, 1024), F32)],
        compiler_params=_cparams(("parallel", "arbitrary")),
        name="rwkv7_chunk",
    )(*args)


def _moba_body(nb, n_sel, q_ref, k_ref, v_ref, o_ref, m_ref, acc_ref, qa_ref):
    bl = MOBA_BLOCK
    qb = MOBA_QTILE_BLOCKS if nb % MOBA_QTILE_BLOCKS == 0 else 1
    qt = qb * bl
    nt = nb // qb
    nbp =-(-nb // SUBLANE) * SUBLANE
    scale = HEAD_DIM ** -0.5
    lane = lax.broadcasted_iota(jnp.int32, (bl, LANE), 1)
    head0 = lane < HEAD_DIM
    hmask = (head0, jnp.logical_not(head0))
    spare = (lane - HEAD_DIM, lane)
    blk = lax.broadcasted_iota(jnp.int32, (nbp, bl), 0)
    blk_f = blk.astype(F32)
    qpos = lax.broadcasted_iota(jnp.int32, (bl, bl), 0)
    kpos = lax.broadcasted_iota(jnp.int32, (bl, bl), 1)
    causal = kpos <= qpos
    e_row = lax.broadcasted_iota(jnp.int32, (nbp, LANE), 0)
    e_lane = lax.broadcasted_iota(jnp.int32, (nbp, LANE), 1)
    place = ((e_lane == e_row + HEAD_DIM).astype(BF16), (e_lane == e_row).astype(BF16))
    klane = e_lane < HEAD_DIM

    kmean = jnp.mean(k_ref[...].astype(F32).reshape(nb, bl, LANE), axis=1)
    if nbp > nb:
        kmean = jnp.concatenate([kmean, jnp.zeros((nbp - nb, LANE), F32)], axis=0)
    kmean_h = (jnp.where(klane, kmean, 0.0), jnp.where(klane, 0.0, kmean))

    def rows(i):
        if isinstance(i, int):
            return pl.ds(i * bl, bl)
        return pl.ds(pl.multiple_of(i * bl, bl), bl)

    def kv_tiles(j):
        kb = k_ref[rows(j), :].astype(F32) * (scale * LOG2E)
        vb = v_ref[rows(j), :]
        kp = [jnp.where(hmask[h], kb, (spare[h] == j).astype(F32)).astype(BF16) for h in range(2)]
        vp = [jnp.where(hmask[h], vb, 1.0).astype(BF16) for h in range(2)]
        return kp, vp

    dg = qb if qb > 1 else (2 if nb % 2 == 0 else 1)

    def diag_body(ii, carry):
        chains = [(ii * dg + g, h) for g in range(dg) for h in range(2)]
        qf = {g: q_ref[rows(ii * dg + g), :].astype(F32) for g in range(dg)}
        kv = {g: kv_tiles(ii * dg + g) for g in range(dg)}
        qh = [jnp.where(hmask[h], qf[g], 0.0) for g in range(dg) for h in range(2)]
        s = [_dot(qh[c].astype(BF16), kv[c // 2][0][h], "nt") for c, (_, h) in enumerate(chains)]
        bs = [jnp.where(blk < i, _mm3(kmean_h[h], qf[c // 2], "nt"), MASK_VALUE)
              for c, (i, h) in enumerate(chains)]
        sel_t = []
        for c, (i, h) in enumerate(chains):
            work = bs[c]
            sel = jnp.zeros((nbp, bl), jnp.bool_)
            for _ in range(n_sel):
                best = jnp.max(work, axis=0, keepdims=True)
                first = jnp.min(jnp.where(work == best, blk_f, float(nbp)), axis=0, keepdims=True)
                hit = blk_f == first
                sel = sel | hit
                work = jnp.where(hit, -jnp.inf, work)
            sel_t.append((sel & (blk < i)).astype(BF16))
        picked = [_dot(sel_t[c], place[h], "tn") for c, (_, h) in enumerate(chains)]
        for c, (i, h) in enumerate(chains):
            sm = jnp.where(causal, s[c], MASK_VALUE)
            m = jnp.max(sm, axis=1, keepdims=True)
            p = jnp.exp2(sm - m)
            m_ref[h, rows(i), :] = jnp.broadcast_to(m, (bl, LANE))
            acc_ref[h, rows(i), :] = _dot(p.astype(BF16), kv[c // 2][1][h])
            bias = jnp.where((spare[h] >= 0) & (spare[h] < nb) & (picked[c] < 0.5), MASK_VALUE, 0.0)
            qa_ref[h, rows(i), :] = (qh[c] + bias).astype(BF16)
        return carry

    lax.fori_loop(0, nb // dg, diag_body, 0)

    def q_tile(t, kp, vp):
        rs = pl.ds(t * qt, qt)
        s = [_dot(qa_ref[h, rs, :], kp[h], "nt") for h in range(2)]
        for h in range(2):
            m_prev = m_ref[h, rs, :]
            m_new = jnp.maximum(m_prev, jnp.max(s[h], axis=1, keepdims=True))
            p = jnp.exp2(s[h] - jnp.concatenate([m_new, m_new], axis=1))
            m_ref[h, rs, :] = m_new
            acc_ref[h, rs, :] = (acc_ref[h, rs, :] * jnp.exp2(m_prev - m_new)
                                 + _dot(p.astype(BF16), vp[h]))

    for t0 in range(nt):
        def key_body(j, carry, t0=t0):
            kp, vp = kv_tiles(j)
            for t in range(t0, nt):
                q_tile(t, kp, vp)
            return carry

        lax.fori_loop(max(0, t0 * qb - 1), min(nb - 1, (t0 + 1) * qb - 1), key_body, 0)

    def out_body(t, carry):
        rs = pl.ds(pl.multiple_of(t * qt, qt), qt)
        a0 = acc_ref[0, rs, :]
        a1 = acc_ref[1, rs, :]
        first = lax.broadcasted_iota(jnp.int32, (qt, LANE), 1) < HEAD_DIM
        out = jnp.where(first, a0 / pltpu.roll(a0, HEAD_DIM, 1), a1 / pltpu.roll(a1, HEAD_DIM, 1))
        o_ref[rs, :] = out.astype(o_ref.dtype)
        return carry

    lax.fori_loop(0, nt, out_body, 0)


def _moba(proj, b, s):
    t = proj.shape[0]
    nb = s // MOBA_BLOCK
    n_sel = min(MOBA_TOPK, nb - 1)
    nhp = 512 // LANE
    qc, kc, vc = COL_MOBA // LANE, (COL_MOBA + 512) // LANE, (COL_MOBA + 1024) // LANE
    return pl.pallas_call(
        functools.partial(_moba_body, nb, n_sel),
        grid=(b, nhp),
        in_specs=[
            pl.BlockSpec((s, LANE), lambda bi, hp: (bi, qc + hp)),
            pl.BlockSpec((s, LANE), lambda bi, hp: (bi, kc + hp)),
            pl.BlockSpec((s, LANE), lambda bi, hp: (bi, vc + hp)),
        ],
        out_specs=pl.BlockSpec((s, LANE), lambda bi, hp: (bi, hp)),
        out_shape=jax.ShapeDtypeStruct((t, 512), BF16),
        scratch_shapes=[pltpu.VMEM((2, s, LANE), F32), pltpu.VMEM((2, s, LANE), F32),
                        pltpu.VMEM((2, s, LANE), BF16)],
        compiler_params=_cparams(("parallel", "parallel")),
        name="moba_attn",
    )(proj, proj, proj)


def _mlstm_body(qk_ref, v_ref, o_ref, gate_ref, cw_ref, cb_ref, gbias_ref, hng_ref,
                y_ref, cst_ref, n_ref, m_ref, prev_ref):
    c = CHUNK
    n = N_HEADS * c

    @pl.when(pl.program_id(1) == 0)
    def _():
        cst_ref[...] = jnp.zeros_like(cst_ref)
        n_ref[...] = jnp.zeros_like(n_ref)
        m_ref[...] = jnp.zeros_like(m_ref)
        prev_ref[...] = jnp.zeros_like(prev_ref)

    x = qk_ref[...].astype(F32)
    prev8 = prev_ref[...]
    cw = cw_ref[...]
    lag = lambda j: _per(lambda xx, p8: _lagged(xx, p8, j), x, prev8)
    conv = cb_ref[...] + cw[3:4] * x + cw[2:3] * lag(1) + cw[1:2] * lag(2) + cw[0:1] * lag(3)
    prev_ref[...] = x[:, c - SUBLANE:, :]
    qk = conv * jax.nn.sigmoid(conv)
    q = qk[:, :, 0:RW]
    k = qk[:, :, RW:2 * RW] * (HEAD_DIM ** -0.5)
    v = v_ref[...].astype(F32)

    same = _same_head(n)
    hm = same.astype(F32)
    hm16 = same.astype(BF16)
    k_mt = _tile_heads(k.astype(BF16)) * hm16
    v_mt = _tile_heads(v.astype(BF16)) * hm16

    g = gate_ref[...].astype(F32) + gbias_ref[...]
    glane = lax.broadcasted_iota(jnp.int32, (c, LANE), 1)
    lg = jnp.where(glane < N_HEADS, g, -_softplus(-g))
    ti = lax.broadcasted_iota(jnp.int32, (c, c), 0)
    tj = lax.broadcasted_iota(jnp.int32, (c, c), 1)
    tril = (ti >= tj).astype(F32)
    cum = _per(lambda z: _mm2l(tril, z), lg)
    src = lax.broadcasted_iota(jnp.int32, (LANE, n), 0)
    dst_head, _ = _lane_head_pos(LANE, n)
    li_e = _rows(_mm2r, lg, (src == dst_head).astype(F32))
    b_e = _rows(_mm2r, cum, (src == dst_head + N_HEADS).astype(F32))
    pos = lax.broadcasted_iota(jnp.int32, (c, n), 0)
    lane_head, lane_pos = _lane_head_pos(c, n)
    li_row = jnp.sum(jnp.where(pos == lane_pos, li_e, 0.0), axis=1, keepdims=True)
    b_row = jnp.sum(jnp.where(pos == lane_pos, b_e, 0.0), axis=1, keepdims=True)
    bend = b_e[:, c - 1:c, :]

    m_row = m_ref[...]
    dmat = jnp.where(pos >= lane_pos, b_e - b_row + li_row, -jnp.inf)
    inter = b_e + m_row
    m_t = inter
    for h in range(N_HEADS):
        in_head = lane_head == h
        mh = jnp.max(jnp.where(in_head, dmat, -jnp.inf), axis=2, keepdims=True)
        m_t = jnp.where(in_head, jnp.maximum(m_t, mh), m_t)
    wts = jnp.exp(dmat - m_t)
    s_inter = jnp.exp(inter - m_t)
    cst = cst_ref[...]
    n_row = n_ref[...]
    qk_w = _bmm1(q, k_mt, "nt") * wts
    num = s_inter * _bmm1(q, cst, "nt") + _bmm1(qk_w, v_mt)
    den = _rows(_mm2r, s_inter * (q * n_row) + qk_w, hm)
    y = num / jnp.maximum(jnp.abs(den), jnp.exp(-m_t))

    g_e = bend - b_e + li_e
    m_new = jnp.maximum(bend + m_row, jnp.max(g_e, axis=1, keepdims=True))
    w_e = jnp.exp(g_e - m_new)
    scale = jnp.exp(bend + m_row - m_new)
    cst_ref[...] = scale * cst + jnp.where(same, _bmm1(v * w_e, k, "tn"), 0.0)
    n_ref[...] = scale * n_row + jnp.sum(k * w_e, axis=1, keepdims=True)
    m_ref[...] = m_new

    mean = _rows(_mm2r, y, hm) * (1.0 / HEAD_DIM)
    d = y - mean
    var = _rows(_mm2r, d * d, hm) * (1.0 / HEAD_DIM)
    y = d * lax.rsqrt(var + NORM_EPS) * hng_ref[...] * jax.nn.sigmoid(o_ref[...].astype(F32))
    y_ref[...] = y.astype(y_ref.dtype)


def _mlstm(proj3, conv_w, conv_b, i_b, f_b, hn_g):
    b, s, _ = proj3.shape
    c = CHUNK
    nc = s // c
    n = N_HEADS * c
    bt = _batch_tile(b)
    gbias = jnp.concatenate([i_b, f_b, jnp.zeros((LANE - 2 * N_HEADS,), F32)]).reshape(1, LANE)
    row = lambda bi, ci: (bi, ci, 0)
    const = lambda bi, ci: (0, 0)
    return pl.pallas_call(
        _mlstm_body,
        grid=(b // bt, nc),
        in_specs=[
            pl.BlockSpec((bt, c, 2 * RW), lambda bi, ci: (bi, ci, COL_MLSTM_QK // (2 * RW))),
            pl.BlockSpec((bt, c, RW), lambda bi, ci: (bi, ci, COL_MLSTM_V // RW)),
            pl.BlockSpec((bt, c, RW), lambda bi, ci: (bi, ci, COL_MLSTM_O // RW)),
            pl.BlockSpec((bt, c, LANE), lambda bi, ci: (bi, ci, COL_MLSTM_G // LANE)),
            pl.BlockSpec(conv_w.shape, const),
            pl.BlockSpec((1, 2 * RW), const),
            pl.BlockSpec((1, LANE), const),
            pl.BlockSpec((1, RW), const),
        ],
        out_specs=pl.BlockSpec((bt, c, RW), row),
        out_shape=jax.ShapeDtypeStruct((b, s, RW), BF16),
        scratch_shapes=[pltpu.VMEM((bt, n, n), F32), pltpu.VMEM((bt, 1, n), F32),
                        pltpu.VMEM((bt, 1, n), F32), pltpu.VMEM((bt, SUBLANE, 2 * RW), F32)],
        compiler_params=_cparams(("parallel", "arbitrary")),
        name="mlstm_chunk",
    )(proj3, proj3, proj3, proj3, conv_w, conv_b.reshape(1, -1), gbias,
      hn_g.reshape(1, -1))


def _merge_body(x_ref, ya_ref, yb_ref, yc_ref, ga_ref, gb_ref, gc_ref, wa_ref, wb_ref, wc_ref,
                wo_ref, g_ref, o_ref):
    merged = (jax.nn.sigmoid(ga_ref[...].astype(F32)) * _mm1(ya_ref[...], wa_ref[...])
              + jax.nn.sigmoid(gb_ref[...].astype(F32)) * _mm1(yb_ref[...], wb_ref[...])
              + jax.nn.sigmoid(gc_ref[...].astype(F32)) * _mm1(yc_ref[...], wc_ref[...]))
    o_ref[...] = x_ref[...] + _rms(_mm1(merged, wo_ref[...]), g_ref[...])


def _merge(x2d, proj, ya, yb, yc, wa, wb, wc, wo, g, tm):
    t, d = x2d.shape
    row = lambda i: (i, 0)
    const = lambda i: (0, 0)
    return pl.pallas_call(
        _merge_body,
        grid=(t // tm,),
        in_specs=[
            pl.BlockSpec((tm, d), row),
            pl.BlockSpec((tm, ya.shape[1]), row),
            pl.BlockSpec((tm, yb.shape[1]), row),
            pl.BlockSpec((tm, yc.shape[1]), row),
            pl.BlockSpec((tm, d), lambda i: (i, 0)),
            pl.BlockSpec((tm, d), lambda i: (i, 1)),
            pl.BlockSpec((tm, d), lambda i: (i, 2)),
            pl.BlockSpec(wa.shape, const),
            pl.BlockSpec(wb.shape, const),
            pl.BlockSpec(wc.shape, const),
            pl.BlockSpec(wo.shape, const),
            pl.BlockSpec((1, d), const),
        ],
        out_specs=pl.BlockSpec((tm, d), row),
        out_shape=jax.ShapeDtypeStruct((t, d), F32),
        compiler_params=_cparams(("parallel",)),
        name="merge_out",
    )(x2d, ya, yb, yc, proj, proj, proj, wa, wb, wc, wo, g.reshape(1, d))


FFN_ROW_GROUPS = 2


def _ffn_body(nt_seq, n_ff, x_ref, g1_ref, upg_ref, upv_ref, cwg_ref, cwv_ref, cbg_ref, cbv_ref,
              down_ref, g2_ref, g3_ref, pg_ref, pp_ref, p_ref, o_ref, h_ref, acc_ref, ugp_ref, uvp_ref):
    i = pl.program_id(0)
    j = pl.program_id(1)
    tm = x_ref.shape[0]

    @pl.when(j == 0)
    def _():
        h_ref[...] = _rms(x_ref[...], g1_ref[...]).astype(BF16)
        acc_ref[...] = jnp.zeros_like(acc_ref)

    @pl.when(i % nt_seq == 0)
    def _():
        ugp_ref[j] = jnp.zeros(ugp_ref.shape[1:], F32)
        uvp_ref[j] = jnp.zeros(uvp_ref.shape[1:], F32)

    ngrp = FFN_ROW_GROUPS if tm % (FFN_ROW_GROUPS * SUBLANE) == 0 else 1
    rg = tm // ngrp
    upg = upg_ref[...]
    upv = upv_ref[...]
    hs = [h_ref[r * rg:(r + 1) * rg, :] for r in range(ngrp)]
    ug = [_dot(hr, upg) for hr in hs]
    uv = [_dot(hr, upv) for hr in hs]
    pg = [ugp_ref[j]] + [ug[r][rg - SUBLANE:, :] for r in range(ngrp - 1)]
    pv = [uvp_ref[j]] + [uv[r][rg - SUBLANE:, :] for r in range(ngrp - 1)]
    ugp_ref[j] = ug[-1][rg - SUBLANE:, :]
    uvp_ref[j] = uv[-1][rg - SUBLANE:, :]
    cwg = cwg_ref[...]
    cwv = cwv_ref[...]
    down = down_ref[...]
    for r in range(ngrp):
        cg = (cbg_ref[...] + cwg[2:3] * ug[r] + cwg[1:2] * _lagged(ug[r], pg[r], 1)
              + cwg[0:1] * _lagged(ug[r], pg[r], 2))
        cv = (cbv_ref[...] + cwv[2:3] * uv[r] + cwv[1:2] * _lagged(uv[r], pv[r], 1)
              + cwv[0:1] * _lagged(uv[r], pv[r], 2))
        act = jax.nn.gelu(cg, approximate=True) * cv
        acc_ref[r * rg:(r + 1) * rg, :] += _mm1(act, down)

    @pl.when(j == n_ff - 1)
    def _():
        x2 = x_ref[...] + _rms(acc_ref[...], g2_ref[...])
        gate = jax.nn.sigmoid(_mm1(_rms(x2, g3_ref[...]), pg_ref[...]))
        o_ref[...] = x2 + gate * _mm1(p_ref[...], pp_ref[...])


def _ffn(x2d, p2d, s, g1, up, cw, cb, down, g2, g3, pgate, pproj, tm, tf):
    t, d = x2d.shape
    dff = down.shape[0]
    n_ff = dff // tf
    nt_seq = s // tm
    ple = p2d.shape[1]
    row = lambda i, j: (i, 0)
    const = lambda i, j: (0, 0)
    cb2 = cb.reshape(1, -1)
    return pl.pallas_call(
        functools.partial(_ffn_body, nt_seq, n_ff),
        grid=(t // tm, n_ff),
        in_specs=[
            pl.BlockSpec((tm, d), row),
            pl.BlockSpec((1, d), const),
            pl.BlockSpec((d, tf), lambda i, j: (0, j)),
            pl.BlockSpec((d, tf), lambda i, j: (0, n_ff + j)),
            pl.BlockSpec((cw.shape[0], tf), lambda i, j: (0, j)),
            pl.BlockSpec((cw.shape[0], tf), lambda i, j: (0, n_ff + j)),
            pl.BlockSpec((1, tf), lambda i, j: (0, j)),
            pl.BlockSpec((1, tf), lambda i, j: (0, n_ff + j)),
            pl.BlockSpec((tf, d), lambda i, j: (j, 0)),
            pl.BlockSpec((1, d), const),
            pl.BlockSpec((1, d), const),
            pl.BlockSpec(pgate.shape, const),
            pl.BlockSpec(pproj.shape, const),
            pl.BlockSpec((tm, ple), row),
        ],
        out_specs=pl.BlockSpec((tm, d), row),
        out_shape=jax.ShapeDtypeStruct((t, d), F32),
        scratch_shapes=[pltpu.VMEM((tm, d), BF16), pltpu.VMEM((tm, d), F32),
                        pltpu.VMEM((n_ff, SUBLANE, tf), F32), pltpu.VMEM((n_ff, SUBLANE, tf), F32)],
        compiler_params=_cparams(("arbitrary", "arbitrary")),
        name="ffn_ple",
    )(x2d, g1.reshape(1, d), up, up, cw, cw, cb2, cb2, down, g2.reshape(1, d), g3.reshape(1, d),
      pgate, pproj, p2d)


def _pack_w_in(w):
    d = w.shape[0]
    rwkv = w[:, 0:1024]
    moba = w[:, 1024:2560]
    ml = w[:, 2560:3592]
    gate = w[:, 3592:6664]
    pad = jnp.zeros((d, PACKED_WIDTH - COL_MLSTM_G - 2 * N_HEADS), w.dtype)
    packed = jnp.concatenate([gate, rwkv, moba, ml, pad], axis=1)
    return packed.astype(BF16)


def _row_tile(t, want):
    return want if t % want == 0 else t


def kernel(x, p, ln_mix_pre, ln_mix_post, ln_ffn_pre, ln_ffn_post, ln_ple, w_in, rwkv_mu, rwkv_w0, rwkv_w2, rwkv_a0, rwkv_a2, rwkv_g2, rwkv_k_k, rwkv_k_a, rwkv_r_k, rwkv_gn_g, rwkv_gn_b, rwkv_v0, rwkv_v1, rwkv_v2, mlstm_conv_w, mlstm_conv_b, mlstm_i_b, mlstm_f_b, mlstm_hn_g, w_br_rwkv, w_br_moba, w_br_mlstm, w_out, ffn_up, ffn_conv_w, ffn_conv_b, ffn_down, ple_proj, ple_gate):
    b, s, d = x.shape
    depth = w_in.shape[0]
    t = b * s
    assert d == 1024 and s % MOBA_BLOCK == 0 and w_in.shape[2] == 6664
    xf = x.reshape(t, d)
    tm_proj = _row_tile(t, 2048)
    tm_ffn = min(512, s)
    tm_merge = _row_tile(t, 1024)
    r2 = lambda a: a.reshape(1, -1)
    v_first = None
    for i in range(depth):
        proj = _norm_proj(xf, ln_mix_pre[i], _pack_w_in(w_in[i]), tm_proj, 1024)
        params = [r2(rwkv_mu[i]), r2(rwkv_w0[i]), rwkv_w2[i], r2(rwkv_a0[i]), rwkv_a2[i], rwkv_g2[i],
                  r2(rwkv_k_k[i]), r2(rwkv_k_a[i]), r2(rwkv_r_k[i]), r2(rwkv_gn_g[i]), r2(rwkv_gn_b[i])]
        if i > 0:
            params += [r2(rwkv_v0[i - 1]), rwkv_v1[i - 1], rwkv_v2[i - 1]]
        proj3 = proj.reshape(b, s, -1)
        y_a, v_cur = _rwkv(proj3, params, v_first if i > 0 else None)
        if i == 0:
            v_first = v_cur
        y_b = _moba(proj, b, s)
        y_c = _mlstm(proj3, mlstm_conv_w[i], mlstm_conv_b[i], mlstm_i_b[i], mlstm_f_b[i], mlstm_hn_g[i])
        xf = _merge(xf, proj, y_a.reshape(t, -1), y_b, y_c.reshape(t, -1), w_br_rwkv[i].astype(BF16), w_br_moba[i].astype(BF16),
                    w_br_mlstm[i].astype(BF16), w_out[i].astype(BF16), ln_mix_post[i], tm_merge)
        xf = _ffn(xf, p[i].reshape(t, -1), s, ln_ffn_pre[i], ffn_up[i].astype(BF16), ffn_conv_w[i],
                  ffn_conv_b[i], ffn_down[i].astype(BF16), ln_ffn_post[i], ln_ple[i],
                  ple_gate[i].astype(BF16), ple_proj[i].astype(BF16), tm_ffn, 1408)
    return xf.reshape(b, s, d)
```

```python
import functools

import jax
import jax.numpy as jnp
from jax import lax
from jax.experimental import pallas as pl
from jax.experimental.pallas import tpu as pltpu

F32 = jnp.float32
BF16 = jnp.bfloat16

HEAD_DIM = 64
N_HEADS = 4
RW = N_HEADS * HEAD_DIM
BATCH_TILE = 8
CHUNK = HEAD_DIM
HEAD_SHIFT = HEAD_DIM.bit_length() - 1
MOBA_BLOCK = 256
MOBA_TOPK = 3
MOBA_QTILE_BLOCKS = 4
MOBA_PAIRS = 2
MOBA_DIAG_CHAINS = 8
NORM_EPS = 1e-6
RWKV_GN_EPS = 64e-5
MASK_VALUE = -1e30
LOG2E = 1.4426950408889634
LANE = 128
SUBLANE = 8
VMEM_LIMIT = 56 * 1024 * 1024

COL_RWKV = 3072
COL_MOBA = 4096
COL_MLSTM_QK = 5632
COL_MLSTM_V = 6144
COL_MLSTM_O = 6400
COL_MLSTM_G = 6656
PACKED_WIDTH = 7168

_DIMS = {
    "nn": (((1,), (0,)), ((), ())),
    "nt": (((1,), (1,)), ((), ())),
    "tn": (((0,), (0,)), ((), ())),
}


def _dot(a, b, dims="nn"):
    return lax.dot_general(a, b, _DIMS[dims], preferred_element_type=F32)


def _split(a):
    hi = a.astype(BF16)
    lo = (a - hi.astype(F32)).astype(BF16)
    return hi, lo


def _mm1(a, b, dims="nn"):
    return _dot(a.astype(BF16), b.astype(BF16), dims)


def _mm3(a, b, dims="nn"):
    ah, al = _split(a)
    bh, bl = _split(b)
    return _dot(ah, bh, dims) + (_dot(ah, bl, dims) + _dot(al, bh, dims))


def _mm2r(a, e, dims="nn"):
    ah, al = _split(a)
    eb = e.astype(BF16)
    return _dot(ah, eb, dims) + _dot(al, eb, dims)


def _mm2l(e, a, dims="nn"):
    ah, al = _split(a)
    eb = e.astype(BF16)
    return _dot(eb, ah, dims) + _dot(eb, al, dims)


def _softplus(x):
    return jnp.maximum(x, 0.0) + jnp.log(1.0 + jnp.exp(-jnp.abs(x)))


def _rms(x, g):
    ms = jnp.mean(x * x, axis=-1, keepdims=True)
    return x * lax.rsqrt(ms + NORM_EPS) * g


def _lagged(x, prev8, lag):
    full = jnp.concatenate([prev8, x], axis=0)
    return pltpu.roll(full, lag, 0)[SUBLANE:, :]


def _tile_heads(x):
    return jnp.concatenate([x] * N_HEADS, axis=-2)


def _per(fn, *arrs):
    return jnp.stack([fn(*(a[i] for a in arrs)) for i in range(arrs[0].shape[0])])


def _rows(fn, x, w):
    lead = x.shape[:-1]
    return fn(x.reshape(-1, x.shape[-1]), w).reshape(*lead, -1)


def _bmm1(a, b, dims="nn"):
    return _per(lambda x, y: _mm1(x, y, dims), a, b)


def _batch_tile(b):
    return BATCH_TILE if b % BATCH_TILE == 0 else 1


def _same_head(n):
    ri = lax.broadcasted_iota(jnp.int32, (n, n), 0)
    ci = lax.broadcasted_iota(jnp.int32, (n, n), 1)
    return (ri >> HEAD_SHIFT) == (ci >> HEAD_SHIFT)


def _lane_head_pos(rows, n):
    lane = lax.broadcasted_iota(jnp.int32, (rows, n), 1)
    return lane >> HEAD_SHIFT, lane & (HEAD_DIM - 1)


def _cparams(sem):
    return pltpu.CompilerParams(dimension_semantics=sem, vmem_limit_bytes=VMEM_LIMIT)


def _proj_body(x_ref, g_ref, w_ref, o_ref, h_ref):
    @pl.when(pl.program_id(1) == 0)
    def _():
        h_ref[...] = _rms(x_ref[...], g_ref[...]).astype(BF16)

    o_ref[...] = _dot(h_ref[...], w_ref[...]).astype(o_ref.dtype)


def _norm_proj(x2d, g, w, tm, tn):
    t, d = x2d.shape
    n = w.shape[1]
    return pl.pallas_call(
        _proj_body,
        grid=(t // tm, n // tn),
        in_specs=[
            pl.BlockSpec((tm, d), lambda i, j: (i, 0)),
            pl.BlockSpec((1, d), lambda i, j: (0, 0)),
            pl.BlockSpec((d, tn), lambda i, j: (0, j)),
        ],
        out_specs=pl.BlockSpec((tm, tn), lambda i, j: (i, j)),
        out_shape=jax.ShapeDtypeStruct((t, n), BF16),
        scratch_shapes=[pltpu.VMEM((tm, d), BF16)],
        compiler_params=_cparams(("parallel", "arbitrary")),
        name="norm_proj",
    )(x2d, g.reshape(1, d), w)


def _rwkv_body(has_vres, *refs):
    if has_vres:
        (slab_ref, vf_ref, mu_ref, w0_ref, w2_ref, a0_ref, a2_ref, g2_ref, kk_ref, ka_ref,
         rk_ref, gg_ref, gb_ref, v0_ref, v1_ref, v2_ref, y_ref, vo_ref, st_ref, prev_ref) = refs
    else:
        (slab_ref, mu_ref, w0_ref, w2_ref, a0_ref, a2_ref, g2_ref, kk_ref, ka_ref,
         rk_ref, gg_ref, gb_ref, y_ref, vo_ref, st_ref, prev_ref) = refs
    c = CHUNK

    @pl.when(pl.program_id(1) == 0)
    def _():
        st_ref[...] = jnp.zeros_like(st_ref)
        prev_ref[...] = jnp.zeros_like(prev_ref)

    slab = slab_ref[...].astype(F32)
    shifted = _per(lambda x, p8: _lagged(x, p8, 1), slab, prev_ref[...])
    prev_ref[...] = slab[:, c - SUBLANE:, :]
    xs = slab + mu_ref[...] * (shifted - slab)
    r = xs[:, :, 0:RW]
    k = xs[:, :, RW:2 * RW]
    v = xs[:, :, 2 * RW:3 * RW]
    o_w = 3 * RW
    o_a = o_w + w2_ref.shape[0]
    o_g = o_a + a2_ref.shape[0]
    xw = xs[:, :, o_w:o_a]
    xa = xs[:, :, o_a:o_g]
    xg = xs[:, :, o_g:o_g + g2_ref.shape[0]]
    wlog = -_softplus(-(w0_ref[...] + _rows(_mm3, jnp.tanh(xw), w2_ref[...]))) - 0.5
    alr = jax.nn.sigmoid(a0_ref[...] + _rows(_mm1, xa, a2_ref[...]))
    g = _rows(_mm1, jax.nn.sigmoid(xg), g2_ref[...])
    if has_vres:
        mix = jax.nn.sigmoid(v0_ref[...] + _rows(_mm1, _rows(_mm1, v, v1_ref[...]), v2_ref[...]))
        v = v + (vf_ref[...] - v) * mix
    vo_ref[...] = v

    n = N_HEADS * c
    same = _same_head(n)
    hm = same.astype(F32)
    pos = lax.broadcasted_iota(jnp.int32, (c, n), 0)
    _, lane_pos = _lane_head_pos(c, n)

    kk = k * kk_ref[...]
    kk = kk / jnp.maximum(jnp.sqrt(_rows(_mm2r, kk * kk, hm)), 1e-12)
    k2 = k * (1.0 + (alr - 1.0) * ka_ref[...])

    lw = -jnp.exp(wlog)
    ti = lax.broadcasted_iota(jnp.int32, (c, c), 0)
    tj = lax.broadcasted_iota(jnp.int32, (c, c), 1)
    tril = (ti >= tj).astype(F32)
    cs = _per(lambda x: _mm2l(tril, x), lw)
    cs_end = cs[:, c - 1:c, :]
    p_in = jnp.exp(cs)
    p_ex = jnp.exp(cs - lw)
    p_inv = jnp.exp(-cs)
    p_tail = jnp.exp(cs_end - cs)

    hm16 = same.astype(BF16)

    def expand(x):
        return _tile_heads(x.astype(BF16)) * hm16

    a_c = -kk * p_ex
    r_c = r * p_in
    b_bd = expand(kk * alr * p_inv)
    k_bd = expand(k2 * p_inv)
    v_bd = expand(v)
    strict = pos > lane_pos
    incl = pos >= lane_pos
    l_ab = jnp.where(strict, _bmm1(a_c, b_bd, "nt"), 0.0)
    l_ak = jnp.where(strict, _bmm1(a_c, k_bd, "nt"), 0.0)
    m_rb = jnp.where(incl, _bmm1(r_c, b_bd, "nt"), 0.0)
    m_rk = jnp.where(incl, _bmm1(r_c, k_bd, "nt"), 0.0)

    tinv = (pos == lane_pos).astype(F32) + l_ab
    npow = l_ab
    for _ in range(5):
        npow = _bmm1(npow, expand(npow))
        tinv = tinv + _bmm1(tinv, expand(npow))

    st = st_ref[...]
    u = _bmm1(tinv, expand(_bmm1(a_c, st, "nt") + _bmm1(l_ak, v_bd)))
    y = _bmm1(r_c, st, "nt") + _bmm1(m_rb, expand(u)) + _bmm1(m_rk, v_bd)
    upd = _bmm1(u, kk * alr * p_tail, "tn") + _bmm1(v, k2 * p_tail, "tn")
    st_ref[...] = st * p_in[:, c - 1:c, :] + jnp.where(same, upd, 0.0)

    mean = _rows(_mm2r, y, hm) * (1.0 / HEAD_DIM)
    d = y - mean
    var = _rows(_mm2r, d * d, hm) * (1.0 / HEAD_DIM)
    yn = d * lax.rsqrt(var + RWKV_GN_EPS) * gg_ref[...] + gb_ref[...]
    bonus = _rows(_mm2r, r * k2 * rk_ref[...], hm) * v
    y_ref[...] = ((yn + bonus) * g).astype(y_ref.dtype)


def _rwkv(proj3, params, v_first3):
    b, s, _ = proj3.shape
    nc = s // CHUNK
    bt = _batch_tile(b)
    has_vres = v_first3 is not None
    row = lambda bi, ci: (bi, ci, 0)
    const = lambda bi, ci: (0, 0)
    in_specs = [pl.BlockSpec((bt, CHUNK, 1024), lambda bi, ci: (bi, ci, COL_RWKV // 1024))]
    args = [proj3]
    if has_vres:
        in_specs.append(pl.BlockSpec((bt, CHUNK, RW), row))
        args.append(v_first3)
    for prm in params:
        in_specs.append(pl.BlockSpec(prm.shape, const))
        args.append(prm)
    n = N_HEADS * CHUNK
    return pl.pallas_call(
        functools.partial(_rwkv_body, has_vres),
        grid=(b // bt, nc),
        in_specs=in_specs,
        out_specs=[pl.BlockSpec((bt, CHUNK, RW), row), pl.BlockSpec((bt, CHUNK, RW), row)],
        out_shape=[jax.ShapeDtypeStruct((b, s, RW), BF16), jax.ShapeDtypeStruct((b, s, RW), F32)],
        scratch_shapes=[pltpu.VMEM((bt, n, n), F32), pltpu.VMEM((bt, SUBLANE, 1024), F32)],
        compiler_params=_cparams(("parallel", "arbitrary")),
        name="rwkv7_chunk",
    )(*args)


def _moba_body(nb, n_sel, q_ref, k_ref, v_ref, o_ref, m_ref, acc_ref, qa_ref):
    bl = MOBA_BLOCK
    nh = 2 * (q_ref.shape[1] // LANE)
    qb = MOBA_QTILE_BLOCKS if nb % MOBA_QTILE_BLOCKS == 0 else 1
    qt = qb * bl
    nt = nb // qb
    nbp = -(-nb // SUBLANE) * SUBLANE
    scale = HEAD_DIM ** -0.5
    lane = lax.broadcasted_iota(jnp.int32, (bl, LANE), 1)
    head0 = lane < HEAD_DIM
    hmask = (head0, jnp.logical_not(head0))
    spare = (lane - HEAD_DIM, lane)
    blk = lax.broadcasted_iota(jnp.int32, (nbp, bl), 0)
    blk_f = blk.astype(F32)
    qpos = lax.broadcasted_iota(jnp.int32, (bl, bl), 0)
    kpos = lax.broadcasted_iota(jnp.int32, (bl, bl), 1)
    causal = kpos <= qpos
    e_row = lax.broadcasted_iota(jnp.int32, (nbp, LANE), 0)
    e_lane = lax.broadcasted_iota(jnp.int32, (nbp, LANE), 1)
    place = ((e_lane == e_row + HEAD_DIM).astype(BF16), (e_lane == e_row).astype(BF16))
    klane = e_lane < HEAD_DIM

    def pair(h):
        return slice((h // 2) * LANE, (h // 2 + 1) * LANE)

    kmean = jnp.mean(k_ref[...].astype(F32).reshape(nb, bl, k_ref.shape[1]), axis=1)
    if nbp > nb:
        kmean = jnp.concatenate([kmean, jnp.zeros((nbp - nb, kmean.shape[1]), F32)], axis=0)
    kmean_h = [jnp.where(klane == (h % 2 == 0), kmean[:, pair(h)], 0.0) for h in range(nh)]

    def rows(i):
        if isinstance(i, int):
            return pl.ds(i * bl, bl)
        return pl.ds(pl.multiple_of(i * bl, bl), bl)

    def kv_tiles(j):
        kp, vp = [], []
        for h in range(nh):
            kb = k_ref[rows(j), pair(h)].astype(F32) * (scale * LOG2E)
            vb = v_ref[rows(j), pair(h)]
            kp.append(jnp.where(hmask[h % 2], kb, (spare[h % 2] == j).astype(F32)).astype(BF16))
            vp.append(jnp.where(hmask[h % 2], vb, 1.0).astype(BF16))
        return kp, vp

    dg = max(1, MOBA_DIAG_CHAINS // nh)
    while nb % dg:
        dg -= 1

    def diag_body(ii, carry):
        chains = [(g, h) for g in range(dg) for h in range(nh)]
        kv = [kv_tiles(ii * dg + g) for g in range(dg)]
        qf = [q_ref[rows(ii * dg + g), pair(h)].astype(F32) for g, h in chains]
        qh = [jnp.where(hmask[h % 2], qf[c], 0.0) for c, (g, h) in enumerate(chains)]
        s = [_dot(qh[c].astype(BF16), kv[g][0][h], "nt") for c, (g, h) in enumerate(chains)]
        bs = [jnp.where(blk < ii * dg + g, _mm3(kmean_h[h], qf[c], "nt"), MASK_VALUE)
              for c, (g, h) in enumerate(chains)]
        sel_t = []
        for c, (g, h) in enumerate(chains):
            work = bs[c]
            sel = jnp.zeros((nbp, bl), jnp.bool_)
            for _ in range(n_sel):
                best = jnp.max(work, axis=0, keepdims=True)
                first = jnp.min(jnp.where(work == best, blk_f, float(nbp)), axis=0, keepdims=True)
                hit = blk_f == first
                sel = sel | hit
                work = jnp.where(hit, -jnp.inf, work)
            sel_t.append((sel & (blk < ii * dg + g)).astype(BF16))
        picked = [_dot(sel_t[c], place[h % 2], "tn") for c, (g, h) in enumerate(chains)]
        for c, (g, h) in enumerate(chains):
            i = ii * dg + g
            sm = jnp.where(causal, s[c], MASK_VALUE)
            m = jnp.max(sm, axis=1, keepdims=True)
            p = jnp.exp2(sm - m)
            m_ref[h, rows(i), :] = jnp.broadcast_to(m, (bl, LANE))
            acc_ref[h, rows(i), :] = _dot(p.astype(BF16), kv[g][1][h])
            sp = spare[h % 2]
            bias = jnp.where((sp >= 0) & (sp < nb) & (picked[c] < 0.5), MASK_VALUE, 0.0)
            qa_ref[h, rows(i), :] = (qh[c] + bias).astype(BF16)
        return carry

    lax.fori_loop(0, nb // dg, diag_body, 0)

    def scores(t, h, kp):
        return _dot(qa_ref[h, pl.ds(t * qt, qt), :], kp[h], "nt")

    def update(t, h, s, vp):
        rs = pl.ds(t * qt, qt)
        m_prev = m_ref[h, rs, :]
        m_new = jnp.maximum(m_prev, jnp.max(s, axis=1, keepdims=True))
        p = jnp.exp2(s - jnp.concatenate([m_new, m_new], axis=1))
        m_ref[h, rs, :] = m_new
        acc_ref[h, rs, :] = acc_ref[h, rs, :] * jnp.exp2(m_prev - m_new) + _dot(p.astype(BF16), vp[h])

    for t0 in range(nt):
        def key_body(j, carry, t0=t0):
            kp, vp = kv_tiles(j)
            items = [(t, h) for t in range(t0, nt) for h in range(nh)]
            s_next = scores(*items[0], kp)
            for idx, (t, h) in enumerate(items):
                s = s_next
                if idx + 1 < len(items):
                    s_next = scores(*items[idx + 1], kp)
                update(t, h, s, vp)
            return carry

        lax.fori_loop(max(0, t0 * qb - 1), min(nb - 1, (t0 + 1) * qb - 1), key_body, 0)

    def out_body(t, carry):
        rs = pl.ds(pl.multiple_of(t * qt, qt), qt)
        first = lax.broadcasted_iota(jnp.int32, (qt, LANE), 1) < HEAD_DIM
        for h in range(0, nh, 2):
            a0 = acc_ref[h, rs, :]
            a1 = acc_ref[h + 1, rs, :]
            out = jnp.where(first, a0 / pltpu.roll(a0, HEAD_DIM, 1), a1 / pltpu.roll(a1, HEAD_DIM, 1))
            o_ref[rs, pair(h)] = out.astype(o_ref.dtype)
        return carry

    lax.fori_loop(0, nt, out_body, 0)


def _moba(proj, b, s):
    t = proj.shape[0]
    nb = s // MOBA_BLOCK
    n_sel = min(MOBA_TOPK, nb - 1)
    w = MOBA_PAIRS * LANE
    ngrp = 512 // w
    qc, kc, vc = COL_MOBA // w, (COL_MOBA + 512) // w, (COL_MOBA + 1024) // w
    nh = 2 * MOBA_PAIRS
    return pl.pallas_call(
        functools.partial(_moba_body, nb, n_sel),
        grid=(b, ngrp),
        in_specs=[
            pl.BlockSpec((s, w), lambda bi, g: (bi, qc + g)),
            pl.BlockSpec((s, w), lambda bi, g: (bi, kc + g)),
            pl.BlockSpec((s, w), lambda bi, g: (bi, vc + g)),
        ],
        out_specs=pl.BlockSpec((s, w), lambda bi, g: (bi, g)),
        out_shape=jax.ShapeDtypeStruct((t, 512), BF16),
        scratch_shapes=[pltpu.VMEM((nh, s, LANE), F32), pltpu.VMEM((nh, s, LANE), F32),
                        pltpu.VMEM((nh, s, LANE), BF16)],
        compiler_params=_cparams(("parallel", "parallel")),
        name="moba_attn",
    )(proj, proj, proj)


def _mlstm_body(qk_ref, v_ref, o_ref, gate_ref, cw_ref, cb_ref, gbias_ref, hng_ref,
                y_ref, cst_ref, n_ref, m_ref, prev_ref):
    c = CHUNK
    n = N_HEADS * c

    @pl.when(pl.program_id(1) == 0)
    def _():
        cst_ref[...] = jnp.zeros_like(cst_ref)
        n_ref[...] = jnp.zeros_like(n_ref)
        m_ref[...] = jnp.zeros_like(m_ref)
        prev_ref[...] = jnp.zeros_like(prev_ref)

    x = qk_ref[...].astype(F32)
    prev8 = prev_ref[...]
    cw = cw_ref[...]
    lag = lambda j: _per(lambda xx, p8: _lagged(xx, p8, j), x, prev8)
    conv = cb_ref[...] + cw[3:4] * x + cw[2:3] * lag(1) + cw[1:2] * lag(2) + cw[0:1] * lag(3)
    prev_ref[...] = x[:, c - SUBLANE:, :]
    qk = conv * jax.nn.sigmoid(conv)
    q = qk[:, :, 0:RW]
    k = qk[:, :, RW:2 * RW] * (HEAD_DIM ** -0.5)
    v = v_ref[...].astype(F32)

    same = _same_head(n)
    hm = same.astype(F32)
    hm16 = same.astype(BF16)
    k_mt = _tile_heads(k.astype(BF16)) * hm16
    v_mt = _tile_heads(v.astype(BF16)) * hm16

    g = gate_ref[...].astype(F32) + gbias_ref[...]
    glane = lax.broadcasted_iota(jnp.int32, (c, LANE), 1)
    lg = jnp.where(glane < N_HEADS, g, -_softplus(-g))
    ti = lax.broadcasted_iota(jnp.int32, (c, c), 0)
    tj = lax.broadcasted_iota(jnp.int32, (c, c), 1)
    tril = (ti >= tj).astype(F32)
    cum = _per(lambda z: _mm2l(tril, z), lg)
    src = lax.broadcasted_iota(jnp.int32, (LANE, n), 0)
    dst_head, _ = _lane_head_pos(LANE, n)
    li_e = _rows(_mm2r, lg, (src == dst_head).astype(F32))
    b_e = _rows(_mm2r, cum, (src == dst_head + N_HEADS).astype(F32))
    pos = lax.broadcasted_iota(jnp.int32, (c, n), 0)
    lane_head, lane_pos = _lane_head_pos(c, n)
    li_row = jnp.sum(jnp.where(pos == lane_pos, li_e, 0.0), axis=1, keepdims=True)
    b_row = jnp.sum(jnp.where(pos == lane_pos, b_e, 0.0), axis=1, keepdims=True)
    bend = b_e[:, c - 1:c, :]

    m_row = m_ref[...]
    dmat = jnp.where(pos >= lane_pos, b_e - b_row + li_row, -jnp.inf)
    inter = b_e + m_row
    m_t = inter
    for h in range(N_HEADS):
        in_head = lane_head == h
        mh = jnp.max(jnp.where(in_head, dmat, -jnp.inf), axis=2, keepdims=True)
        m_t = jnp.where(in_head, jnp.maximum(m_t, mh), m_t)
    wts = jnp.exp(dmat - m_t)
    s_inter = jnp.exp(inter - m_t)
    cst = cst_ref[...]
    n_row = n_ref[...]
    qk_w = _bmm1(q, k_mt, "nt") * wts
    num = s_inter * _bmm1(q, cst, "nt") + _bmm1(qk_w, v_mt)
    den = _rows(_mm2r, s_inter * (q * n_row) + qk_w, hm)
    y = num / jnp.maximum(jnp.abs(den), jnp.exp(-m_t))

    g_e = bend - b_e + li_e
    m_new = jnp.maximum(bend + m_row, jnp.max(g_e, axis=1, keepdims=True))
    w_e = jnp.exp(g_e - m_new)
    scale = jnp.exp(bend + m_row - m_new)
    cst_ref[...] = scale * cst + jnp.where(same, _bmm1(v * w_e, k, "tn"), 0.0)
    n_ref[...] = scale * n_row + jnp.sum(k * w_e, axis=1, keepdims=True)
    m_ref[...] = m_new

    mean = _rows(_mm2r, y, hm) * (1.0 / HEAD_DIM)
    d = y - mean
    var = _rows(_mm2r, d * d, hm) * (1.0 / HEAD_DIM)
    y = d * lax.rsqrt(var + NORM_EPS) * hng_ref[...] * jax.nn.sigmoid(o_ref[...].astype(F32))
    y_ref[...] = y.astype(y_ref.dtype)


def _mlstm(proj3, conv_w, conv_b, i_b, f_b, hn_g):
    b, s, _ = proj3.shape
    c = CHUNK
    nc = s // c
    n = N_HEADS * c
    bt = _batch_tile(b)
    gbias = jnp.concatenate([i_b, f_b, jnp.zeros((LANE - 2 * N_HEADS,), F32)]).reshape(1, LANE)
    row = lambda bi, ci: (bi, ci, 0)
    const = lambda bi, ci: (0, 0)
    return pl.pallas_call(
        _mlstm_body,
        grid=(b // bt, nc),
        in_specs=[
            pl.BlockSpec((bt, c, 2 * RW), lambda bi, ci: (bi, ci, COL_MLSTM_QK // (2 * RW))),
            pl.BlockSpec((bt, c, RW), lambda bi, ci: (bi, ci, COL_MLSTM_V // RW)),
            pl.BlockSpec((bt, c, RW), lambda bi, ci: (bi, ci, COL_MLSTM_O // RW)),
            pl.BlockSpec((bt, c, LANE), lambda bi, ci: (bi, ci, COL_MLSTM_G // LANE)),
            pl.BlockSpec(conv_w.shape, const),
            pl.BlockSpec((1, 2 * RW), const),
            pl.BlockSpec((1, LANE), const),
            pl.BlockSpec((1, RW), const),
        ],
        out_specs=pl.BlockSpec((bt, c, RW), row),
        out_shape=jax.ShapeDtypeStruct((b, s, RW), BF16),
        scratch_shapes=[pltpu.VMEM((bt, n, n), F32), pltpu.VMEM((bt, 1, n), F32),
                        pltpu.VMEM((bt, 1, n), F32), pltpu.VMEM((bt, SUBLANE, 2 * RW), F32)],
        compiler_params=_cparams(("parallel", "arbitrary")),
        name="mlstm_chunk",
    )(proj3, proj3, proj3, proj3, conv_w, conv_b.reshape(1, -1), gbias,
      hn_g.reshape(1, -1))


def _merge_body(x_ref, ya_ref, yb_ref, yc_ref, ga_ref, gb_ref, gc_ref, wa_ref, wb_ref, wc_ref,
                wo_ref, g_ref, o_ref):
    merged = (jax.nn.sigmoid(ga_ref[...].astype(F32)) * _mm1(ya_ref[...], wa_ref[...])
              + jax.nn.sigmoid(gb_ref[...].astype(F32)) * _mm1(yb_ref[...], wb_ref[...])
              + jax.nn.sigmoid(gc_ref[...].astype(F32)) * _mm1(yc_ref[...], wc_ref[...]))
    o_ref[...] = x_ref[...] + _rms(_mm1(merged, wo_ref[...]), g_ref[...])


def _merge(x2d, proj, ya, yb, yc, wa, wb, wc, wo, g, tm):
    t, d = x2d.shape
    row = lambda i: (i, 0)
    const = lambda i: (0, 0)
    return pl.pallas_call(
        _merge_body,
        grid=(t // tm,),
        in_specs=[
            pl.BlockSpec((tm, d), row),
            pl.BlockSpec((tm, ya.shape[1]), row),
            pl.BlockSpec((tm, yb.shape[1]), row),
            pl.BlockSpec((tm, yc.shape[1]), row),
            pl.BlockSpec((tm, d), lambda i: (i, 0)),
            pl.BlockSpec((tm, d), lambda i: (i, 1)),
            pl.BlockSpec((tm, d), lambda i: (i, 2)),
            pl.BlockSpec(wa.shape, const),
            pl.BlockSpec(wb.shape, const),
            pl.BlockSpec(wc.shape, const),
            pl.BlockSpec(wo.shape, const),
            pl.BlockSpec((1, d), const),
        ],
        out_specs=pl.BlockSpec((tm, d), row),
        out_shape=jax.ShapeDtypeStruct((t, d), F32),
        compiler_params=_cparams(("parallel",)),
        name="merge_out",
    )(x2d, ya, yb, yc, proj, proj, proj, wa, wb, wc, wo, g.reshape(1, d))


FFN_ROW_GROUPS = 2


def _ffn_body(nt_seq, n_ff, x_ref, g1_ref, upg_ref, upv_ref, cwg_ref, cwv_ref, cbg_ref, cbv_ref,
              down_ref, g2_ref, g3_ref, pg_ref, pp_ref, p_ref, o_ref, h_ref, acc_ref, ugp_ref, uvp_ref):
    i = pl.program_id(0)
    j = pl.program_id(1)
    tm = x_ref.shape[0]

    @pl.when(j == 0)
    def _():
        h_ref[...] = _rms(x_ref[...], g1_ref[...]).astype(BF16)
        acc_ref[...] = jnp.zeros_like(acc_ref)

    @pl.when(i % nt_seq == 0)
    def _():
        ugp_ref[j] = jnp.zeros(ugp_ref.shape[1:], F32)
        uvp_ref[j] = jnp.zeros(uvp_ref.shape[1:], F32)

    ngrp = FFN_ROW_GROUPS if tm % (FFN_ROW_GROUPS * SUBLANE) == 0 else 1
    rg = tm // ngrp
    upg = upg_ref[...]
    upv = upv_ref[...]
    hs = [h_ref[r * rg:(r + 1) * rg, :] for r in range(ngrp)]
    ug = [_dot(hr, upg) for hr in hs]
    uv = [_dot(hr, upv) for hr in hs]
    pg = [ugp_ref[j]] + [ug[r][rg - SUBLANE:, :] for r in range(ngrp - 1)]
    pv = [uvp_ref[j]] + [uv[r][rg - SUBLANE:, :] for r in range(ngrp - 1)]
    ugp_ref[j] = ug[-1][rg - SUBLANE:, :]
    uvp_ref[j] = uv[-1][rg - SUBLANE:, :]
    cwg = cwg_ref[...]
    cwv = cwv_ref[...]
    down = down_ref[...]
    for r in range(ngrp):
        cg = (cbg_ref[...] + cwg[2:3] * ug[r] + cwg[1:2] * _lagged(ug[r], pg[r], 1)
              + cwg[0:1] * _lagged(ug[r], pg[r], 2))
        cv = (cbv_ref[...] + cwv[2:3] * uv[r] + cwv[1:2] * _lagged(uv[r], pv[r], 1)
              + cwv[0:1] * _lagged(uv[r], pv[r], 2))
        act = jax.nn.gelu(cg, approximate=True) * cv
        acc_ref[r * rg:(r + 1) * rg, :] += _mm1(act, down)

    @pl.when(j == n_ff - 1)
    def _():
        x2 = x_ref[...] + _rms(acc_ref[...], g2_ref[...])
        gate = jax.nn.sigmoid(_mm1(_rms(x2, g3_ref[...]), pg_ref[...]))
        o_ref[...] = x2 + gate * _mm1(p_ref[...], pp_ref[...])


def _ffn(x2d, p2d, s, g1, up, cw, cb, down, g2, g3, pgate, pproj, tm, tf):
    t, d = x2d.shape
    dff = down.shape[0]
    n_ff = dff // tf
    nt_seq = s // tm
    ple = p2d.shape[1]
    row = lambda i, j: (i, 0)
    const = lambda i, j: (0, 0)
    cb2 = cb.reshape(1, -1)
    return pl.pallas_call(
        functools.partial(_ffn_body, nt_seq, n_ff),
        grid=(t // tm, n_ff),
        in_specs=[
            pl.BlockSpec((tm, d), row),
            pl.BlockSpec((1, d), const),
            pl.BlockSpec((d, tf), lambda i, j: (0, j)),
            pl.BlockSpec((d, tf), lambda i, j: (0, n_ff + j)),
            pl.BlockSpec((cw.shape[0], tf), lambda i, j: (0, j)),
            pl.BlockSpec((cw.shape[0], tf), lambda i, j: (0, n_ff + j)),
            pl.BlockSpec((1, tf), lambda i, j: (0, j)),
            pl.BlockSpec((1, tf), lambda i, j: (0, n_ff + j)),
            pl.BlockSpec((tf, d), lambda i, j: (j, 0)),
            pl.BlockSpec((1, d), const),
            pl.BlockSpec((1, d), const),
            pl.BlockSpec(pgate.shape, const),
            pl.BlockSpec(pproj.shape, const),
            pl.BlockSpec((tm, ple), row),
        ],
        out_specs=pl.BlockSpec((tm, d), row),
        out_shape=jax.ShapeDtypeStruct((t, d), F32),
        scratch_shapes=[pltpu.VMEM((tm, d), BF16), pltpu.VMEM((tm, d), F32),
                        pltpu.VMEM((n_ff, SUBLANE, tf), F32), pltpu.VMEM((n_ff, SUBLANE, tf), F32)],
        compiler_params=_cparams(("arbitrary", "arbitrary")),
        name="ffn_ple",
    )(x2d, g1.reshape(1, d), up, up, cw, cw, cb2, cb2, down, g2.reshape(1, d), g3.reshape(1, d),
      pgate, pproj, p2d)


def _pack_w_in(w):
    d = w.shape[0]
    rwkv = w[:, 0:1024]
    moba = w[:, 1024:2560]
    ml = w[:, 2560:3592]
    gate = w[:, 3592:6664]
    pad = jnp.zeros((d, PACKED_WIDTH - COL_MLSTM_G - 2 * N_HEADS), w.dtype)
    packed = jnp.concatenate([gate, rwkv, moba, ml, pad], axis=1)
    return packed.astype(BF16)


def _row_tile(t, want):
    return want if t % want == 0 else t


def kernel(x, p, ln_mix_pre, ln_mix_post, ln_ffn_pre, ln_ffn_post, ln_ple, w_in, rwkv_mu, rwkv_w0, rwkv_w2, rwkv_a0, rwkv_a2, rwkv_g2, rwkv_k_k, rwkv_k_a, rwkv_r_k, rwkv_gn_g, rwkv_gn_b, rwkv_v0, rwkv_v1, rwkv_v2, mlstm_conv_w, mlstm_conv_b, mlstm_i_b, mlstm_f_b, mlstm_hn_g, w_br_rwkv, w_br_moba, w_br_mlstm, w_out, ffn_up, ffn_conv_w, ffn_conv_b, ffn_down, ple_proj, ple_gate):
    b, s, d = x.shape
    depth = w_in.shape[0]
    t = b * s
    assert d == 1024 and s % MOBA_BLOCK == 0 and w_in.shape[2] == 6664
    xf = x.reshape(t, d)
    tm_proj = _row_tile(t, 2048)
    tm_ffn = min(512, s)
    tm_merge = _row_tile(t, 1024)
    r2 = lambda a: a.reshape(1, -1)
    v_first = None
    for i in range(depth):
        proj = _norm_proj(xf, ln_mix_pre[i], _pack_w_in(w_in[i]), tm_proj, 1024)
        params = [r2(rwkv_mu[i]), r2(rwkv_w0[i]), rwkv_w2[i], r2(rwkv_a0[i]), rwkv_a2[i], rwkv_g2[i],
                  r2(rwkv_k_k[i]), r2(rwkv_k_a[i]), r2(rwkv_r_k[i]), r2(rwkv_gn_g[i]), r2(rwkv_gn_b[i])]
        if i > 0:
            params += [r2(rwkv_v0[i - 1]), rwkv_v1[i - 1], rwkv_v2[i - 1]]
        proj3 = proj.reshape(b, s, -1)
        y_a, v_cur = _rwkv(proj3, params, v_first if i > 0 else None)
        if i == 0:
            v_first = v_cur
        y_b = _moba(proj, b, s)
        y_c = _mlstm(proj3, mlstm_conv_w[i], mlstm_conv_b[i], mlstm_i_b[i], mlstm_f_b[i], mlstm_hn_g[i])
        xf = _merge(xf, proj, y_a.reshape(t, -1), y_b, y_c.reshape(t, -1), w_br_rwkv[i].astype(BF16), w_br_moba[i].astype(BF16),
                    w_br_mlstm[i].astype(BF16), w_out[i].astype(BF16), ln_mix_post[i], tm_merge)
        xf = _ffn(xf, p[i].reshape(t, -1), s, ln_ffn_pre[i], ffn_up[i].astype(BF16), ffn_conv_w[i],
                  ffn_conv_b[i], ffn_down[i].astype(BF16), ln_ffn_post[i], ln_ple[i],
                  ple_gate[i].astype(BF16), ple_proj[i].astype(BF16), tm_ffn, 1408)
    return xf.reshape(b, s, d)
```

```python
import functools

import jax
import jax.numpy as jnp
from jax import lax
from jax.experimental import pallas as pl
from jax.experimental.pallas import tpu as pltpu

F32 = jnp.float32
BF16 = jnp.bfloat16

HEAD_DIM = 64
N_HEADS = 4
RW = N_HEADS * HEAD_DIM
BATCH_TILE = 8
CHUNK = HEAD_DIM
HEAD_SHIFT = HEAD_DIM.bit_length() - 1
MOBA_BLOCK = 256
MOBA_TOPK = 3
MOBA_QTILE_BLOCKS = 2
MOBA_PAIRS = 2
MOBA_DIAG_CHAINS = 8
NORM_EPS = 1e-6
RWKV_GN_EPS = 64e-5
MASK_VALUE = -1e30
LOG2E = 1.4426950408889634
LANE = 128
SUBLANE = 8
VMEM_LIMIT = 56 * 1024 * 1024

COL_RWKV = 3072
COL_MOBA = 4096
COL_MLSTM_QK = 5632
COL_MLSTM_V = 6144
COL_MLSTM_O = 6400
COL_MLSTM_G = 6656
PACKED_WIDTH = 7168

_DIMS = {
    "nn": (((1,), (0,)), ((), ())),
    "nt": (((1,), (1,)), ((), ())),
    "tn": (((0,), (0,)), ((), ())),
}


def _dot(a, b, dims="nn"):
    return lax.dot_general(a, b, _DIMS[dims], preferred_element_type=F32)


def _split(a):
    hi = a.astype(BF16)
    lo = (a - hi.astype(F32)).astype(BF16)
    return hi, lo


def _mm1(a, b, dims="nn"):
    return _dot(a.astype(BF16), b.astype(BF16), dims)


def _mm3(a, b, dims="nn"):
    ah, al = _split(a)
    bh, bl = _split(b)
    return _dot(ah, bh, dims) + (_dot(ah, bl, dims) + _dot(al, bh, dims))


def _mm2r(a, e, dims="nn"):
    ah, al = _split(a)
    eb = e.astype(BF16)
    return _dot(ah, eb, dims) + _dot(al, eb, dims)


def _mm2l(e, a, dims="nn"):
    ah, al = _split(a)
    eb = e.astype(BF16)
    return _dot(eb, ah, dims) + _dot(eb, al, dims)


def _softplus(x):
    return jnp.maximum(x, 0.0) + jnp.log(1.0 + jnp.exp(-jnp.abs(x)))


def _rms(x, g):
    ms = jnp.mean(x * x, axis=-1, keepdims=True)
    return x * lax.rsqrt(ms + NORM_EPS) * g


def _lagged(x, prev8, lag):
    full = jnp.concatenate([prev8, x], axis=0)
    return pltpu.roll(full, lag, 0)[SUBLANE:, :]


def _tile_heads(x):
    return jnp.concatenate([x] * N_HEADS, axis=-2)


def _per(fn, *arrs):
    return jnp.stack([fn(*(a[i] for a in arrs)) for i in range(arrs[0].shape[0])])


def _rows(fn, x, w):
    lead = x.shape[:-1]
    return fn(x.reshape(-1, x.shape[-1]), w).reshape(*lead, -1)


def _bmm1(a, b, dims="nn"):
    return _per(lambda x, y: _mm1(x, y, dims), a, b)


def _batch_tile(b):
    return BATCH_TILE if b % BATCH_TILE == 0 else 1


def _same_head(n):
    ri = lax.broadcasted_iota(jnp.int32, (n, n), 0)
    ci = lax.broadcasted_iota(jnp.int32, (n, n), 1)
    return (ri >> HEAD_SHIFT) == (ci >> HEAD_SHIFT)


def _lane_head_pos(rows, n):
    lane = lax.broadcasted_iota(jnp.int32, (rows, n), 1)
    return lane >> HEAD_SHIFT, lane & (HEAD_DIM - 1)


def _cparams(sem):
    return pltpu.CompilerParams(dimension_semantics=sem, vmem_limit_bytes=VMEM_LIMIT)


def _proj_body(x_ref, g_ref, w_ref, o_ref, h_ref):
    @pl.when(pl.program_id(1) == 0)
    def _():
        h_ref[...] = _rms(x_ref[...], g_ref[...]).astype(BF16)

    o_ref[...] = _dot(h_ref[...], w_ref[...]).astype(o_ref.dtype)


def _norm_proj(x2d, g, w, tm, tn):
    t, d = x2d.shape
    n = w.shape[1]
    return pl.pallas_call(
        _proj_body,
        grid=(t // tm, n // tn),
        in_specs=[
            pl.BlockSpec((tm, d), lambda i, j: (i, 0)),
            pl.BlockSpec((1, d), lambda i, j: (0, 0)),
            pl.BlockSpec((d, tn), lambda i, j: (0, j)),
        ],
        out_specs=pl.BlockSpec((tm, tn), lambda i, j: (i, j)),
        out_shape=jax.ShapeDtypeStruct((t, n), BF16),
        scratch_shapes=[pltpu.VMEM((tm, d), BF16)],
        compiler_params=_cparams(("parallel", "arbitrary")),
        name="norm_proj",
    )(x2d, g.reshape(1, d), w)


def _rwkv_body(has_vres, *refs):
    if has_vres:
        (slab_ref, vf_ref, mu_ref, w0_ref, w2_ref, a0_ref, a2_ref, g2_ref, kk_ref, ka_ref,
         rk_ref, gg_ref, gb_ref, v0_ref, v1_ref, v2_ref, y_ref, vo_ref, st_ref, prev_ref) = refs
    else:
        (slab_ref, mu_ref, w0_ref, w2_ref, a0_ref, a2_ref, g2_ref, kk_ref, ka_ref,
         rk_ref, gg_ref, gb_ref, y_ref, vo_ref, st_ref, prev_ref) = refs
    c = CHUNK

    @pl.when(pl.program_id(1) == 0)
    def _():
        st_ref[...] = jnp.zeros_like(st_ref)
        prev_ref[...] = jnp.zeros_like(prev_ref)

    slab = slab_ref[...].astype(F32)
    shifted = _per(lambda x, p8: _lagged(x, p8, 1), slab, prev_ref[...])
    prev_ref[...] = slab[:, c - SUBLANE:, :]
    xs = slab + mu_ref[...] * (shifted - slab)
    r = xs[:, :, 0:RW]
    k = xs[:, :, RW:2 * RW]
    v = xs[:, :, 2 * RW:3 * RW]
    o_w = 3 * RW
    o_a = o_w + w2_ref.shape[0]
    o_g = o_a + a2_ref.shape[0]
    xw = xs[:, :, o_w:o_a]
    xa = xs[:, :, o_a:o_g]
    xg = xs[:, :, o_g:o_g + g2_ref.shape[0]]
    wlog = -_softplus(-(w0_ref[...] + _rows(_mm3, jnp.tanh(xw), w2_ref[...]))) - 0.5
    alr = jax.nn.sigmoid(a0_ref[...] + _rows(_mm1, xa, a2_ref[...]))
    g = _rows(_mm1, jax.nn.sigmoid(xg), g2_ref[...])
    if has_vres:
        mix = jax.nn.sigmoid(v0_ref[...] + _rows(_mm1, _rows(_mm1, v, v1_ref[...]), v2_ref[...]))
        v = v + (vf_ref[...] - v) * mix
    vo_ref[...] = v

    n = N_HEADS * c
    same = _same_head(n)
    hm = same.astype(F32)
    pos = lax.broadcasted_iota(jnp.int32, (c, n), 0)
    _, lane_pos = _lane_head_pos(c, n)

    kk = k * kk_ref[...]
    kk = kk / jnp.maximum(jnp.sqrt(_rows(_mm2r, kk * kk, hm)), 1e-12)
    k2 = k * (1.0 + (alr - 1.0) * ka_ref[...])

    lw = -jnp.exp(wlog)
    ti = lax.broadcasted_iota(jnp.int32, (c, c), 0)
    tj = lax.broadcasted_iota(jnp.int32, (c, c), 1)
    tril = (ti >= tj).astype(F32)
    cs = _per(lambda x: _mm2l(tril, x), lw)
    cs_end = cs[:, c - 1:c, :]
    p_in = jnp.exp(cs)
    p_ex = jnp.exp(cs - lw)
    p_inv = jnp.exp(-cs)
    p_tail = jnp.exp(cs_end - cs)

    hm16 = same.astype(BF16)

    def expand(x):
        return _tile_heads(x.astype(BF16)) * hm16

    a_c = -kk * p_ex
    r_c = r * p_in
    b_bd = expand(kk * alr * p_inv)
    k_bd = expand(k2 * p_inv)
    v_bd = expand(v)
    strict = pos > lane_pos
    incl = pos >= lane_pos
    l_ab = jnp.where(strict, _bmm1(a_c, b_bd, "nt"), 0.0)
    l_ak = jnp.where(strict, _bmm1(a_c, k_bd, "nt"), 0.0)
    m_rb = jnp.where(incl, _bmm1(r_c, b_bd, "nt"), 0.0)
    m_rk = jnp.where(incl, _bmm1(r_c, k_bd, "nt"), 0.0)

    tinv = (pos == lane_pos).astype(F32) + l_ab
    npow = l_ab
    for _ in range(5):
        npow = _bmm1(npow, expand(npow))
        tinv = tinv + _bmm1(tinv, expand(npow))

    st = st_ref[...]
    u = _bmm1(tinv, expand(_bmm1(a_c, st, "nt") + _bmm1(l_ak, v_bd)))
    y = _bmm1(r_c, st, "nt") + _bmm1(m_rb, expand(u)) + _bmm1(m_rk, v_bd)
    upd = _bmm1(u, kk * alr * p_tail, "tn") + _bmm1(v, k2 * p_tail, "tn")
    st_ref[...] = st * p_in[:, c - 1:c, :] + jnp.where(same, upd, 0.0)

    mean = _rows(_mm2r, y, hm) * (1.0 / HEAD_DIM)
    d = y - mean
    var = _rows(_mm2r, d * d, hm) * (1.0 / HEAD_DIM)
    yn = d * lax.rsqrt(var + RWKV_GN_EPS) * gg_ref[...] + gb_ref[...]
    bonus = _rows(_mm2r, r * k2 * rk_ref[...], hm) * v
    y_ref[...] = ((yn + bonus) * g).astype(y_ref.dtype)


def _rwkv(proj3, params, v_first3):
    b, s, _ = proj3.shape
    nc = s // CHUNK
    bt = _batch_tile(b)
    has_vres = v_first3 is not None
    row = lambda bi, ci: (bi, ci, 0)
    const = lambda bi, ci: (0, 0)
    in_specs = [pl.BlockSpec((bt, CHUNK, 1024), lambda bi, ci: (bi, ci, COL_RWKV // 1024))]
    args = [proj3]
    if has_vres:
        in_specs.append(pl.BlockSpec((bt, CHUNK, RW), row))
        args.append(v_first3)
    for prm in params:
        in_specs.append(pl.BlockSpec(prm.shape, const))
        args.append(prm)
    n = N_HEADS * CHUNK
    return pl.pallas_call(
        functools.partial(_rwkv_body, has_vres),
        grid=(b // bt, nc),
        in_specs=in_specs,
        out_specs=[pl.BlockSpec((bt, CHUNK, RW), row), pl.BlockSpec((bt, CHUNK, RW), row)],
        out_shape=[jax.ShapeDtypeStruct((b, s, RW), BF16), jax.ShapeDtypeStruct((b, s, RW), F32)],
        scratch_shapes=[pltpu.VMEM((bt, n, n), F32), pltpu.VMEM((bt, SUBLANE, 1024), F32)],
        compiler_params=_cparams(("parallel", "arbitrary")),
        name="rwkv7_chunk",
    )(*args)


def _moba_body(nb, n_sel, q_ref, k_ref, v_ref, o_ref, m_ref, acc_ref, qa_ref):
    bl = MOBA_BLOCK
    nh = 2 * (q_ref.shape[1] // LANE)
    qb = MOBA_QTILE_BLOCKS if nb % MOBA_QTILE_BLOCKS == 0 else 1
    qt = qb * bl
    nt = nb // qb
    nbp = -(-nb // SUBLANE) * SUBLANE
    scale = HEAD_DIM ** -0.5
    lane = lax.broadcasted_iota(jnp.int32, (bl, LANE), 1)
    head0 = lane < HEAD_DIM
    hmask = (head0, jnp.logical_not(head0))
    spare = (lane - HEAD_DIM, lane)
    blk = lax.broadcasted_iota(jnp.int32, (nbp, bl), 0)
    blk_f = blk.astype(F32)
    qpos = lax.broadcasted_iota(jnp.int32, (bl, bl), 0)
    kpos = lax.broadcasted_iota(jnp.int32, (bl, bl), 1)
    causal = kpos <= qpos
    e_row = lax.broadcasted_iota(jnp.int32, (nbp, LANE), 0)
    e_lane = lax.broadcasted_iota(jnp.int32, (nbp, LANE), 1)
    place = ((e_lane == e_row + HEAD_DIM).astype(BF16), (e_lane == e_row).astype(BF16))
    klane = e_lane < HEAD_DIM

    def pair(h):
        return slice((h // 2) * LANE, (h // 2 + 1) * LANE)

    kmean = jnp.mean(k_ref[...].astype(F32).reshape(nb, bl, k_ref.shape[1]), axis=1)
    if nbp > nb:
        kmean = jnp.concatenate([kmean, jnp.zeros((nbp - nb, kmean.shape[1]), F32)], axis=0)
    kmean_h = [jnp.where(klane == (h % 2 == 0), kmean[:, pair(h)], 0.0) for h in range(nh)]

    def rows(i):
        if isinstance(i, int):
            return pl.ds(i * bl, bl)
        return pl.ds(pl.multiple_of(i * bl, bl), bl)

    def kv_tiles(j):
        kp, vp = [], []
        for h in range(nh):
            kb = k_ref[rows(j), pair(h)].astype(F32) * (scale * LOG2E)
            vb = v_ref[rows(j), pair(h)]
            kp.append(jnp.where(hmask[h % 2], kb, (spare[h % 2] == j).astype(F32)).astype(BF16))
            vp.append(jnp.where(hmask[h % 2], vb, 1.0).astype(BF16))
        return kp, vp

    dg = max(1, MOBA_DIAG_CHAINS // nh)
    while nb % dg:
        dg -= 1

    def diag_body(ii, carry):
        chains = [(g, h) for g in range(dg) for h in range(nh)]
        kv = [kv_tiles(ii * dg + g) for g in range(dg)]
        qf = [q_ref[rows(ii * dg + g), pair(h)].astype(F32) for g, h in chains]
        qh = [jnp.where(hmask[h % 2], qf[c], 0.0) for c, (g, h) in enumerate(chains)]
        s = [_dot(qh[c].astype(BF16), kv[g][0][h], "nt") for c, (g, h) in enumerate(chains)]
        bs = [jnp.where(blk < ii * dg + g, _mm3(kmean_h[h], qf[c], "nt"), MASK_VALUE)
              for c, (g, h) in enumerate(chains)]
        sel_t = []
        for c, (g, h) in enumerate(chains):
            work = bs[c]
            sel = jnp.zeros((nbp, bl), jnp.bool_)
            for _ in range(n_sel):
                best = jnp.max(work, axis=0, keepdims=True)
                first = jnp.min(jnp.where(work == best, blk_f, float(nbp)), axis=0, keepdims=True)
                hit = blk_f == first
                sel = sel | hit
                work = jnp.where(hit, -jnp.inf, work)
            sel_t.append((sel & (blk < ii * dg + g)).astype(BF16))
        picked = [_dot(sel_t[c], place[h % 2], "tn") for c, (g, h) in enumerate(chains)]
        for c, (g, h) in enumerate(chains):
            i = ii * dg + g
            sm = jnp.where(causal, s[c], MASK_VALUE)
            m = jnp.max(sm, axis=1, keepdims=True)
            p = jnp.exp2(sm - m)
            m_ref[h, rows(i), :] = jnp.broadcast_to(m, (bl, LANE))
            acc_ref[h, rows(i), :] = _dot(p.astype(BF16), kv[g][1][h])
            sp = spare[h % 2]
            bias = jnp.where((sp >= 0) & (sp < nb) & (picked[c] < 0.5), MASK_VALUE, 0.0)
            qa_ref[h, rows(i), :] = (qh[c] + bias).astype(BF16)
        return carry

    lax.fori_loop(0, nb // dg, diag_body, 0)

    def scores(t, h, kp):
        return _dot(qa_ref[h, pl.ds(t * qt, qt), :], kp[h], "nt")

    def update(t, h, s, vp):
        rs = pl.ds(t * qt, qt)
        m_prev = m_ref[h, rs, :]
        m_new = jnp.maximum(m_prev, jnp.max(s, axis=1, keepdims=True))
        p = jnp.exp2(s - jnp.concatenate([m_new, m_new], axis=1))
        m_ref[h, rs, :] = m_new
        acc_ref[h, rs, :] = acc_ref[h, rs, :] * jnp.exp2(m_prev - m_new) + _dot(p.astype(BF16), vp[h])

    for t0 in range(nt):
        def key_body(j, carry, t0=t0):
            kp, vp = kv_tiles(j)
            items = [(t, h) for t in range(t0, nt) for h in range(nh)]
            s_next = scores(*items[0], kp)
            for idx, (t, h) in enumerate(items):
                s = s_next
                if idx + 1 < len(items):
                    s_next = scores(*items[idx + 1], kp)
                update(t, h, s, vp)
            return carry

        lax.fori_loop(max(0, t0 * qb - 1), min(nb - 1, (t0 + 1) * qb - 1), key_body, 0)

    def out_body(t, carry):
        rs = pl.ds(pl.multiple_of(t * qt, qt), qt)
        first = lax.broadcasted_iota(jnp.int32, (qt, LANE), 1) < HEAD_DIM
        for h in range(0, nh, 2):
            a0 = acc_ref[h, rs, :]
            a1 = acc_ref[h + 1, rs, :]
            out = jnp.where(first, a0 / pltpu.roll(a0, HEAD_DIM, 1), a1 / pltpu.roll(a1, HEAD_DIM, 1))
            o_ref[rs, pair(h)] = out.astype(o_ref.dtype)
        return carry

    lax.fori_loop(0, nt, out_body, 0)


def _moba(proj, b, s):
    t = proj.shape[0]
    nb = s // MOBA_BLOCK
    n_sel = min(MOBA_TOPK, nb - 1)
    w = MOBA_PAIRS * LANE
    ngrp = 512 // w
    qc, kc, vc = COL_MOBA // w, (COL_MOBA + 512) // w, (COL_MOBA + 1024) // w
    nh = 2 * MOBA_PAIRS
    return pl.pallas_call(
        functools.partial(_moba_body, nb, n_sel),
        grid=(b, ngrp),
        in_specs=[
            pl.BlockSpec((s, w), lambda bi, g: (bi, qc + g)),
            pl.BlockSpec((s, w), lambda bi, g: (bi, kc + g)),
            pl.BlockSpec((s, w), lambda bi, g: (bi, vc + g)),
        ],
        out_specs=pl.BlockSpec((s, w), lambda bi, g: (bi, g)),
        out_shape=jax.ShapeDtypeStruct((t, 512), BF16),
        scratch_shapes=[pltpu.VMEM((nh, s, LANE), F32), pltpu.VMEM((nh, s, LANE), F32),
                        pltpu.VMEM((nh, s, LANE), BF16)],
        compiler_params=_cparams(("parallel", "parallel")),
        name="moba_attn",
    )(proj, proj, proj)


def _mlstm_body(qk_ref, v_ref, o_ref, gate_ref, cw_ref, cb_ref, gbias_ref, hng_ref,
                y_ref, cst_ref, n_ref, m_ref, prev_ref):
    c = CHUNK
    n = N_HEADS * c

    @pl.when(pl.program_id(1) == 0)
    def _():
        cst_ref[...] = jnp.zeros_like(cst_ref)
        n_ref[...] = jnp.zeros_like(n_ref)
        m_ref[...] = jnp.zeros_like(m_ref)
        prev_ref[...] = jnp.zeros_like(prev_ref)

    x = qk_ref[...].astype(F32)
    prev8 = prev_ref[...]
    cw = cw_ref[...]
    lag = lambda j: _per(lambda xx, p8: _lagged(xx, p8, j), x, prev8)
    conv = cb_ref[...] + cw[3:4] * x + cw[2:3] * lag(1) + cw[1:2] * lag(2) + cw[0:1] * lag(3)
    prev_ref[...] = x[:, c - SUBLANE:, :]
    qk = conv * jax.nn.sigmoid(conv)
    q = qk[:, :, 0:RW]
    k = qk[:, :, RW:2 * RW] * (HEAD_DIM ** -0.5)
    v = v_ref[...].astype(F32)

    same = _same_head(n)
    hm = same.astype(F32)
    hm16 = same.astype(BF16)
    k_mt = _tile_heads(k.astype(BF16)) * hm16
    v_mt = _tile_heads(v.astype(BF16)) * hm16

    g = gate_ref[...].astype(F32) + gbias_ref[...]
    glane = lax.broadcasted_iota(jnp.int32, (c, LANE), 1)
    lg = jnp.where(glane < N_HEADS, g, -_softplus(-g))
    ti = lax.broadcasted_iota(jnp.int32, (c, c), 0)
    tj = lax.broadcasted_iota(jnp.int32, (c, c), 1)
    tril = (ti >= tj).astype(F32)
    cum = _per(lambda z: _mm2l(tril, z), lg)
    src = lax.broadcasted_iota(jnp.int32, (LANE, n), 0)
    dst_head, _ = _lane_head_pos(LANE, n)
    li_e = _rows(_mm2r, lg, (src == dst_head).astype(F32))
    b_e = _rows(_mm2r, cum, (src == dst_head + N_HEADS).astype(F32))
    pos = lax.broadcasted_iota(jnp.int32, (c, n), 0)
    lane_head, lane_pos = _lane_head_pos(c, n)
    li_row = jnp.sum(jnp.where(pos == lane_pos, li_e, 0.0), axis=1, keepdims=True)
    b_row = jnp.sum(jnp.where(pos == lane_pos, b_e, 0.0), axis=1, keepdims=True)
    bend = b_e[:, c - 1:c, :]

    m_row = m_ref[...]
    dmat = jnp.where(pos >= lane_pos, b_e - b_row + li_row, -jnp.inf)
    inter = b_e + m_row
    m_t = inter
    for h in range(N_HEADS):
        in_head = lane_head == h
        mh = jnp.max(jnp.where(in_head, dmat, -jnp.inf), axis=2, keepdims=True)
        m_t = jnp.where(in_head, jnp.maximum(m_t, mh), m_t)
    wts = jnp.exp(dmat - m_t)
    s_inter = jnp.exp(inter - m_t)
    cst = cst_ref[...]
    n_row = n_ref[...]
    qk_w = _bmm1(q, k_mt, "nt") * wts
    num = s_inter * _bmm1(q, cst, "nt") + _bmm1(qk_w, v_mt)
    den = _rows(_mm2r, s_inter * (q * n_row) + qk_w, hm)
    y = num / jnp.maximum(jnp.abs(den), jnp.exp(-m_t))

    g_e = bend - b_e + li_e
    m_new = jnp.maximum(bend + m_row, jnp.max(g_e, axis=1, keepdims=True))
    w_e = jnp.exp(g_e - m_new)
    scale = jnp.exp(bend + m_row - m_new)
    cst_ref[...] = scale * cst + jnp.where(same, _bmm1(v * w_e, k, "tn"), 0.0)
    n_ref[...] = scale * n_row + jnp.sum(k * w_e, axis=1, keepdims=True)
    m_ref[...] = m_new

    mean = _rows(_mm2r, y, hm) * (1.0 / HEAD_DIM)
    d = y - mean
    var = _rows(_mm2r, d * d, hm) * (1.0 / HEAD_DIM)
    y = d * lax.rsqrt(var + NORM_EPS) * hng_ref[...] * jax.nn.sigmoid(o_ref[...].astype(F32))
    y_ref[...] = y.astype(y_ref.dtype)


def _mlstm(proj3, conv_w, conv_b, i_b, f_b, hn_g):
    b, s, _ = proj3.shape
    c = CHUNK
    nc = s // c
    n = N_HEADS * c
    bt = _batch_tile(b)
    gbias = jnp.concatenate([i_b, f_b, jnp.zeros((LANE - 2 * N_HEADS,), F32)]).reshape(1, LANE)
    row = lambda bi, ci: (bi, ci, 0)
    const = lambda bi, ci: (0, 0)
    return pl.pallas_call(
        _mlstm_body,
        grid=(b // bt, nc),
        in_specs=[
            pl.BlockSpec((bt, c, 2 * RW), lambda bi, ci: (bi, ci, COL_MLSTM_QK // (2 * RW))),
            pl.BlockSpec((bt, c, RW), lambda bi, ci: (bi, ci, COL_MLSTM_V // RW)),
            pl.BlockSpec((bt, c, RW), lambda bi, ci: (bi, ci, COL_MLSTM_O // RW)),
            pl.BlockSpec((bt, c, LANE), lambda bi, ci: (bi, ci, COL_MLSTM_G // LANE)),
            pl.BlockSpec(conv_w.shape, const),
            pl.BlockSpec((1, 2 * RW), const),
            pl.BlockSpec((1, LANE), const),
            pl.BlockSpec((1, RW), const),
        ],
        out_specs=pl.BlockSpec((bt, c, RW), row),
        out_shape=jax.ShapeDtypeStruct((b, s, RW), BF16),
        scratch_shapes=[pltpu.VMEM((bt, n, n), F32), pltpu.VMEM((bt, 1, n), F32),
                        pltpu.VMEM((bt, 1, n), F32), pltpu.VMEM((bt, SUBLANE, 2 * RW), F32)],
        compiler_params=_cparams(("parallel", "arbitrary")),
        name="mlstm_chunk",
    )(proj3, proj3, proj3, proj3, conv_w, conv_b.reshape(1, -1), gbias,
      hn_g.reshape(1, -1))


def _merge_body(x_ref, ya_ref, yb_ref, yc_ref, ga_ref, gb_ref, gc_ref, wa_ref, wb_ref, wc_ref,
                wo_ref, g_ref, o_ref):
    merged = (jax.nn.sigmoid(ga_ref[...].astype(F32)) * _mm1(ya_ref[...], wa_ref[...])
              + jax.nn.sigmoid(gb_ref[...].astype(F32)) * _mm1(yb_ref[...], wb_ref[...])
              + jax.nn.sigmoid(gc_ref[...].astype(F32)) * _mm1(yc_ref[...], wc_ref[...]))
    o_ref[...] = x_ref[...] + _rms(_mm1(merged, wo_ref[...]), g_ref[...])


def _merge(x2d, proj, ya, yb, yc, wa, wb, wc, wo, g, tm):
    t, d = x2d.shape
    row = lambda i: (i, 0)
    const = lambda i: (0, 0)
    return pl.pallas_call(
        _merge_body,
        grid=(t // tm,),
        in_specs=[
            pl.BlockSpec((tm, d), row),
            pl.BlockSpec((tm, ya.shape[1]), row),
            pl.BlockSpec((tm, yb.shape[1]), row),
            pl.BlockSpec((tm, yc.shape[1]), row),
            pl.BlockSpec((tm, d), lambda i: (i, 0)),
            pl.BlockSpec((tm, d), lambda i: (i, 1)),
            pl.BlockSpec((tm, d), lambda i: (i, 2)),
            pl.BlockSpec(wa.shape, const),
            pl.BlockSpec(wb.shape, const),
            pl.BlockSpec(wc.shape, const),
            pl.BlockSpec(wo.shape, const),
            pl.BlockSpec((1, d), const),
        ],
        out_specs=pl.BlockSpec((tm, d), row),
        out_shape=jax.ShapeDtypeStruct((t, d), F32),
        compiler_params=_cparams(("parallel",)),
        name="merge_out",
    )(x2d, ya, yb, yc, proj, proj, proj, wa, wb, wc, wo, g.reshape(1, d))


FFN_ROW_GROUPS = 2


def _ffn_body(nt_seq, n_ff, x_ref, g1_ref, upg_ref, upv_ref, cwg_ref, cwv_ref, cbg_ref, cbv_ref,
              down_ref, g2_ref, g3_ref, pg_ref, pp_ref, p_ref, o_ref, h_ref, acc_ref, ugp_ref, uvp_ref):
    i = pl.program_id(0)
    j = pl.program_id(1)
    tm = x_ref.shape[0]

    @pl.when(j == 0)
    def _():
        h_ref[...] = _rms(x_ref[...], g1_ref[...]).astype(BF16)
        acc_ref[...] = jnp.zeros_like(acc_ref)

    @pl.when(i % nt_seq == 0)
    def _():
        ugp_ref[j] = jnp.zeros(ugp_ref.shape[1:], F32)
        uvp_ref[j] = jnp.zeros(uvp_ref.shape[1:], F32)

    ngrp = FFN_ROW_GROUPS if tm % (FFN_ROW_GROUPS * SUBLANE) == 0 else 1
    rg = tm // ngrp
    upg = upg_ref[...]
    upv = upv_ref[...]
    hs = [h_ref[r * rg:(r + 1) * rg, :] for r in range(ngrp)]
    ug = [_dot(hr, upg) for hr in hs]
    uv = [_dot(hr, upv) for hr in hs]
    pg = [ugp_ref[j]] + [ug[r][rg - SUBLANE:, :] for r in range(ngrp - 1)]
    pv = [uvp_ref[j]] + [uv[r][rg - SUBLANE:, :] for r in range(ngrp - 1)]
    ugp_ref[j] = ug[-1][rg - SUBLANE:, :]
    uvp_ref[j] = uv[-1][rg - SUBLANE:, :]
    cwg = cwg_ref[...]
    cwv = cwv_ref[...]
    down = down_ref[...]
    for r in range(ngrp):
        cg = (cbg_ref[...] + cwg[2:3] * ug[r] + cwg[1:2] * _lagged(ug[r], pg[r], 1)
              + cwg[0:1] * _lagged(ug[r], pg[r], 2))
        cv = (cbv_ref[...] + cwv[2:3] * uv[r] + cwv[1:2] * _lagged(uv[r], pv[r], 1)
              + cwv[0:1] * _lagged(uv[r], pv[r], 2))
        act = jax.nn.gelu(cg, approximate=True) * cv
        acc_ref[r * rg:(r + 1) * rg, :] += _mm1(act, down)

    @pl.when(j == n_ff - 1)
    def _():
        x2 = x_ref[...] + _rms(acc_ref[...], g2_ref[...])
        gate = jax.nn.sigmoid(_mm1(_rms(x2, g3_ref[...]), pg_ref[...]))
        o_ref[...] = x2 + gate * _mm1(p_ref[...], pp_ref[...])


def _ffn(x2d, p2d, s, g1, up, cw, cb, down, g2, g3, pgate, pproj, tm, tf):
    t, d = x2d.shape
    dff = down.shape[0]
    n_ff = dff // tf
    nt_seq = s // tm
    ple = p2d.shape[1]
    row = lambda i, j: (i, 0)
    const = lambda i, j: (0, 0)
    cb2 = cb.reshape(1, -1)
    return pl.pallas_call(
        functools.partial(_ffn_body, nt_seq, n_ff),
        grid=(t // tm, n_ff),
        in_specs=[
            pl.BlockSpec((tm, d), row),
            pl.BlockSpec((1, d), const),
            pl.BlockSpec((d, tf), lambda i, j: (0, j)),
            pl.BlockSpec((d, tf), lambda i, j: (0, n_ff + j)),
            pl.BlockSpec((cw.shape[0], tf), lambda i, j: (0, j)),
            pl.BlockSpec((cw.shape[0], tf), lambda i, j: (0, n_ff + j)),
            pl.BlockSpec((1, tf), lambda i, j: (0, j)),
            pl.BlockSpec((1, tf), lambda i, j: (0, n_ff + j)),
            pl.BlockSpec((tf, d), lambda i, j: (j, 0)),
            pl.BlockSpec((1, d), const),
            pl.BlockSpec((1, d), const),
            pl.BlockSpec(pgate.shape, const),
            pl.BlockSpec(pproj.shape, const),
            pl.BlockSpec((tm, ple), row),
        ],
        out_specs=pl.BlockSpec((tm, d), row),
        out_shape=jax.ShapeDtypeStruct((t, d), F32),
        scratch_shapes=[pltpu.VMEM((tm, d), BF16), pltpu.VMEM((tm, d), F32),
                        pltpu.VMEM((n_ff, SUBLANE, tf), F32), pltpu.VMEM((n_ff, SUBLANE, tf), F32)],
        compiler_params=_cparams(("arbitrary", "arbitrary")),
        name="ffn_ple",
    )(x2d, g1.reshape(1, d), up, up, cw, cw, cb2, cb2, down, g2.reshape(1, d), g3.reshape(1, d),
      pgate, pproj, p2d)


def _pack_w_in(w):
    d = w.shape[0]
    rwkv = w[:, 0:1024]
    moba = w[:, 1024:2560]
    ml = w[:, 2560:3592]
    gate = w[:, 3592:6664]
    pad = jnp.zeros((d, PACKED_WIDTH - COL_MLSTM_G - 2 * N_HEADS), w.dtype)
    packed = jnp.concatenate([gate, rwkv, moba, ml, pad], axis=1)
    return packed.astype(BF16)


def _row_tile(t, want):
    return want if t % want == 0 else t


def kernel(x, p, ln_mix_pre, ln_mix_post, ln_ffn_pre, ln_ffn_post, ln_ple, w_in, rwkv_mu, rwkv_w0, rwkv_w2, rwkv_a0, rwkv_a2, rwkv_g2, rwkv_k_k, rwkv_k_a, rwkv_r_k, rwkv_gn_g, rwkv_gn_b, rwkv_v0, rwkv_v1, rwkv_v2, mlstm_conv_w, mlstm_conv_b, mlstm_i_b, mlstm_f_b, mlstm_hn_g, w_br_rwkv, w_br_moba, w_br_mlstm, w_out, ffn_up, ffn_conv_w, ffn_conv_b, ffn_down, ple_proj, ple_gate):
    b, s, d = x.shape
    depth = w_in.shape[0]
    t = b * s
    assert d == 1024 and s % MOBA_BLOCK == 0 and w_in.shape[2] == 6664
    xf = x.reshape(t, d)
    tm_proj = _row_tile(t, 2048)
    tm_ffn = min(512, s)
    tm_merge = _row_tile(t, 1024)
    r2 = lambda a: a.reshape(1, -1)
    v_first = None
    for i in range(depth):
        proj = _norm_proj(xf, ln_mix_pre[i], _pack_w_in(w_in[i]), tm_proj, 1024)
        params = [r2(rwkv_mu[i]), r2(rwkv_w0[i]), rwkv_w2[i], r2(rwkv_a0[i]), rwkv_a2[i], rwkv_g2[i],
                  r2(rwkv_k_k[i]), r2(rwkv_k_a[i]), r2(rwkv_r_k[i]), r2(rwkv_gn_g[i]), r2(rwkv_gn_b[i])]
        if i > 0:
            params += [r2(rwkv_v0[i - 1]), rwkv_v1[i - 1], rwkv_v2[i - 1]]
        proj3 = proj.reshape(b, s, -1)
        y_a, v_cur = _rwkv(proj3, params, v_first if i > 0 else None)
        if i == 0:
            v_first = v_cur
        y_b = _moba(proj, b, s)
        y_c = _mlstm(proj3, mlstm_conv_w[i], mlstm_conv_b[i], mlstm_i_b[i], mlstm_f_b[i], mlstm_hn_g[i])
        xf = _merge(xf, proj, y_a.reshape(t, -1), y_b, y_c.reshape(t, -1), w_br_rwkv[i].astype(BF16), w_br_moba[i].astype(BF16),
                    w_br_mlstm[i].astype(BF16), w_out[i].astype(BF16), ln_mix_post[i], tm_merge)
        xf = _ffn(xf, p[i].reshape(t, -1), s, ln_ffn_pre[i], ffn_up[i].astype(BF16), ffn_conv_w[i],
                  ffn_conv_b[i], ffn_down[i].astype(BF16), ln_ffn_post[i], ln_ple[i],
                  ple_gate[i].astype(BF16), ple_proj[i].astype(BF16), tm_ffn, 1408)
    return xf.reshape(b, s, d)
```

```python
import functools

import jax
import jax.numpy as jnp
from jax import lax
from jax.experimental import pallas as pl
from jax.experimental.pallas import tpu as pltpu

F32 = jnp.float32
BF16 = jnp.bfloat16

HEAD_DIM = 64
N_HEADS = 4
RW = N_HEADS * HEAD_DIM
BATCH_TILE = 8
CHUNK = HEAD_DIM
HEAD_SHIFT = HEAD_DIM.bit_length() - 1
MOBA_BLOCK = 256
MOBA_TOPK = 3
MOBA_QTILE_BLOCKS = 2
MOBA_PAIRS = 2
MOBA_DIAG_CHAINS = 8
NORM_EPS = 1e-6
RWKV_GN_EPS = 64e-5
MASK_VALUE = -1e30
LOG2E = 1.4426950408889634
LANE = 128
SUBLANE = 8
VMEM_LIMIT = 56 * 1024 * 1024

COL_RWKV = 3072
COL_MOBA = 4096
COL_MLSTM_QK = 5632
COL_MLSTM_V = 6144
COL_MLSTM_O = 6400
COL_MLSTM_G = 6656
PACKED_WIDTH = 7168

_DIMS = {
    "nn": (((1,), (0,)), ((), ())),
    "nt": (((1,), (1,)), ((), ())),
    "tn": (((0,), (0,)), ((), ())),
}


def _dot(a, b, dims="nn"):
    return lax.dot_general(a, b, _DIMS[dims], preferred_element_type=F32)


def _split(a):
    hi = a.astype(BF16)
    lo = (a - hi.astype(F32)).astype(BF16)
    return hi, lo


def _mm1(a, b, dims="nn"):
    return _dot(a.astype(BF16), b.astype(BF16), dims)


def _mm3(a, b, dims="nn"):
    ah, al = _split(a)
    bh, bl = _split(b)
    return _dot(ah, bh, dims) + (_dot(ah, bl, dims) + _dot(al, bh, dims))


def _mm2r(a, e, dims="nn"):
    ah, al = _split(a)
    eb = e.astype(BF16)
    return _dot(ah, eb, dims) + _dot(al, eb, dims)


def _mm2l(e, a, dims="nn"):
    ah, al = _split(a)
    eb = e.astype(BF16)
    return _dot(eb, ah, dims) + _dot(eb, al, dims)


def _softplus(x):
    return jnp.maximum(x, 0.0) + jnp.log(1.0 + jnp.exp(-jnp.abs(x)))


def _rms(x, g):
    ms = jnp.mean(x * x, axis=-1, keepdims=True)
    return x * lax.rsqrt(ms + NORM_EPS) * g


def _lagged(x, prev8, lag):
    full = jnp.concatenate([prev8, x], axis=0)
    return pltpu.roll(full, lag, 0)[SUBLANE:, :]


def _tile_heads(x):
    return jnp.concatenate([x] * N_HEADS, axis=-2)


def _per(fn, *arrs):
    return jnp.stack([fn(*(a[i] for a in arrs)) for i in range(arrs[0].shape[0])])


def _rows(fn, x, w):
    lead = x.shape[:-1]
    return fn(x.reshape(-1, x.shape[-1]), w).reshape(*lead, -1)


def _bmm1(a, b, dims="nn"):
    return _per(lambda x, y: _mm1(x, y, dims), a, b)


def _batch_tile(b):
    return BATCH_TILE if b % BATCH_TILE == 0 else 1


def _same_head(n):
    ri = lax.broadcasted_iota(jnp.int32, (n, n), 0)
    ci = lax.broadcasted_iota(jnp.int32, (n, n), 1)
    return (ri >> HEAD_SHIFT) == (ci >> HEAD_SHIFT)


def _lane_head_pos(rows, n):
    lane = lax.broadcasted_iota(jnp.int32, (rows, n), 1)
    return lane >> HEAD_SHIFT, lane & (HEAD_DIM - 1)


def _cparams(sem):
    return pltpu.CompilerParams(dimension_semantics=sem, vmem_limit_bytes=VMEM_LIMIT)


def _proj_body(x_ref, g_ref, w_ref, o_ref, h_ref):
    @pl.when(pl.program_id(1) == 0)
    def _():
        h_ref[...] = _rms(x_ref[...], g_ref[...]).astype(BF16)

    o_ref[...] = _dot(h_ref[...], w_ref[...]).astype(o_ref.dtype)


def _norm_proj(x2d, g, w, tm, tn):
    t, d = x2d.shape
    n = w.shape[1]
    return pl.pallas_call(
        _proj_body,
        grid=(t // tm, n // tn),
        in_specs=[
            pl.BlockSpec((tm, d), lambda i, j: (i, 0)),
            pl.BlockSpec((1, d), lambda i, j: (0, 0)),
            pl.BlockSpec((d, tn), lambda i, j: (0, j)),
        ],
        out_specs=pl.BlockSpec((tm, tn), lambda i, j: (i, j)),
        out_shape=jax.ShapeDtypeStruct((t, n), BF16),
        scratch_shapes=[pltpu.VMEM((tm, d), BF16)],
        compiler_params=_cparams(("parallel", "arbitrary")),
        name="norm_proj",
    )(x2d, g.reshape(1, d), w)


def _rwkv_body(has_vres, *refs):
    if has_vres:
        (slab_ref, vf_ref, mu_ref, w0_ref, w2_ref, a0_ref, a2_ref, g2_ref, kk_ref, ka_ref,
         rk_ref, gg_ref, gb_ref, v0_ref, v1_ref, v2_ref, y_ref, vo_ref, st_ref, prev_ref) = refs
    else:
        (slab_ref, mu_ref, w0_ref, w2_ref, a0_ref, a2_ref, g2_ref, kk_ref, ka_ref,
         rk_ref, gg_ref, gb_ref, y_ref, vo_ref, st_ref, prev_ref) = refs
    c = CHUNK

    @pl.when(pl.program_id(1) == 0)
    def _():
        st_ref[...] = jnp.zeros_like(st_ref)
        prev_ref[...] = jnp.zeros_like(prev_ref)

    slab = slab_ref[...].astype(F32)
    shifted = _per(lambda x, p8: _lagged(x, p8, 1), slab, prev_ref[...])
    prev_ref[...] = slab[:, c - SUBLANE:, :]
    xs = slab + mu_ref[...] * (shifted - slab)
    r = xs[:, :, 0:RW]
    k = xs[:, :, RW:2 * RW]
    v = xs[:, :, 2 * RW:3 * RW]
    o_w = 3 * RW
    o_a = o_w + w2_ref.shape[0]
    o_g = o_a + a2_ref.shape[0]
    xw = xs[:, :, o_w:o_a]
    xa = xs[:, :, o_a:o_g]
    xg = xs[:, :, o_g:o_g + g2_ref.shape[0]]
    wlog = -_softplus(-(w0_ref[...] + _rows(_mm3, jnp.tanh(xw), w2_ref[...]))) - 0.5
    alr = jax.nn.sigmoid(a0_ref[...] + _rows(_mm1, xa, a2_ref[...]))
    g = _rows(_mm1, jax.nn.sigmoid(xg), g2_ref[...])
    if has_vres:
        mix = jax.nn.sigmoid(v0_ref[...] + _rows(_mm1, _rows(_mm1, v, v1_ref[...]), v2_ref[...]))
        v = v + (vf_ref[...] - v) * mix
    vo_ref[...] = v

    n = N_HEADS * c
    same = _same_head(n)
    hm = same.astype(F32)
    pos = lax.broadcasted_iota(jnp.int32, (c, n), 0)
    _, lane_pos = _lane_head_pos(c, n)

    kk = k * kk_ref[...]
    kk = kk / jnp.maximum(jnp.sqrt(_rows(_mm2r, kk * kk, hm)), 1e-12)
    k2 = k * (1.0 + (alr - 1.0) * ka_ref[...])

    lw = -jnp.exp(wlog)
    ti = lax.broadcasted_iota(jnp.int32, (c, c), 0)
    tj = lax.broadcasted_iota(jnp.int32, (c, c), 1)
    tril = (ti >= tj).astype(F32)
    cs = _per(lambda x: _mm2l(tril, x), lw)
    cs_end = cs[:, c - 1:c, :]
    p_in = jnp.exp(cs)
    p_ex = jnp.exp(cs - lw)
    p_inv = jnp.exp(-cs)
    p_tail = jnp.exp(cs_end - cs)

    hm16 = same.astype(BF16)

    def expand(x):
        return _tile_heads(x.astype(BF16)) * hm16

    a_c = -kk * p_ex
    r_c = r * p_in
    b_bd = expand(kk * alr * p_inv)
    k_bd = expand(k2 * p_inv)
    v_bd = expand(v)
    strict = pos > lane_pos
    incl = pos >= lane_pos
    l_ab = jnp.where(strict, _bmm1(a_c, b_bd, "nt"), 0.0)
    l_ak = jnp.where(strict, _bmm1(a_c, k_bd, "nt"), 0.0)
    m_rb = jnp.where(incl, _bmm1(r_c, b_bd, "nt"), 0.0)
    m_rk = jnp.where(incl, _bmm1(r_c, k_bd, "nt"), 0.0)

    tinv = (pos == lane_pos).astype(F32) + l_ab
    npow = l_ab
    for _ in range(5):
        npow = _bmm1(npow, expand(npow))
        tinv = tinv + _bmm1(tinv, expand(npow))

    st = st_ref[...]
    u = _bmm1(tinv, expand(_bmm1(a_c, st, "nt") + _bmm1(l_ak, v_bd)))
    y = _bmm1(r_c, st, "nt") + _bmm1(m_rb, expand(u)) + _bmm1(m_rk, v_bd)
    upd = _bmm1(u, kk * alr * p_tail, "tn") + _bmm1(v, k2 * p_tail, "tn")
    st_ref[...] = st * p_in[:, c - 1:c, :] + jnp.where(same, upd, 0.0)

    mean = _rows(_mm2r, y, hm) * (1.0 / HEAD_DIM)
    d = y - mean
    var = _rows(_mm2r, d * d, hm) * (1.0 / HEAD_DIM)
    yn = d * lax.rsqrt(var + RWKV_GN_EPS) * gg_ref[...] + gb_ref[...]
    bonus = _rows(_mm2r, r * k2 * rk_ref[...], hm) * v
    y_ref[...] = ((yn + bonus) * g).astype(y_ref.dtype)


def _rwkv(proj3, params, v_first3):
    b, s, _ = proj3.shape
    nc = s // CHUNK
    bt = _batch_tile(b)
    has_vres = v_first3 is not None
    row = lambda bi, ci: (bi, ci, 0)
    const = lambda bi, ci: (0, 0)
    in_specs = [pl.BlockSpec((bt, CHUNK, 1024), lambda bi, ci: (bi, ci, COL_RWKV // 1024))]
    args = [proj3]
    if has_vres:
        in_specs.append(pl.BlockSpec((bt, CHUNK, RW), row))
        args.append(v_first3)
    for prm in params:
        in_specs.append(pl.BlockSpec(prm.shape, const))
        args.append(prm)
    n = N_HEADS * CHUNK
    return pl.pallas_call(
        functools.partial(_rwkv_body, has_vres),
        grid=(b // bt, nc),
        in_specs=in_specs,
        out_specs=[pl.BlockSpec((bt, CHUNK, RW), row), pl.BlockSpec((bt, CHUNK, RW), row)],
        out_shape=[jax.ShapeDtypeStruct((b, s, RW), BF16), jax.ShapeDtypeStruct((b, s, RW), F32)],
        scratch_shapes=[pltpu.VMEM((bt, n, n), F32), pltpu.VMEM((bt, SUBLANE, 1024), F32)],
        compiler_params=_cparams(("parallel", "arbitrary")),
        name="rwkv7_chunk",
    )(*args)


def _moba_body(nb, n_sel, q_ref, k_ref, v_ref, o_ref, m_ref, acc_ref, qa_ref):
    bl = MOBA_BLOCK
    nh = 2 * (q_ref.shape[1] // LANE)
    qb = MOBA_QTILE_BLOCKS if nb % MOBA_QTILE_BLOCKS == 0 else 1
    qt = qb * bl
    nt = nb // qb
    nbp = -(-nb // SUBLANE) * SUBLANE
    scale = HEAD_DIM ** -0.5
    lane = lax.broadcasted_iota(jnp.int32, (bl, LANE), 1)
    head0 = lane < HEAD_DIM
    hmask = (head0, jnp.logical_not(head0))
    spare = (lane - HEAD_DIM, lane)
    blk = lax.broadcasted_iota(jnp.int32, (nbp, bl), 0)
    blk_f = blk.astype(F32)
    qpos = lax.broadcasted_iota(jnp.int32, (bl, bl), 0)
    kpos = lax.broadcasted_iota(jnp.int32, (bl, bl), 1)
    causal = kpos <= qpos
    e_row = lax.broadcasted_iota(jnp.int32, (nbp, LANE), 0)
    e_lane = lax.broadcasted_iota(jnp.int32, (nbp, LANE), 1)
    place = ((e_lane == e_row + HEAD_DIM).astype(BF16), (e_lane == e_row).astype(BF16))
    klane = e_lane < HEAD_DIM

    def pair(h):
        return slice((h // 2) * LANE, (h // 2 + 1) * LANE)

    kmean = jnp.mean(k_ref[...].astype(F32).reshape(nb, bl, k_ref.shape[1]), axis=1)
    if nbp > nb:
        kmean = jnp.concatenate([kmean, jnp.zeros((nbp - nb, kmean.shape[1]), F32)], axis=0)
    kmean_h = [jnp.where(klane == (h % 2 == 0), kmean[:, pair(h)], 0.0) for h in range(nh)]

    def rows(i):
        if isinstance(i, int):
            return pl.ds(i * bl, bl)
        return pl.ds(pl.multiple_of(i * bl, bl), bl)

    def kv_tiles(j):
        kp, vp = [], []
        for h in range(nh):
            kb = k_ref[rows(j), pair(h)].astype(F32) * (scale * LOG2E)
            vb = v_ref[rows(j), pair(h)]
            kp.append(jnp.where(hmask[h % 2], kb, (spare[h % 2] == j).astype(F32)).astype(BF16))
            vp.append(jnp.where(hmask[h % 2], vb, 1.0).astype(BF16))
        return kp, vp

    dg = max(1, MOBA_DIAG_CHAINS // nh)
    while nb % dg:
        dg -= 1

    def diag_body(ii, carry):
        chains = [(g, h) for g in range(dg) for h in range(nh)]
        kv = [kv_tiles(ii * dg + g) for g in range(dg)]
        qf = [q_ref[rows(ii * dg + g), pair(h)].astype(F32) for g, h in chains]
        qh = [jnp.where(hmask[h % 2], qf[c], 0.0) for c, (g, h) in enumerate(chains)]
        s = [_dot(qh[c].astype(BF16), kv[g][0][h], "nt") for c, (g, h) in enumerate(chains)]
        bs = [jnp.where(blk < ii * dg + g, _mm3(kmean_h[h], qf[c], "nt"), MASK_VALUE)
              for c, (g, h) in enumerate(chains)]
        sel_t = []
        for c, (g, h) in enumerate(chains):
            work = bs[c]
            sel = jnp.zeros((nbp, bl), jnp.bool_)
            for _ in range(n_sel):
                best = jnp.max(work, axis=0, keepdims=True)
                first = jnp.min(jnp.where(work == best, blk_f, float(nbp)), axis=0, keepdims=True)
                hit = blk_f == first
                sel = sel | hit
                work = jnp.where(hit, -jnp.inf, work)
            sel_t.append((sel & (blk < ii * dg + g)).astype(BF16))
        picked = [_dot(sel_t[c], place[h % 2], "tn") for c, (g, h) in enumerate(chains)]
        for c, (g, h) in enumerate(chains):
            i = ii * dg + g
            sm = jnp.where(causal, s[c], MASK_VALUE)
            m = jnp.max(sm, axis=1, keepdims=True)
            p = jnp.exp2(sm - m)
            m_ref[h, rows(i), :] = jnp.broadcast_to(m, (bl, LANE))
            acc_ref[h, rows(i), :] = _dot(p.astype(BF16), kv[g][1][h])
            sp = spare[h % 2]
            bias = jnp.where((sp >= 0) & (sp < nb) & (picked[c] < 0.5), MASK_VALUE, 0.0)
            qa_ref[h, rows(i), :] = (qh[c] + bias).astype(BF16)
        return carry

    lax.fori_loop(0, nb // dg, diag_body, 0)

    def scores(t, h, kp):
        return _dot(qa_ref[h, pl.ds(t * qt, qt), :], kp[h], "nt")

    def update(t, h, s, vp):
        rs = pl.ds(t * qt, qt)
        m_prev = m_ref[h, rs, :]
        m_new = jnp.maximum(m_prev, jnp.max(s, axis=1, keepdims=True))
        p = jnp.exp2(s - jnp.concatenate([m_new, m_new], axis=1))
        m_ref[h, rs, :] = m_new
        acc_ref[h, rs, :] = acc_ref[h, rs, :] * jnp.exp2(m_prev - m_new) + _dot(p.astype(BF16), vp[h])

    for t0 in range(nt):
        def key_body(j, carry, t0=t0):
            kp, vp = kv_tiles(j)
            items = [(t, h) for t in range(t0, nt) for h in range(nh)]
            s_next = scores(*items[0], kp)
            for idx, (t, h) in enumerate(items):
                s = s_next
                if idx + 1 < len(items):
                    s_next = scores(*items[idx + 1], kp)
                update(t, h, s, vp)
            return carry

        lax.fori_loop(max(0, t0 * qb - 1), min(nb - 1, (t0 + 1) * qb - 1), key_body, 0)

    def out_body(t, carry):
        rs = pl.ds(pl.multiple_of(t * qt, qt), qt)
        first = lax.broadcasted_iota(jnp.int32, (qt, LANE), 1) < HEAD_DIM
        for h in range(0, nh, 2):
            a0 = acc_ref[h, rs, :]
            a1 = acc_ref[h + 1, rs, :]
            out = jnp.where(first, a0 / pltpu.roll(a0, HEAD_DIM, 1), a1 / pltpu.roll(a1, HEAD_DIM, 1))
            o_ref[rs, pair(h)] = out.astype(o_ref.dtype)
        return carry

    lax.fori_loop(0, nt, out_body, 0)


def _moba(proj, b, s):
    t = proj.shape[0]
    nb = s // MOBA_BLOCK
    n_sel = min(MOBA_TOPK, nb - 1)
    w = MOBA_PAIRS * LANE
    ngrp = 512 // w
    qc, kc, vc = COL_MOBA // w, (COL_MOBA + 512) // w, (COL_MOBA + 1024) // w
    nh = 2 * MOBA_PAIRS
    return pl.pallas_call(
        functools.partial(_moba_body, nb, n_sel),
        grid=(b, ngrp),
        in_specs=[
            pl.BlockSpec((s, w), lambda bi, g: (bi, qc + g)),
            pl.BlockSpec((s, w), lambda bi, g: (bi, kc + g)),
            pl.BlockSpec((s, w), lambda bi, g: (bi, vc + g)),
        ],
        out_specs=pl.BlockSpec((s, w), lambda bi, g: (bi, g)),
        out_shape=jax.ShapeDtypeStruct((t, 512), BF16),
        scratch_shapes=[pltpu.VMEM((nh, s, LANE), F32), pltpu.VMEM((nh, s, LANE), F32),
                        pltpu.VMEM((nh, s, LANE), BF16)],
        compiler_params=_cparams(("parallel", "parallel")),
        name="moba_attn",
    )(proj, proj, proj)


def _mlstm_body(qk_ref, v_ref, o_ref, gate_ref, cw_ref, cb_ref, gbias_ref, hng_ref,
                y_ref, cst_ref, n_ref, m_ref, prev_ref):
    c = CHUNK
    n = N_HEADS * c

    @pl.when(pl.program_id(1) == 0)
    def _():
        cst_ref[...] = jnp.zeros_like(cst_ref)
        n_ref[...] = jnp.zeros_like(n_ref)
        m_ref[...] = jnp.zeros_like(m_ref)
        prev_ref[...] = jnp.zeros_like(prev_ref)

    x = qk_ref[...].astype(F32)
    prev8 = prev_ref[...]
    cw = cw_ref[...]
    lag = lambda j: _per(lambda xx, p8: _lagged(xx, p8, j), x, prev8)
    conv = cb_ref[...] + cw[3:4] * x + cw[2:3] * lag(1) + cw[1:2] * lag(2) + cw[0:1] * lag(3)
    prev_ref[...] = x[:, c - SUBLANE:, :]
    qk = conv * jax.nn.sigmoid(conv)
    q = qk[:, :, 0:RW]
    k = qk[:, :, RW:2 * RW] * (HEAD_DIM ** -0.5)
    v = v_ref[...].astype(F32)

    same = _same_head(n)
    hm = same.astype(F32)
    hm16 = same.astype(BF16)
    k_mt = _tile_heads(k.astype(BF16)) * hm16
    v_mt = _tile_heads(v.astype(BF16)) * hm16

    g = gate_ref[...].astype(F32) + gbias_ref[...]
    glane = lax.broadcasted_iota(jnp.int32, (c, LANE), 1)
    lg = jnp.where(glane < N_HEADS, g, -_softplus(-g))
    ti = lax.broadcasted_iota(jnp.int32, (c, c), 0)
    tj = lax.broadcasted_iota(jnp.int32, (c, c), 1)
    tril = (ti >= tj).astype(F32)
    cum = _per(lambda z: _mm2l(tril, z), lg)
    src = lax.broadcasted_iota(jnp.int32, (LANE, n), 0)
    dst_head, _ = _lane_head_pos(LANE, n)
    li_e = _rows(_mm2r, lg, (src == dst_head).astype(F32))
    b_e = _rows(_mm2r, cum, (src == dst_head + N_HEADS).astype(F32))
    pos = lax.broadcasted_iota(jnp.int32, (c, n), 0)
    lane_head, lane_pos = _lane_head_pos(c, n)
    li_row = jnp.sum(jnp.where(pos == lane_pos, li_e, 0.0), axis=1, keepdims=True)
    b_row = jnp.sum(jnp.where(pos == lane_pos, b_e, 0.0), axis=1, keepdims=True)
    bend = b_e[:, c - 1:c, :]

    m_row = m_ref[...]
    dmat = jnp.where(pos >= lane_pos, b_e - b_row + li_row, -jnp.inf)
    inter = b_e + m_row
    m_t = inter
    for h in range(N_HEADS):
        in_head = lane_head == h
        mh = jnp.max(jnp.where(in_head, dmat, -jnp.inf), axis=2, keepdims=True)
        m_t = jnp.where(in_head, jnp.maximum(m_t, mh), m_t)
    wts = jnp.exp(dmat - m_t)
    s_inter = jnp.exp(inter - m_t)
    cst = cst_ref[...]
    n_row = n_ref[...]
    qk_w = _bmm1(q, k_mt, "nt") * wts
    num = s_inter * _bmm1(q, cst, "nt") + _bmm1(qk_w, v_mt)
    den = _rows(_mm2r, s_inter * (q * n_row) + qk_w, hm)
    y = num / jnp.maximum(jnp.abs(den), jnp.exp(-m_t))

    g_e = bend - b_e + li_e
    m_new = jnp.maximum(bend + m_row, jnp.max(g_e, axis=1, keepdims=True))
    w_e = jnp.exp(g_e - m_new)
    scale = jnp.exp(bend + m_row - m_new)
    cst_ref[...] = scale * cst + jnp.where(same, _bmm1(v * w_e, k, "tn"), 0.0)
    n_ref[...] = scale * n_row + jnp.sum(k * w_e, axis=1, keepdims=True)
    m_ref[...] = m_new

    mean = _rows(_mm2r, y, hm) * (1.0 / HEAD_DIM)
    d = y - mean
    var = _rows(_mm2r, d * d, hm) * (1.0 / HEAD_DIM)
    y = d * lax.rsqrt(var + NORM_EPS) * hng_ref[...] * jax.nn.sigmoid(o_ref[...].astype(F32))
    y_ref[...] = y.astype(y_ref.dtype)


def _mlstm(proj3, conv_w, conv_b, i_b, f_b, hn_g):
    b, s, _ = proj3.shape
    c = CHUNK
    nc = s // c
    n = N_HEADS * c
    bt = _batch_tile(b)
    gbias = jnp.concatenate([i_b, f_b, jnp.zeros((LANE - 2 * N_HEADS,), F32)]).reshape(1, LANE)
    row = lambda bi, ci: (bi, ci, 0)
    const = lambda bi, ci: (0, 0)
    return pl.pallas_call(
        _mlstm_body,
        grid=(b // bt, nc),
        in_specs=[
            pl.BlockSpec((bt, c, 2 * RW), lambda bi, ci: (bi, ci, COL_MLSTM_QK // (2 * RW))),
            pl.BlockSpec((bt, c, RW), lambda bi, ci: (bi, ci, COL_MLSTM_V // RW)),
            pl.BlockSpec((bt, c, RW), lambda bi, ci: (bi, ci, COL_MLSTM_O // RW)),
            pl.BlockSpec((bt, c, LANE), lambda bi, ci: (bi, ci, COL_MLSTM_G // LANE)),
            pl.BlockSpec(conv_w.shape, const),
            pl.BlockSpec((1, 2 * RW), const),
            pl.BlockSpec((1, LANE), const),
            pl.BlockSpec((1, RW), const),
        ],
        out_specs=pl.BlockSpec((bt, c, RW), row),
        out_shape=jax.ShapeDtypeStruct((b, s, RW), BF16),
        scratch_shapes=[pltpu.VMEM((bt, n, n), F32), pltpu.VMEM((bt, 1, n), F32),
                        pltpu.VMEM((bt, 1, n), F32), pltpu.VMEM((bt, SUBLANE, 2 * RW), F32)],
        compiler_params=_cparams(("parallel", "arbitrary")),
        name="mlstm_chunk",
    )(proj3, proj3, proj3, proj3, conv_w, conv_b.reshape(1, -1), gbias,
      hn_g.reshape(1, -1))


MERGE_ROW_GROUPS = 2


def _merge_body(x_ref, ya_ref, yb_ref, yc_ref, ga_ref, gb_ref, gc_ref, wa_ref, wb_ref, wc_ref,
                wo_ref, g_ref, o_ref):
    tm = x_ref.shape[0]
    ngrp = MERGE_ROW_GROUPS if tm % (MERGE_ROW_GROUPS * 2 * SUBLANE) == 0 else 1
    grp = [slice(r * (tm // ngrp), (r + 1) * (tm // ngrp)) for r in range(ngrp)]
    br = [(_mm1(ya_ref[sl, :], wa_ref[...]), _mm1(yb_ref[sl, :], wb_ref[...]), _mm1(yc_ref[sl, :], wc_ref[...]))
          for sl in grp]
    z = []
    for (ma, mb, mc), sl in zip(br, grp):
        merged = (jax.nn.sigmoid(ga_ref[sl, :].astype(F32)) * ma + jax.nn.sigmoid(gb_ref[sl, :].astype(F32)) * mb
                  + jax.nn.sigmoid(gc_ref[sl, :].astype(F32)) * mc)
        z.append(_mm1(merged, wo_ref[...]))
    for zr, sl in zip(z, grp):
        o_ref[sl, :] = x_ref[sl, :] + _rms(zr, g_ref[...])


def _merge(x2d, proj, ya, yb, yc, wa, wb, wc, wo, g, tm):
    t, d = x2d.shape
    row = lambda i: (i, 0)
    const = lambda i: (0, 0)
    return pl.pallas_call(
        _merge_body,
        grid=(t // tm,),
        in_specs=[
            pl.BlockSpec((tm, d), row),
            pl.BlockSpec((tm, ya.shape[1]), row),
            pl.BlockSpec((tm, yb.shape[1]), row),
            pl.BlockSpec((tm, yc.shape[1]), row),
            pl.BlockSpec((tm, d), lambda i: (i, 0)),
            pl.BlockSpec((tm, d), lambda i: (i, 1)),
            pl.BlockSpec((tm, d), lambda i: (i, 2)),
            pl.BlockSpec(wa.shape, const),
            pl.BlockSpec(wb.shape, const),
            pl.BlockSpec(wc.shape, const),
            pl.BlockSpec(wo.shape, const),
            pl.BlockSpec((1, d), const),
        ],
        out_specs=pl.BlockSpec((tm, d), row),
        out_shape=jax.ShapeDtypeStruct((t, d), F32),
        compiler_params=_cparams(("parallel",)),
        name="merge_out",
    )(x2d, ya, yb, yc, proj, proj, proj, wa, wb, wc, wo, g.reshape(1, d))


FFN_ROW_GROUPS = 2


def _ffn_body(nt_seq, n_ff, x_ref, g1_ref, upg_ref, upv_ref, cwg_ref, cwv_ref, cbg_ref, cbv_ref,
              down_ref, g2_ref, g3_ref, pg_ref, pp_ref, p_ref, o_ref, h_ref, acc_ref, ugp_ref, uvp_ref):
    i = pl.program_id(0)
    j = pl.program_id(1)
    tm = x_ref.shape[0]

    @pl.when(j == 0)
    def _():
        h_ref[...] = _rms(x_ref[...], g1_ref[...]).astype(BF16)
        acc_ref[...] = jnp.zeros_like(acc_ref)

    @pl.when(i % nt_seq == 0)
    def _():
        ugp_ref[j] = jnp.zeros(ugp_ref.shape[1:], F32)
        uvp_ref[j] = jnp.zeros(uvp_ref.shape[1:], F32)

    ngrp = FFN_ROW_GROUPS if tm % (FFN_ROW_GROUPS * SUBLANE) == 0 else 1
    rg = tm // ngrp
    upg = upg_ref[...]
    upv = upv_ref[...]
    hs = [h_ref[r * rg:(r + 1) * rg, :] for r in range(ngrp)]
    ug = [_dot(hr, upg) for hr in hs]
    uv = [_dot(hr, upv) for hr in hs]
    pg = [ugp_ref[j]] + [ug[r][rg - SUBLANE:, :] for r in range(ngrp - 1)]
    pv = [uvp_ref[j]] + [uv[r][rg - SUBLANE:, :] for r in range(ngrp - 1)]
    ugp_ref[j] = ug[-1][rg - SUBLANE:, :]
    uvp_ref[j] = uv[-1][rg - SUBLANE:, :]
    cwg = cwg_ref[...]
    cwv = cwv_ref[...]
    down = down_ref[...]
    for r in range(ngrp):
        cg = (cbg_ref[...] + cwg[2:3] * ug[r] + cwg[1:2] * _lagged(ug[r], pg[r], 1)
              + cwg[0:1] * _lagged(ug[r], pg[r], 2))
        cv = (cbv_ref[...] + cwv[2:3] * uv[r] + cwv[1:2] * _lagged(uv[r], pv[r], 1)
              + cwv[0:1] * _lagged(uv[r], pv[r], 2))
        act = jax.nn.gelu(cg, approximate=True) * cv
        acc_ref[r * rg:(r + 1) * rg, :] += _mm1(act, down)

    @pl.when(j == n_ff - 1)
    def _():
        grp = [slice(r * rg, (r + 1) * rg) for r in range(ngrp)]
        emb = [_mm1(p_ref[sl, :], pp_ref[...]) for sl in grp]
        x2, gate = [], []
        for sl in grp:
            x2.append(x_ref[sl, :] + _rms(acc_ref[sl, :], g2_ref[...]))
            gate.append(_mm1(_rms(x2[-1], g3_ref[...]), pg_ref[...]))
        for r, sl in enumerate(grp):
            o_ref[sl, :] = x2[r] + jax.nn.sigmoid(gate[r]) * emb[r]


def _ffn(x2d, p2d, s, g1, up, cw, cb, down, g2, g3, pgate, pproj, tm, tf):
    t, d = x2d.shape
    dff = down.shape[0]
    n_ff = dff // tf
    nt_seq = s // tm
    ple = p2d.shape[1]
    row = lambda i, j: (i, 0)
    const = lambda i, j: (0, 0)
    cb2 = cb.reshape(1, -1)
    return pl.pallas_call(
        functools.partial(_ffn_body, nt_seq, n_ff),
        grid=(t // tm, n_ff),
        in_specs=[
            pl.BlockSpec((tm, d), row),
            pl.BlockSpec((1, d), const),
            pl.BlockSpec((d, tf), lambda i, j: (0, j)),
            pl.BlockSpec((d, tf), lambda i, j: (0, n_ff + j)),
            pl.BlockSpec((cw.shape[0], tf), lambda i, j: (0, j)),
            pl.BlockSpec((cw.shape[0], tf), lambda i, j: (0, n_ff + j)),
            pl.BlockSpec((1, tf), lambda i, j: (0, j)),
            pl.BlockSpec((1, tf), lambda i, j: (0, n_ff + j)),
            pl.BlockSpec((tf, d), lambda i, j: (j, 0)),
            pl.BlockSpec((1, d), const),
            pl.BlockSpec((1, d), const),
            pl.BlockSpec(pgate.shape, const),
            pl.BlockSpec(pproj.shape, const),
            pl.BlockSpec((tm, ple), row),
        ],
        out_specs=pl.BlockSpec((tm, d), row),
        out_shape=jax.ShapeDtypeStruct((t, d), F32),
        scratch_shapes=[pltpu.VMEM((tm, d), BF16), pltpu.VMEM((tm, d), F32),
                        pltpu.VMEM((n_ff, SUBLANE, tf), F32), pltpu.VMEM((n_ff, SUBLANE, tf), F32)],
        compiler_params=_cparams(("arbitrary", "arbitrary")),
        name="ffn_ple",
    )(x2d, g1.reshape(1, d), up, up, cw, cw, cb2, cb2, down, g2.reshape(1, d), g3.reshape(1, d),
      pgate, pproj, p2d)


def _pack_w_in(w):
    d = w.shape[0]
    rwkv = w[:, 0:1024]
    moba = w[:, 1024:2560]
    ml = w[:, 2560:3592]
    gate = w[:, 3592:6664]
    pad = jnp.zeros((d, PACKED_WIDTH - COL_MLSTM_G - 2 * N_HEADS), w.dtype)
    packed = jnp.concatenate([gate, rwkv, moba, ml, pad], axis=1)
    return packed.astype(BF16)


def _row_tile(t, want):
    return want if t % want == 0 else t


def kernel(x, p, ln_mix_pre, ln_mix_post, ln_ffn_pre, ln_ffn_post, ln_ple, w_in, rwkv_mu, rwkv_w0, rwkv_w2, rwkv_a0, rwkv_a2, rwkv_g2, rwkv_k_k, rwkv_k_a, rwkv_r_k, rwkv_gn_g, rwkv_gn_b, rwkv_v0, rwkv_v1, rwkv_v2, mlstm_conv_w, mlstm_conv_b, mlstm_i_b, mlstm_f_b, mlstm_hn_g, w_br_rwkv, w_br_moba, w_br_mlstm, w_out, ffn_up, ffn_conv_w, ffn_conv_b, ffn_down, ple_proj, ple_gate):
    b, s, d = x.shape
    depth = w_in.shape[0]
    t = b * s
    assert d == 1024 and s % MOBA_BLOCK == 0 and w_in.shape[2] == 6664
    xf = x.reshape(t, d)
    tm_proj = _row_tile(t, 2048)
    tm_ffn = min(512, s)
    tm_merge = _row_tile(t, 1024)
    r2 = lambda a: a.reshape(1, -1)
    v_first = None
    for i in range(depth):
        proj = _norm_proj(xf, ln_mix_pre[i], _pack_w_in(w_in[i]), tm_proj, 1024)
        params = [r2(rwkv_mu[i]), r2(rwkv_w0[i]), rwkv_w2[i], r2(rwkv_a0[i]), rwkv_a2[i], rwkv_g2[i],
                  r2(rwkv_k_k[i]), r2(rwkv_k_a[i]), r2(rwkv_r_k[i]), r2(rwkv_gn_g[i]), r2(rwkv_gn_b[i])]
        if i > 0:
            params += [r2(rwkv_v0[i - 1]), rwkv_v1[i - 1], rwkv_v2[i - 1]]
        proj3 = proj.reshape(b, s, -1)
        y_a, v_cur = _rwkv(proj3, params, v_first if i > 0 else None)
        if i == 0:
            v_first = v_cur
        y_b = _moba(proj, b, s)
        y_c = _mlstm(proj3, mlstm_conv_w[i], mlstm_conv_b[i], mlstm_i_b[i], mlstm_f_b[i], mlstm_hn_g[i])
        xf = _merge(xf, proj, y_a.reshape(t, -1), y_b, y_c.reshape(t, -1), w_br_rwkv[i].astype(BF16), w_br_moba[i].astype(BF16),
                    w_br_mlstm[i].astype(BF16), w_out[i].astype(BF16), ln_mix_post[i], tm_merge)
        xf = _ffn(xf, p[i].reshape(t, -1), s, ln_ffn_pre[i], ffn_up[i].astype(BF16), ffn_conv_w[i],
                  ffn_conv_b[i], ffn_down[i].astype(BF16), ln_ffn_post[i], ln_ple[i],
                  ple_gate[i].astype(BF16), ple_proj[i].astype(BF16), tm_ffn, 1408)
    return xf.reshape(b, s, d)
```

```python
import functools

import jax
import jax.numpy as jnp
from jax import lax
from jax.experimental import pallas as pl
from jax.experimental.pallas import tpu as pltpu

F32 = jnp.float32
BF16 = jnp.bfloat16

HEAD_DIM = 64
N_HEADS = 4
RW = N_HEADS * HEAD_DIM
BATCH_TILE = 8
CHUNK = HEAD_DIM
HEAD_SHIFT = HEAD_DIM.bit_length() - 1
MOBA_BLOCK = 256
MOBA_TOPK = 3
MOBA_QTILE_BLOCKS = 2
MOBA_PAIRS = 2
MOBA_DIAG_CHAINS = 8
NORM_EPS = 1e-6
RWKV_GN_EPS = 64e-5
MASK_VALUE = -1e30
LOG2E = 1.4426950408889634
LANE = 128
SUBLANE = 8
VMEM_LIMIT = 56 * 1024 * 1024

COL_RWKV = 3072
COL_MOBA = 4096
COL_MLSTM_QK = 5632
COL_MLSTM_V = 6144
COL_MLSTM_O = 6400
COL_MLSTM_G = 6656
PACKED_WIDTH = 7168

_DIMS = {
    "nn": (((1,), (0,)), ((), ())),
    "nt": (((1,), (1,)), ((), ())),
    "tn": (((0,), (0,)), ((), ())),
}


def _dot(a, b, dims="nn"):
    return lax.dot_general(a, b, _DIMS[dims], preferred_element_type=F32)


def _split(a):
    hi = a.astype(BF16)
    lo = (a - hi.astype(F32)).astype(BF16)
    return hi, lo


def _mm1(a, b, dims="nn"):
    return _dot(a.astype(BF16), b.astype(BF16), dims)


def _mm3(a, b, dims="nn"):
    ah, al = _split(a)
    bh, bl = _split(b)
    return _dot(ah, bh, dims) + (_dot(ah, bl, dims) + _dot(al, bh, dims))


def _mm2r(a, e, dims="nn"):
    ah, al = _split(a)
    eb = e.astype(BF16)
    return _dot(ah, eb, dims) + _dot(al, eb, dims)


def _mm2l(e, a, dims="nn"):
    ah, al = _split(a)
    eb = e.astype(BF16)
    return _dot(eb, ah, dims) + _dot(eb, al, dims)


def _softplus(x):
    return jnp.maximum(x, 0.0) + jnp.log(1.0 + jnp.exp(-jnp.abs(x)))


def _rms(x, g):
    ms = jnp.mean(x * x, axis=-1, keepdims=True)
    return x * lax.rsqrt(ms + NORM_EPS) * g


def _lagged(x, prev8, lag):
    full = jnp.concatenate([prev8, x], axis=0)
    return pltpu.roll(full, lag, 0)[SUBLANE:, :]


def _tile_heads(x):
    return jnp.concatenate([x] * N_HEADS, axis=-2)


def _per(fn, *arrs):
    return jnp.stack([fn(*(a[i] for a in arrs)) for i in range(arrs[0].shape[0])])


def _rows(fn, x, w):
    lead = x.shape[:-1]
    return fn(x.reshape(-1, x.shape[-1]), w).reshape(*lead, -1)


def _bmm1(a, b, dims="nn"):
    return _per(lambda x, y: _mm1(x, y, dims), a, b)


def _batch_tile(b):
    return BATCH_TILE if b % BATCH_TILE == 0 else 1


def _same_head(n):
    ri = lax.broadcasted_iota(jnp.int32, (n, n), 0)
    ci = lax.broadcasted_iota(jnp.int32, (n, n), 1)
    return (ri >> HEAD_SHIFT) == (ci >> HEAD_SHIFT)


def _lane_head_pos(rows, n):
    lane = lax.broadcasted_iota(jnp.int32, (rows, n), 1)
    return lane >> HEAD_SHIFT, lane & (HEAD_DIM - 1)


def _cparams(sem):
    return pltpu.CompilerParams(dimension_semantics=sem, vmem_limit_bytes=VMEM_LIMIT)


PROJ_ROW_GROUPS = 4


def _proj_body(x_ref, g_ref, w_ref, o_ref, h_ref):
    tm = x_ref.shape[0]
    ngrp = PROJ_ROW_GROUPS if tm % (PROJ_ROW_GROUPS * 2 * SUBLANE) == 0 else 1
    grp = [slice(r * (tm // ngrp), (r + 1) * (tm // ngrp)) for r in range(ngrp)]

    @pl.when(pl.program_id(1) == 0)
    def _():
        w = w_ref[...]
        for sl in grp:
            h = _rms(x_ref[sl, :], g_ref[...]).astype(BF16)
            h_ref[sl, :] = h
            o_ref[sl, :] = _dot(h, w).astype(o_ref.dtype)

    @pl.when(pl.program_id(1) != 0)
    def _():
        o_ref[...] = _dot(h_ref[...], w_ref[...]).astype(o_ref.dtype)


def _norm_proj(x2d, g, w, tm, tn):
    t, d = x2d.shape
    n = w.shape[1]
    return pl.pallas_call(
        _proj_body,
        grid=(t // tm, n // tn),
        in_specs=[
            pl.BlockSpec((tm, d), lambda i, j: (i, 0)),
            pl.BlockSpec((1, d), lambda i, j: (0, 0)),
            pl.BlockSpec((d, tn), lambda i, j: (0, j)),
        ],
        out_specs=pl.BlockSpec((tm, tn), lambda i, j: (i, j)),
        out_shape=jax.ShapeDtypeStruct((t, n), BF16),
        scratch_shapes=[pltpu.VMEM((tm, d), BF16)],
        compiler_params=_cparams(("parallel", "arbitrary")),
        name="norm_proj",
    )(x2d, g.reshape(1, d), w)


def _rwkv_body(has_vres, *refs):
    if has_vres:
        (slab_ref, vf_ref, mu_ref, w0_ref, w2_ref, a0_ref, a2_ref, g2_ref, kk_ref, ka_ref,
         rk_ref, gg_ref, gb_ref, v0_ref, v1_ref, v2_ref, y_ref, vo_ref, st_ref, prev_ref) = refs
    else:
        (slab_ref, mu_ref, w0_ref, w2_ref, a0_ref, a2_ref, g2_ref, kk_ref, ka_ref,
         rk_ref, gg_ref, gb_ref, y_ref, vo_ref, st_ref, prev_ref) = refs
    c = CHUNK

    @pl.when(pl.program_id(1) == 0)
    def _():
        st_ref[...] = jnp.zeros_like(st_ref)
        prev_ref[...] = jnp.zeros_like(prev_ref)

    slab = slab_ref[...].astype(F32)
    shifted = _per(lambda x, p8: _lagged(x, p8, 1), slab, prev_ref[...])
    prev_ref[...] = slab[:, c - SUBLANE:, :]
    xs = slab + mu_ref[...] * (shifted - slab)
    r = xs[:, :, 0:RW]
    k = xs[:, :, RW:2 * RW]
    v = xs[:, :, 2 * RW:3 * RW]
    o_w = 3 * RW
    o_a = o_w + w2_ref.shape[0]
    o_g = o_a + a2_ref.shape[0]
    xw = xs[:, :, o_w:o_a]
    xa = xs[:, :, o_a:o_g]
    xg = xs[:, :, o_g:o_g + g2_ref.shape[0]]
    wlog = -_softplus(-(w0_ref[...] + _rows(_mm3, jnp.tanh(xw), w2_ref[...]))) - 0.5
    alr = jax.nn.sigmoid(a0_ref[...] + _rows(_mm1, xa, a2_ref[...]))
    g = _rows(_mm1, jax.nn.sigmoid(xg), g2_ref[...])
    if has_vres:
        mix = jax.nn.sigmoid(v0_ref[...] + _rows(_mm1, _rows(_mm1, v, v1_ref[...]), v2_ref[...]))
        v = v + (vf_ref[...] - v) * mix
    vo_ref[...] = v

    n = N_HEADS * c
    same = _same_head(n)
    hm = same.astype(F32)
    pos = lax.broadcasted_iota(jnp.int32, (c, n), 0)
    _, lane_pos = _lane_head_pos(c, n)

    kk = k * kk_ref[...]
    kk = kk / jnp.maximum(jnp.sqrt(_rows(_mm2r, kk * kk, hm)), 1e-12)
    k2 = k * (1.0 + (alr - 1.0) * ka_ref[...])

    lw = -jnp.exp(wlog)
    ti = lax.broadcasted_iota(jnp.int32, (c, c), 0)
    tj = lax.broadcasted_iota(jnp.int32, (c, c), 1)
    tril = (ti >= tj).astype(F32)
    cs = _per(lambda x: _mm2l(tril, x), lw)
    cs_end = cs[:, c - 1:c, :]
    p_in = jnp.exp(cs)
    p_ex = jnp.exp(cs - lw)
    p_inv = jnp.exp(-cs)
    p_tail = jnp.exp(cs_end - cs)

    hm16 = same.astype(BF16)

    def expand(x):
        return _tile_heads(x.astype(BF16)) * hm16

    a_c = -kk * p_ex
    r_c = r * p_in
    b_bd = expand(kk * alr * p_inv)
    k_bd = expand(k2 * p_inv)
    v_bd = expand(v)
    strict = pos > lane_pos
    incl = pos >= lane_pos
    l_ab = jnp.where(strict, _bmm1(a_c, b_bd, "nt"), 0.0)
    l_ak = jnp.where(strict, _bmm1(a_c, k_bd, "nt"), 0.0)
    m_rb = jnp.where(incl, _bmm1(r_c, b_bd, "nt"), 0.0)
    m_rk = jnp.where(incl, _bmm1(r_c, k_bd, "nt"), 0.0)

    tinv = (pos == lane_pos).astype(F32) + l_ab
    npow = l_ab
    for _ in range(5):
        npow = _bmm1(npow, expand(npow))
        tinv = tinv + _bmm1(tinv, expand(npow))

    st = st_ref[...]
    u = _bmm1(tinv, expand(_bmm1(a_c, st, "nt") + _bmm1(l_ak, v_bd)))
    y = _bmm1(r_c, st, "nt") + _bmm1(m_rb, expand(u)) + _bmm1(m_rk, v_bd)
    upd = _bmm1(u, kk * alr * p_tail, "tn") + _bmm1(v, k2 * p_tail, "tn")
    st_ref[...] = st * p_in[:, c - 1:c, :] + jnp.where(same, upd, 0.0)

    mean = _rows(_mm2r, y, hm) * (1.0 / HEAD_DIM)
    d = y - mean
    var = _rows(_mm2r, d * d, hm) * (1.0 / HEAD_DIM)
    yn = d * lax.rsqrt(var + RWKV_GN_EPS) * gg_ref[...] + gb_ref[...]
    bonus = _rows(_mm2r, r * k2 * rk_ref[...], hm) * v
    y_ref[...] = ((yn + bonus) * g).astype(y_ref.dtype)


def _rwkv(proj3, params, v_first3):
    b, s, _ = proj3.shape
    nc = s // CHUNK
    bt = _batch_tile(b)
    has_vres = v_first3 is not None
    row = lambda bi, ci: (bi, ci, 0)
    const = lambda bi, ci: (0, 0)
    in_specs = [pl.BlockSpec((bt, CHUNK, 1024), lambda bi, ci: (bi, ci, COL_RWKV // 1024))]
    args = [proj3]
    if has_vres:
        in_specs.append(pl.BlockSpec((bt, CHUNK, RW), row))
        args.append(v_first3)
    for prm in params:
        in_specs.append(pl.BlockSpec(prm.shape, const))
        args.append(prm)
    n = N_HEADS * CHUNK
    return pl.pallas_call(
        functools.partial(_rwkv_body, has_vres),
        grid=(b // bt, nc),
        in_specs=in_specs,
        out_specs=[pl.BlockSpec((bt, CHUNK, RW), row), pl.BlockSpec((bt, CHUNK, RW), row)],
        out_shape=[jax.ShapeDtypeStruct((b, s, RW), BF16), jax.ShapeDtypeStruct((b, s, RW), F32)],
        scratch_shapes=[pltpu.VMEM((bt, n, n), F32), pltpu.VMEM((bt, SUBLANE, 1024), F32)],
        compiler_params=_cparams(("parallel", "arbitrary")),
        name="rwkv7_chunk",
    )(*args)


def _moba_body(nb, n_sel, q_ref, k_ref, v_ref, o_ref, m_ref, acc_ref, qa_ref):
    bl = MOBA_BLOCK
    nh = 2 * (q_ref.shape[1] // LANE)
    qb = MOBA_QTILE_BLOCKS if nb % MOBA_QTILE_BLOCKS == 0 else 1
    qt = qb * bl
    nt = nb // qb
    nbp = -(-nb // SUBLANE) * SUBLANE
    scale = HEAD_DIM ** -0.5
    lane = lax.broadcasted_iota(jnp.int32, (bl, LANE), 1)
    head0 = lane < HEAD_DIM
    hmask = (head0, jnp.logical_not(head0))
    spare = (lane - HEAD_DIM, lane)
    blk = lax.broadcasted_iota(jnp.int32, (nbp, bl), 0)
    blk_f = blk.astype(F32)
    qpos = lax.broadcasted_iota(jnp.int32, (bl, bl), 0)
    kpos = lax.broadcasted_iota(jnp.int32, (bl, bl), 1)
    causal = kpos <= qpos
    e_row = lax.broadcasted_iota(jnp.int32, (nbp, LANE), 0)
    e_lane = lax.broadcasted_iota(jnp.int32, (nbp, LANE), 1)
    place = ((e_lane == e_row + HEAD_DIM).astype(BF16), (e_lane == e_row).astype(BF16))
    klane = e_lane < HEAD_DIM

    def pair(h):
        return slice((h // 2) * LANE, (h // 2 + 1) * LANE)

    kmean = jnp.mean(k_ref[...].astype(F32).reshape(nb, bl, k_ref.shape[1]), axis=1)
    if nbp > nb:
        kmean = jnp.concatenate([kmean, jnp.zeros((nbp - nb, kmean.shape[1]), F32)], axis=0)
    kmean_h = [jnp.where(klane == (h % 2 == 0), kmean[:, pair(h)], 0.0) for h in range(nh)]

    def rows(i):
        if isinstance(i, int):
            return pl.ds(i * bl, bl)
        return pl.ds(pl.multiple_of(i * bl, bl), bl)

    def kv_tiles(j):
        kp, vp = [], []
        for h in range(nh):
            kb = k_ref[rows(j), pair(h)].astype(F32) * (scale * LOG2E)
            vb = v_ref[rows(j), pair(h)]
            kp.append(jnp.where(hmask[h % 2], kb, (spare[h % 2] == j).astype(F32)).astype(BF16))
            vp.append(jnp.where(hmask[h % 2], vb, 1.0).astype(BF16))
        return kp, vp

    dg = max(1, MOBA_DIAG_CHAINS // nh)
    while nb % dg:
        dg -= 1

    def diag_body(ii, carry):
        chains = [(g, h) for g in range(dg) for h in range(nh)]
        kv = [kv_tiles(ii * dg + g) for g in range(dg)]
        qf = [q_ref[rows(ii * dg + g), pair(h)].astype(F32) for g, h in chains]
        qh = [jnp.where(hmask[h % 2], qf[c], 0.0) for c, (g, h) in enumerate(chains)]
        s = [_dot(qh[c].astype(BF16), kv[g][0][h], "nt") for c, (g, h) in enumerate(chains)]
        bs = [jnp.where(blk < ii * dg + g, _mm3(kmean_h[h], qf[c], "nt"), MASK_VALUE)
              for c, (g, h) in enumerate(chains)]
        sel_t = []
        for c, (g, h) in enumerate(chains):
            work = bs[c]
            sel = jnp.zeros((nbp, bl), jnp.bool_)
            for _ in range(n_sel):
                best = jnp.max(work, axis=0, keepdims=True)
                first = jnp.min(jnp.where(work == best, blk_f, float(nbp)), axis=0, keepdims=True)
                hit = blk_f == first
                sel = sel | hit
                work = jnp.where(hit, -jnp.inf, work)
            sel_t.append((sel & (blk < ii * dg + g)).astype(BF16))
        picked = [_dot(sel_t[c], place[h % 2], "tn") for c, (g, h) in enumerate(chains)]
        for c, (g, h) in enumerate(chains):
            i = ii * dg + g
            sm = jnp.where(causal, s[c], MASK_VALUE)
            m = jnp.max(sm, axis=1, keepdims=True)
            p = jnp.exp2(sm - m)
            m_ref[h, rows(i), :] = jnp.broadcast_to(m, (bl, LANE))
            acc_ref[h, rows(i), :] = _dot(p.astype(BF16), kv[g][1][h])
            sp = spare[h % 2]
            bias = jnp.where((sp >= 0) & (sp < nb) & (picked[c] < 0.5), MASK_VALUE, 0.0)
            qa_ref[h, rows(i), :] = (qh[c] + bias).astype(BF16)
        return carry

    lax.fori_loop(0, nb // dg, diag_body, 0)

    def scores(t, h, kp):
        return _dot(qa_ref[h, pl.ds(t * qt, qt), :], kp[h], "nt")

    def update(t, h, s, vp):
        rs = pl.ds(t * qt, qt)
        m_prev = m_ref[h, rs, :]
        m_new = jnp.maximum(m_prev, jnp.max(s, axis=1, keepdims=True))
        p = jnp.exp2(s - jnp.concatenate([m_new, m_new], axis=1))
        m_ref[h, rs, :] = m_new
        acc_ref[h, rs, :] = acc_ref[h, rs, :] * jnp.exp2(m_prev - m_new) + _dot(p.astype(BF16), vp[h])

    for t0 in range(nt):
        def key_body(j, carry, t0=t0):
            kp, vp = kv_tiles(j)
            items = [(t, h) for t in range(t0, nt) for h in range(nh)]
            s_next = scores(*items[0], kp)
            for idx, (t, h) in enumerate(items):
                s = s_next
                if idx + 1 < len(items):
                    s_next = scores(*items[idx + 1], kp)
                update(t, h, s, vp)
            return carry

        lax.fori_loop(max(0, t0 * qb - 1), min(nb - 1, (t0 + 1) * qb - 1), key_body, 0)

    def out_body(t, carry):
        rs = pl.ds(pl.multiple_of(t * qt, qt), qt)
        first = lax.broadcasted_iota(jnp.int32, (qt, LANE), 1) < HEAD_DIM
        for h in range(0, nh, 2):
            a0 = acc_ref[h, rs, :]
            a1 = acc_ref[h + 1, rs, :]
            out = jnp.where(first, a0 / pltpu.roll(a0, HEAD_DIM, 1), a1 / pltpu.roll(a1, HEAD_DIM, 1))
            o_ref[rs, pair(h)] = out.astype(o_ref.dtype)
        return carry

    lax.fori_loop(0, nt, out_body, 0)


def _moba(proj, b, s):
    t = proj.shape[0]
    nb = s // MOBA_BLOCK
    n_sel = min(MOBA_TOPK, nb - 1)
    w = MOBA_PAIRS * LANE
    ngrp = 512 // w
    qc, kc, vc = COL_MOBA // w, (COL_MOBA + 512) // w, (COL_MOBA + 1024) // w
    nh = 2 * MOBA_PAIRS
    return pl.pallas_call(
        functools.partial(_moba_body, nb, n_sel),
        grid=(b, ngrp),
        in_specs=[
            pl.BlockSpec((s, w), lambda bi, g: (bi, qc + g)),
            pl.BlockSpec((s, w), lambda bi, g: (bi, kc + g)),
            pl.BlockSpec((s, w), lambda bi, g: (bi, vc + g)),
        ],
        out_specs=pl.BlockSpec((s, w), lambda bi, g: (bi, g)),
        out_shape=jax.ShapeDtypeStruct((t, 512), BF16),
        scratch_shapes=[pltpu.VMEM((nh, s, LANE), F32), pltpu.VMEM((nh, s, LANE), F32),
                        pltpu.VMEM((nh, s, LANE), BF16)],
        compiler_params=_cparams(("parallel", "parallel")),
        name="moba_attn",
    )(proj, proj, proj)


def _mlstm_body(qk_ref, v_ref, o_ref, gate_ref, cw_ref, cb_ref, gbias_ref, hng_ref,
                y_ref, cst_ref, n_ref, m_ref, prev_ref):
    c = CHUNK
    n = N_HEADS * c

    @pl.when(pl.program_id(1) == 0)
    def _():
        cst_ref[...] = jnp.zeros_like(cst_ref)
        n_ref[...] = jnp.zeros_like(n_ref)
        m_ref[...] = jnp.zeros_like(m_ref)
        prev_ref[...] = jnp.zeros_like(prev_ref)

    x = qk_ref[...].astype(F32)
    prev8 = prev_ref[...]
    cw = cw_ref[...]
    lag = lambda j: _per(lambda xx, p8: _lagged(xx, p8, j), x, prev8)
    conv = cb_ref[...] + cw[3:4] * x + cw[2:3] * lag(1) + cw[1:2] * lag(2) + cw[0:1] * lag(3)
    prev_ref[...] = x[:, c - SUBLANE:, :]
    qk = conv * jax.nn.sigmoid(conv)
    q = qk[:, :, 0:RW]
    k = qk[:, :, RW:2 * RW] * (HEAD_DIM ** -0.5)
    v = v_ref[...].astype(F32)

    same = _same_head(n)
    hm = same.astype(F32)
    hm16 = same.astype(BF16)
    k_mt = _tile_heads(k.astype(BF16)) * hm16
    v_mt = _tile_heads(v.astype(BF16)) * hm16

    g = gate_ref[...].astype(F32) + gbias_ref[...]
    glane = lax.broadcasted_iota(jnp.int32, (c, LANE), 1)
    lg = jnp.where(glane < N_HEADS, g, -_softplus(-g))
    ti = lax.broadcasted_iota(jnp.int32, (c, c), 0)
    tj = lax.broadcasted_iota(jnp.int32, (c, c), 1)
    tril = (ti >= tj).astype(F32)
    cum = _per(lambda z: _mm2l(tril, z), lg)
    src = lax.broadcasted_iota(jnp.int32, (LANE, n), 0)
    dst_head, _ = _lane_head_pos(LANE, n)
    li_e = _rows(_mm2r, lg, (src == dst_head).astype(F32))
    b_e = _rows(_mm2r, cum, (src == dst_head + N_HEADS).astype(F32))
    pos = lax.broadcasted_iota(jnp.int32, (c, n), 0)
    lane_head, lane_pos = _lane_head_pos(c, n)
    li_row = jnp.sum(jnp.where(pos == lane_pos, li_e, 0.0), axis=1, keepdims=True)
    b_row = jnp.sum(jnp.where(pos == lane_pos, b_e, 0.0), axis=1, keepdims=True)
    bend = b_e[:, c - 1:c, :]

    m_row = m_ref[...]
    dmat = jnp.where(pos >= lane_pos, b_e - b_row + li_row, -jnp.inf)
    inter = b_e + m_row
    m_t = inter
    for h in range(N_HEADS):
        in_head = lane_head == h
        mh = jnp.max(jnp.where(in_head, dmat, -jnp.inf), axis=2, keepdims=True)
        m_t = jnp.where(in_head, jnp.maximum(m_t, mh), m_t)
    wts = jnp.exp(dmat - m_t)
    s_inter = jnp.exp(inter - m_t)
    cst = cst_ref[...]
    n_row = n_ref[...]
    qk_w = _bmm1(q, k_mt, "nt") * wts
    num = s_inter * _bmm1(q, cst, "nt") + _bmm1(qk_w, v_mt)
    den = _rows(_mm2r, s_inter * (q * n_row) + qk_w, hm)
    y = num / jnp.maximum(jnp.abs(den), jnp.exp(-m_t))

    g_e = bend - b_e + li_e
    m_new = jnp.maximum(bend + m_row, jnp.max(g_e, axis=1, keepdims=True))
    w_e = jnp.exp(g_e - m_new)
    scale = jnp.exp(bend + m_row - m_new)
    cst_ref[...] = scale * cst + jnp.where(same, _bmm1(v * w_e, k, "tn"), 0.0)
    n_ref[...] = scale * n_row + jnp.sum(k * w_e, axis=1, keepdims=True)
    m_ref[...] = m_new

    mean = _rows(_mm2r, y, hm) * (1.0 / HEAD_DIM)
    d = y - mean
    var = _rows(_mm2r, d * d, hm) * (1.0 / HEAD_DIM)
    y = d * lax.rsqrt(var + NORM_EPS) * hng_ref[...] * jax.nn.sigmoid(o_ref[...].astype(F32))
    y_ref[...] = y.astype(y_ref.dtype)


def _mlstm(proj3, conv_w, conv_b, i_b, f_b, hn_g):
    b, s, _ = proj3.shape
    c = CHUNK
    nc = s // c
    n = N_HEADS * c
    bt = _batch_tile(b)
    gbias = jnp.concatenate([i_b, f_b, jnp.zeros((LANE - 2 * N_HEADS,), F32)]).reshape(1, LANE)
    row = lambda bi, ci: (bi, ci, 0)
    const = lambda bi, ci: (0, 0)
    return pl.pallas_call(
        _mlstm_body,
        grid=(b // bt, nc),
        in_specs=[
            pl.BlockSpec((bt, c, 2 * RW), lambda bi, ci: (bi, ci, COL_MLSTM_QK // (2 * RW))),
            pl.BlockSpec((bt, c, RW), lambda bi, ci: (bi, ci, COL_MLSTM_V // RW)),
            pl.BlockSpec((bt, c, RW), lambda bi, ci: (bi, ci, COL_MLSTM_O // RW)),
            pl.BlockSpec((bt, c, LANE), lambda bi, ci: (bi, ci, COL_MLSTM_G // LANE)),
            pl.BlockSpec(conv_w.shape, const),
            pl.BlockSpec((1, 2 * RW), const),
            pl.BlockSpec((1, LANE), const),
            pl.BlockSpec((1, RW), const),
        ],
        out_specs=pl.BlockSpec((bt, c, RW), row),
        out_shape=jax.ShapeDtypeStruct((b, s, RW), BF16),
        scratch_shapes=[pltpu.VMEM((bt, n, n), F32), pltpu.VMEM((bt, 1, n), F32),
                        pltpu.VMEM((bt, 1, n), F32), pltpu.VMEM((bt, SUBLANE, 2 * RW), F32)],
        compiler_params=_cparams(("parallel", "arbitrary")),
        name="mlstm_chunk",
    )(proj3, proj3, proj3, proj3, conv_w, conv_b.reshape(1, -1), gbias,
      hn_g.reshape(1, -1))


MERGE_ROW_GROUPS = 2


def _merge_body(x_ref, ya_ref, yb_ref, yc_ref, ga_ref, gb_ref, gc_ref, wa_ref, wb_ref, wc_ref,
                wo_ref, g_ref, o_ref):
    tm = x_ref.shape[0]
    ngrp = MERGE_ROW_GROUPS if tm % (MERGE_ROW_GROUPS * 2 * SUBLANE) == 0 else 1
    grp = [slice(r * (tm // ngrp), (r + 1) * (tm // ngrp)) for r in range(ngrp)]
    br = [(_mm1(ya_ref[sl, :], wa_ref[...]), _mm1(yb_ref[sl, :], wb_ref[...]), _mm1(yc_ref[sl, :], wc_ref[...]))
          for sl in grp]
    z = []
    for (ma, mb, mc), sl in zip(br, grp):
        merged = (jax.nn.sigmoid(ga_ref[sl, :].astype(F32)) * ma + jax.nn.sigmoid(gb_ref[sl, :].astype(F32)) * mb
                  + jax.nn.sigmoid(gc_ref[sl, :].astype(F32)) * mc)
        z.append(_mm1(merged, wo_ref[...]))
    for zr, sl in zip(z, grp):
        o_ref[sl, :] = x_ref[sl, :] + _rms(zr, g_ref[...])


def _merge(x2d, proj, ya, yb, yc, wa, wb, wc, wo, g, tm):
    t, d = x2d.shape
    row = lambda i: (i, 0)
    const = lambda i: (0, 0)
    return pl.pallas_call(
        _merge_body,
        grid=(t // tm,),
        in_specs=[
            pl.BlockSpec((tm, d), row),
            pl.BlockSpec((tm, ya.shape[1]), row),
            pl.BlockSpec((tm, yb.shape[1]), row),
            pl.BlockSpec((tm, yc.shape[1]), row),
            pl.BlockSpec((tm, d), lambda i: (i, 0)),
            pl.BlockSpec((tm, d), lambda i: (i, 1)),
            pl.BlockSpec((tm, d), lambda i: (i, 2)),
            pl.BlockSpec(wa.shape, const),
            pl.BlockSpec(wb.shape, const),
            pl.BlockSpec(wc.shape, const),
            pl.BlockSpec(wo.shape, const),
            pl.BlockSpec((1, d), const),
        ],
        out_specs=pl.BlockSpec((tm, d), row),
        out_shape=jax.ShapeDtypeStruct((t, d), F32),
        compiler_params=_cparams(("parallel",)),
        name="merge_out",
    )(x2d, ya, yb, yc, proj, proj, proj, wa, wb, wc, wo, g.reshape(1, d))


FFN_ROW_GROUPS = 2


def _ffn_body(nt_seq, n_ff, x_ref, g1_ref, upg_ref, upv_ref, cwg_ref, cwv_ref, cbg_ref, cbv_ref,
              down_ref, g2_ref, g3_ref, pg_ref, pp_ref, p_ref, o_ref, h_ref, acc_ref, ugp_ref, uvp_ref):
    i = pl.program_id(0)
    j = pl.program_id(1)
    tm = x_ref.shape[0]

    @pl.when(j == 0)
    def _():
        h_ref[...] = _rms(x_ref[...], g1_ref[...]).astype(BF16)
        acc_ref[...] = jnp.zeros_like(acc_ref)

    @pl.when(i % nt_seq == 0)
    def _():
        ugp_ref[j] = jnp.zeros(ugp_ref.shape[1:], F32)
        uvp_ref[j] = jnp.zeros(uvp_ref.shape[1:], F32)

    ngrp = FFN_ROW_GROUPS if tm % (FFN_ROW_GROUPS * SUBLANE) == 0 else 1
    rg = tm // ngrp
    upg = upg_ref[...]
    upv = upv_ref[...]
    hs = [h_ref[r * rg:(r + 1) * rg, :] for r in range(ngrp)]
    ug = [_dot(hr, upg) for hr in hs]
    uv = [_dot(hr, upv) for hr in hs]
    pg = [ugp_ref[j]] + [ug[r][rg - SUBLANE:, :] for r in range(ngrp - 1)]
    pv = [uvp_ref[j]] + [uv[r][rg - SUBLANE:, :] for r in range(ngrp - 1)]
    ugp_ref[j] = ug[-1][rg - SUBLANE:, :]
    uvp_ref[j] = uv[-1][rg - SUBLANE:, :]
    cwg = cwg_ref[...]
    cwv = cwv_ref[...]
    down = down_ref[...]
    for r in range(ngrp):
        cg = (cbg_ref[...] + cwg[2:3] * ug[r] + cwg[1:2] * _lagged(ug[r], pg[r], 1)
              + cwg[0:1] * _lagged(ug[r], pg[r], 2))
        cv = (cbv_ref[...] + cwv[2:3] * uv[r] + cwv[1:2] * _lagged(uv[r], pv[r], 1)
              + cwv[0:1] * _lagged(uv[r], pv[r], 2))
        act = jax.nn.gelu(cg, approximate=True) * cv
        acc_ref[r * rg:(r + 1) * rg, :] += _mm1(act, down)

    @pl.when(j == n_ff - 1)
    def _():
        grp = [slice(r * rg, (r + 1) * rg) for r in range(ngrp)]
        emb = [_mm1(p_ref[sl, :], pp_ref[...]) for sl in grp]
        x2, gate = [], []
        for sl in grp:
            x2.append(x_ref[sl, :] + _rms(acc_ref[sl, :], g2_ref[...]))
            gate.append(_mm1(_rms(x2[-1], g3_ref[...]), pg_ref[...]))
        for r, sl in enumerate(grp):
            o_ref[sl, :] = x2[r] + jax.nn.sigmoid(gate[r]) * emb[r]


def _ffn(x2d, p2d, s, g1, up, cw, cb, down, g2, g3, pgate, pproj, tm, tf):
    t, d = x2d.shape
    dff = down.shape[0]
    n_ff = dff // tf
    nt_seq = s // tm
    ple = p2d.shape[1]
    row = lambda i, j: (i, 0)
    const = lambda i, j: (0, 0)
    cb2 = cb.reshape(1, -1)
    return pl.pallas_call(
        functools.partial(_ffn_body, nt_seq, n_ff),
        grid=(t // tm, n_ff),
        in_specs=[
            pl.BlockSpec((tm, d), row),
            pl.BlockSpec((1, d), const),
            pl.BlockSpec((d, tf), lambda i, j: (0, j)),
            pl.BlockSpec((d, tf), lambda i, j: (0, n_ff + j)),
            pl.BlockSpec((cw.shape[0], tf), lambda i, j: (0, j)),
            pl.BlockSpec((cw.shape[0], tf), lambda i, j: (0, n_ff + j)),
            pl.BlockSpec((1, tf), lambda i, j: (0, j)),
            pl.BlockSpec((1, tf), lambda i, j: (0, n_ff + j)),
            pl.BlockSpec((tf, d), lambda i, j: (j, 0)),
            pl.BlockSpec((1, d), const),
            pl.BlockSpec((1, d), const),
            pl.BlockSpec(pgate.shape, const),
            pl.BlockSpec(pproj.shape, const),
            pl.BlockSpec((tm, ple), row),
        ],
        out_specs=pl.BlockSpec((tm, d), row),
        out_shape=jax.ShapeDtypeStruct((t, d), F32),
        scratch_shapes=[pltpu.VMEM((tm, d), BF16), pltpu.VMEM((tm, d), F32),
                        pltpu.VMEM((n_ff, SUBLANE, tf), F32), pltpu.VMEM((n_ff, SUBLANE, tf), F32)],
        compiler_params=_cparams(("arbitrary", "arbitrary")),
        name="ffn_ple",
    )(x2d, g1.reshape(1, d), up, up, cw, cw, cb2, cb2, down, g2.reshape(1, d), g3.reshape(1, d),
      pgate, pproj, p2d)


def _pack_w_in(w):
    d = w.shape[0]
    rwkv = w[:, 0:1024]
    moba = w[:, 1024:2560]
    ml = w[:, 2560:3592]
    gate = w[:, 3592:6664]
    pad = jnp.zeros((d, PACKED_WIDTH - COL_MLSTM_G - 2 * N_HEADS), w.dtype)
    packed = jnp.concatenate([gate, rwkv, moba, ml, pad], axis=1)
    return packed.astype(BF16)


def _row_tile(t, want):
    return want if t % want == 0 else t


def kernel(x, p, ln_mix_pre, ln_mix_post, ln_ffn_pre, ln_ffn_post, ln_ple, w_in, rwkv_mu, rwkv_w0, rwkv_w2, rwkv_a0, rwkv_a2, rwkv_g2, rwkv_k_k, rwkv_k_a, rwkv_r_k, rwkv_gn_g, rwkv_gn_b, rwkv_v0, rwkv_v1, rwkv_v2, mlstm_conv_w, mlstm_conv_b, mlstm_i_b, mlstm_f_b, mlstm_hn_g, w_br_rwkv, w_br_moba, w_br_mlstm, w_out, ffn_up, ffn_conv_w, ffn_conv_b, ffn_down, ple_proj, ple_gate):
    b, s, d = x.shape
    depth = w_in.shape[0]
    t = b * s
    assert d == 1024 and s % MOBA_BLOCK == 0 and w_in.shape[2] == 6664
    xf = x.reshape(t, d)
    tm_proj = _row_tile(t, 2048)
    tm_ffn = min(512, s)
    tm_merge = _row_tile(t, 1024)
    r2 = lambda a: a.reshape(1, -1)
    v_first = None
    for i in range(depth):
        proj = _norm_proj(xf, ln_mix_pre[i], _pack_w_in(w_in[i]), tm_proj, 1024)
        params = [r2(rwkv_mu[i]), r2(rwkv_w0[i]), rwkv_w2[i], r2(rwkv_a0[i]), rwkv_a2[i], rwkv_g2[i],
                  r2(rwkv_k_k[i]), r2(rwkv_k_a[i]), r2(rwkv_r_k[i]), r2(rwkv_gn_g[i]), r2(rwkv_gn_b[i])]
        if i > 0:
            params += [r2(rwkv_v0[i - 1]), rwkv_v1[i - 1], rwkv_v2[i - 1]]
        proj3 = proj.reshape(b, s, -1)
        y_a, v_cur = _rwkv(proj3, params, v_first if i > 0 else None)
        if i == 0:
            v_first = v_cur
        y_b = _moba(proj, b, s)
        y_c = _mlstm(proj3, mlstm_conv_w[i], mlstm_conv_b[i], mlstm_i_b[i], mlstm_f_b[i], mlstm_hn_g[i])
        xf = _merge(xf, proj, y_a.reshape(t, -1), y_b, y_c.reshape(t, -1), w_br_rwkv[i].astype(BF16), w_br_moba[i].astype(BF16),
                    w_br_mlstm[i].astype(BF16), w_out[i].astype(BF16), ln_mix_post[i], tm_merge)
        xf = _ffn(xf, p[i].reshape(t, -1), s, ln_ffn_pre[i], ffn_up[i].astype(BF16), ffn_conv_w[i],
                  ffn_conv_b[i], ffn_down[i].astype(BF16), ln_ffn_post[i], ln_ple[i],
                  ple_gate[i].astype(BF16), ple_proj[i].astype(BF16), tm_ffn, 1408)
    return xf.reshape(b, s, d)
```
